```python
import jax, jax.numpy as jnp
from jax import lax
import numpy as np

D_MODEL = 4096
BATCH = 2
SEQ = 8192
DEPTH = 1

N_HEADS_A = 16
HEAD_DIM_A = 128
N_KV_A = 4
D_A = N_HEADS_A * HEAD_DIM_A
KV_DIM = N_KV_A * HEAD_DIM_A
N_HEADS_IDX = 16
HEAD_DIM_IDX = 128
D_IDX_Q = N_HEADS_IDX * HEAD_DIM_IDX
TOPK_MAX = 256
Q_BLOCK = 128
ROPE_THETA = 10000.0
HEAD_DIM_B = 64
D_B = 2048
N_HEADS_B = D_B // HEAD_DIM_B
LORA_DECAY = 96
LORA_A = 96
LORA_GATE = 256
GN_EPS = 64e-5
IN_SIZES = (D_A, KV_DIM, KV_DIM, D_IDX_Q, HEAD_DIM_IDX, N_HEADS_IDX, D_B, D_B, D_B)
D_IN = D_A + 2 * KV_DIM + D_IDX_Q + HEAD_DIM_IDX + N_HEADS_IDX + 3 * D_B
D_FF = ((8 * D_MODEL // 3 + 255) // 256) * 256
D_PLE = 256
RMS_EPS = 1e-6

kernel_name = "hybrid_dsa_rwkv7_gated_block"


def _rms(x):
    x32 = x.astype(jnp.float32)
    return (x32 * lax.rsqrt(jnp.mean(x32 * x32, axis=-1, keepdims=True) + RMS_EPS)).astype(x.dtype)


def rmsnorm(x, g):
    return _rms(x) * g


def rope(t, positions):
    d = t.shape[-1]
    inv_freq = ROPE_THETA ** (-jnp.arange(0, d, 2, dtype=jnp.float32) / d)
    ang = positions.astype(jnp.float32)[..., None] * inv_freq
    cos = jnp.cos(ang)[:, :, None, :].astype(t.dtype)
    sin = jnp.sin(ang)[:, :, None, :].astype(t.dtype)
    t1, t2 = t[..., : d // 2], t[..., d // 2:]
    return jnp.concatenate([t1 * cos - t2 * sin, t1 * sin + t2 * cos], axis=-1)


def token_shift(t):
    return jnp.pad(t, ((0, 0), (1, 0), (0, 0)))[:, :-1]


def split_in(proj):
    outs, start = [], 0
    for size in IN_SIZES:
        outs.append(proj[..., start:start + size])
        start += size
    return outs


def dsa_attention(q, k, v, q_idx, k_idx, w_idx):
    B, S = q.shape[0], q.shape[1]
    topk = min(TOPK_MAX, S // 4)
    nb = S // Q_BLOCK
    key_pos = jnp.arange(S)

    def to_blocks(t):
        return jnp.moveaxis(t.reshape((B, nb, Q_BLOCK) + t.shape[2:]), 1, 0)

    def one_block(args):
        blk, q_blk, qi_blk, wi_blk = args
        q_pos = blk * Q_BLOCK + jnp.arange(Q_BLOCK)
        causal = key_pos[None, :] <= q_pos[:, None]
        logits = jnp.einsum('bqhd,bsd->bqhs', qi_blk, k_idx) * (HEAD_DIM_IDX ** -0.5)
        score = jnp.einsum('bqh,bqhs->bqs', wi_blk, jax.nn.relu(logits)).astype(jnp.float32)
        score = jnp.where(causal[None], score, -jnp.inf)
        _, idx = lax.top_k(score, topk)
        k_sel = jax.vmap(lambda kk, ii: kk[ii])(k, idx)
        v_sel = jax.vmap(lambda vv, ii: vv[ii])(v, idx)
        qg = q_blk.reshape(B, Q_BLOCK, N_KV_A, N_HEADS_A // N_KV_A, HEAD_DIM_A)
        s = jnp.einsum('bqgnd,bqkgd->bqgnk', qg, k_sel).astype(jnp.float32) * (HEAD_DIM_A ** -0.5)
        valid = idx <= q_pos[None, :, None]
        s = jnp.where(valid[:, :, None, None, :], s, -jnp.inf)
        prob = jax.nn.softmax(s, axis=-1).astype(v.dtype)
        o = jnp.einsum('bqgnk,bqkgd->bqgnd', prob, v_sel)
        return o.reshape(B, Q_BLOCK, D_A)

    out = lax.map(one_block, (jnp.arange(nb), to_blocks(q), to_blocks(q_idx), to_blocks(w_idx)))
    return jnp.moveaxis(out, 0, 1).reshape(B, S, D_A)


def rwkv7_time_mix(xn, r, k, v, mu_rkv, mu_wag, w0, w1, w2, a0, a1, a2, g1, g2,
                   k_k, k_a, r_k, ln_w, ln_b):
    B, S, _ = xn.shape
    dt = xn.dtype
    r = r + (token_shift(r) - r) * mu_rkv[0]
    k = k + (token_shift(k) - k) * mu_rkv[1]
    v = v + (token_shift(v) - v) * mu_rkv[2]
    xx = token_shift(xn) - xn
    xw = xn + xx * mu_wag[0]
    xa = xn + xx * mu_wag[1]
    xg = xn + xx * mu_wag[2]
    w = -jax.nn.softplus(-(w0 + jnp.tanh(xw @ w1) @ w2)) - 0.5
    decay = jnp.exp(-jnp.exp(w.astype(jnp.float32)))
    a = jax.nn.sigmoid(a0 + (xa @ a1) @ a2)
    g = jax.nn.sigmoid(xg @ g1) @ g2
    kk = (k * k_k).astype(jnp.float32).reshape(B, S, N_HEADS_B, HEAD_DIM_B)
    kk = kk / jnp.maximum(jnp.linalg.norm(kk, axis=-1, keepdims=True), 1e-12)
    k = k * (1.0 + (a - 1.0) * k_a)

    def heads(t):
        return jnp.moveaxis(t.astype(jnp.float32).reshape(B, S, N_HEADS_B, HEAD_DIM_B), 1, 0)

    a_h = a.astype(jnp.float32).reshape(B, S, N_HEADS_B, HEAD_DIM_B)
    xs = (heads(r), heads(decay), heads(k), heads(v),
          jnp.moveaxis(-kk, 1, 0), jnp.moveaxis(kk * a_h, 1, 0))

    def step(state, inp):
        r_t, w_t, k_t, v_t, a_t, b_t = inp
        sa = jnp.einsum('bhij,bhj->bhi', state, a_t)
        state = (state * w_t[:, :, None, :] + sa[..., None] * b_t[:, :, None, :]
                 + v_t[..., None] * k_t[:, :, None, :])
        return state, jnp.einsum('bhij,bhj->bhi', state, r_t)

    state0 = jnp.zeros((B, N_HEADS_B, HEAD_DIM_B, HEAD_DIM_B), jnp.float32)
    _, y = lax.scan(step, state0, xs)
    y = jnp.moveaxis(y, 0, 1)
    mu = jnp.mean(y, axis=-1, keepdims=True)
    var = jnp.mean(jnp.square(y - mu), axis=-1, keepdims=True)
    y = ((y - mu) * lax.rsqrt(var + GN_EPS)).reshape(B, S, D_B).astype(dt) * ln_w + ln_b
    rh = r.reshape(B, S, N_HEADS_B, HEAD_DIM_B)
    kh = k.reshape(B, S, N_HEADS_B, HEAD_DIM_B)
    vh = v.reshape(B, S, N_HEADS_B, HEAD_DIM_B)
    bonus = jnp.sum(rh * kh * r_k, axis=-1, keepdims=True) * vh
    return (y + bonus.reshape(B, S, D_B)) * g


def setup_inputs(seed: int = 0) -> dict:
    key = jax.random.key(seed)
    ks = jax.random.split(key, 40)
    L, D = DEPTH, D_MODEL

    def nrm(k, shape, scale):
        return jax.random.normal(k, shape, jnp.float32) * scale

    def uni(k, shape):
        return jax.random.uniform(k, shape, jnp.float32)

    return {
        "x": nrm(ks[0], (BATCH, SEQ, D), 1.0),
        "p": nrm(ks[1], (L, BATCH, SEQ, D_PLE), 1.0),
        "positions": jnp.broadcast_to(jnp.arange(SEQ, dtype=jnp.int32), (BATCH, SEQ)),
        "norm_mix": 1.0 + nrm(ks[2], (L, D), 0.02),
        "w_in": nrm(ks[3], (L, D, D_IN), D ** -0.5),
        "mu_rkv": uni(ks[4], (L, 3, D_B)),
        "mu_wag": uni(ks[5], (L, 3, D)),
        "w0": nrm(ks[6], (L, D_B), 0.5),
        "w1": nrm(ks[7], (L, D, LORA_DECAY), D ** -0.5),
        "w2": nrm(ks[8], (L, LORA_DECAY, D_B), 0.1 * LORA_DECAY ** -0.5),
        "a0": nrm(ks[9], (L, D_B), 0.1),
        "a1": nrm(ks[10], (L, D, LORA_A), D ** -0.5),
        "a2": nrm(ks[11], (L, LORA_A, D_B), 0.1 * LORA_A ** -0.5),
        "g1": nrm(ks[12], (L, D, LORA_GATE), D ** -0.5),
        "g2": nrm(ks[13], (L, LORA_GATE, D_B), LORA_GATE ** -0.5),
        "k_k": 0.85 + nrm(ks[14], (L, D_B), 0.02),
        "k_a": 1.0 + nrm(ks[15], (L, D_B), 0.02),
        "r_k": nrm(ks[16], (L, N_HEADS_B, HEAD_DIM_B), 0.05),
        "ln_w": 1.0 + nrm(ks[17], (L, D_B), 0.02),
        "ln_b": nrm(ks[18], (L, D_B), 0.02),
        "w_pa": nrm(ks[19], (L, D_A, D), D_A ** -0.5),
        "w_pb": nrm(ks[20], (L, D_B, D), D_B ** -0.5),
        "w_gate": nrm(ks[21], (L, D, 2 * D), D ** -0.5),
        "b_gate": nrm(ks[22], (L, 2 * D), 0.02),
        "w_o": nrm(ks[23], (L, D, D), D ** -0.5),
        "norm_ffn": 1.0 + nrm(ks[24], (L, D), 0.02),
        "w_ffn1": nrm(ks[25], (L, D, D_FF), D ** -0.5),
        "w_ffn3": nrm(ks[26], (L, D, D_FF), D ** -0.5),
        "w_ffn2": nrm(ks[27], (L, D_FF, D), D_FF ** -0.5),
        "w_ple_gate": nrm(ks[28], (L, D, D), D ** -0.5),
        "w_ple": nrm(ks[29], (L, D_PLE, D), D_PLE ** -0.5),
        "norm_final": 1.0 + nrm(ks[30], (D,), 0.02),
    }


def reference(x, p, positions, norm_mix, w_in, mu_rkv, mu_wag, w0, w1, w2, a0, a1, a2,
              g1, g2, k_k, k_a, r_k, ln_w, ln_b, w_pa, w_pb, w_gate, b_gate, w_o,
              norm_ffn, w_ffn1, w_ffn3, w_ffn2, w_ple_gate, w_ple, norm_final):
    B, S, _ = x.shape
    h = x
    for i in range(DEPTH):
        xn = rmsnorm(h, norm_mix[i])
        q, k, v, qi, ki, wi, rb, kb, vb = split_in(xn @ w_in[i])
        q = rope(q.reshape(B, S, N_HEADS_A, HEAD_DIM_A), positions)
        k = rope(k.reshape(B, S, N_KV_A, HEAD_DIM_A), positions)
        v = v.reshape(B, S, N_KV_A, HEAD_DIM_A)
        qi = rope(qi.reshape(B, S, N_HEADS_IDX, HEAD_DIM_IDX), positions)
        ki = rope(ki[:, :, None, :], positions)[:, :, 0, :]
        wi = wi * (N_HEADS_IDX ** -0.5)
        y_a = dsa_attention(q, k, v, qi, ki, wi) @ w_pa[i]
        y_b = rwkv7_time_mix(xn, rb, kb, vb, mu_rkv[i], mu_wag[i], w0[i], w1[i], w2[i],
                             a0[i], a1[i], a2[i], g1[i], g2[i], k_k[i], k_a[i], r_k[i],
                             ln_w[i], ln_b[i]) @ w_pb[i]
        gates = jax.nn.sigmoid(xn @ w_gate[i] + b_gate[i])
        g_a, g_b = gates[..., :D_MODEL], gates[..., D_MODEL:]
        h = h + (g_a * y_a + g_b * y_b) @ w_o[i]
        xf = rmsnorm(h, norm_ffn[i])
        h = h + (jax.nn.silu(xf @ w_ffn1[i]) * (xf @ w_ffn3[i])) @ w_ffn2[i]
        h = h + jax.nn.sigmoid(_rms(h) @ w_ple_gate[i]) * (p[i] @ w_ple[i])
    return rmsnorm(h, norm_final)
```

```python
import functools

import jax
import jax.numpy as jnp
from jax import lax
from jax.experimental import pallas as pl
from jax.experimental.pallas import tpu as pltpu

F32 = jnp.float32
BF16 = jnp.bfloat16
I32 = jnp.int32

N_HEADS_A = 16
HEAD_DIM = 128
N_KV_A = 4
N_HEADS_IDX = 16
TOPK_MAX = 256
Q_BLOCK = 128
ROPE_THETA = 10000.0
HEAD_DIM_B = 64
GN_EPS = 64e-5
RMS_EPS = 1e-6

LANES = 128
SUBLANES = 8
MXU_DIM = 256

INT_MIN = -2 ** 31

CHUNK = 64
HEADS_PER_TILE = MXU_DIM // HEAD_DIM_B
TILE_B = HEADS_PER_TILE * HEAD_DIM_B


def _cparams(sem, vmem_mib):
    return pltpu.CompilerParams(dimension_semantics=sem, vmem_limit_bytes=vmem_mib << 20)


def _dot(a, b):
    return jnp.dot(a, b, preferred_element_type=F32)


def _dot_nt(a, b):
    return lax.dot_general(a, b, (((1,), (1,)), ((), ())), preferred_element_type=F32)


def _dot_tn(a, b):
    return lax.dot_general(a, b, (((0,), (0,)), ((), ())), preferred_element_type=F32)


def _sigmoid(x):
    return 1.0 / (1.0 + jnp.exp(-x))


def _rms_rows(x):
    return x * lax.rsqrt(jnp.mean(x * x, axis=-1, keepdims=True) + RMS_EPS)


def _rope_tab_kernel(pos_ref, freq_ref, sign_ref, cos_ref, sin_ref):
    ang = pos_ref[...].astype(F32) * freq_ref[...]
    cos_ref[...] = jnp.cos(ang)
    sin_ref[...] = jnp.sin(ang) * sign_ref[...]


def _rope_tables(positions):
    n = positions.size
    half = HEAD_DIM // 2
    inv_freq = ROPE_THETA ** (-jnp.arange(0, HEAD_DIM, 2, dtype=F32) / HEAD_DIM)
    freq2 = jnp.concatenate([inv_freq, inv_freq]).reshape(1, HEAD_DIM)
    sign = jnp.concatenate([-jnp.ones((half,), F32), jnp.ones((half,), F32)]).reshape(1, HEAD_DIM)
    bm = min(2048, n)
    return pl.pallas_call(
        _rope_tab_kernel,
        grid=(n // bm,),
        in_specs=[pl.BlockSpec((bm, 1), lambda i: (i, 0)),
                  pl.BlockSpec((1, HEAD_DIM), lambda i: (0, 0)),
                  pl.BlockSpec((1, HEAD_DIM), lambda i: (0, 0))],
        out_specs=[pl.BlockSpec((bm, HEAD_DIM), lambda i: (i, 0)),
                   pl.BlockSpec((bm, HEAD_DIM), lambda i: (i, 0))],
        out_shape=[jax.ShapeDtypeStruct((n, HEAD_DIM), F32)] * 2,
        compiler_params=_cparams(("arbitrary",), 32),
        name="rope_tables",
    )(positions.reshape(n, 1), freq2, sign)


def _rope(t, cos, sin):
    return t * cos + pltpu.roll(t, HEAD_DIM // 2, axis=1) * sin


def _prep_kernel(x_ref, xp_ref, gain_ref, mu_ref, w1_ref, w2_ref, w0_ref, a1_ref, a2_ref, a0_ref,
                 g1_ref, g2_ref, wwi_ref,
                 xn_ref, lw_ref, a_ref, g_ref, wi_ref, *, seq_len, bm):
    i = pl.program_id(0)
    gain = gain_ref[...]
    xn = _rms_rows(x_ref[...]) * gain
    prev = (_rms_rows(xp_ref[...]) * gain)[SUBLANES - 1:SUBLANES, :]
    prev = jnp.where((i * bm) % seq_len == 0, jnp.zeros_like(prev), prev)
    row = lax.broadcasted_iota(I32, xn.shape, 0)
    sh = jnp.where(row == 0, prev, pltpu.roll(xn, 1, axis=0))
    xx = sh - xn
    xn_bf = xn.astype(BF16)
    xn_ref[...] = xn_bf
    wi_ref[...] = _dot(xn_bf, wwi_ref[...]) * (N_HEADS_IDX ** -0.5)

    xw = (xn + xx * mu_ref[0:1, :]).astype(BF16)
    hw = jnp.tanh(_dot(xw, w1_ref[...])).astype(BF16)
    wl = w0_ref[...] + _dot(hw, w2_ref[...])
    z = -wl
    softplus = jnp.maximum(z, 0.0) + jnp.log(1.0 + jnp.exp(-jnp.abs(z)))
    w = -softplus - 0.5
    lw_ref[...] = -jnp.exp(w)

    xa = (xn + xx * mu_ref[1:2, :]).astype(BF16)
    ha = _dot(xa, a1_ref[...]).astype(BF16)
    a_ref[...] = _sigmoid(a0_ref[...] + _dot(ha, a2_ref[...]))

    xg = (xn + xx * mu_ref[2:3, :]).astype(BF16)
    hg = _sigmoid(_dot(xg, g1_ref[...])).astype(BF16)
    g_ref[...] = _dot(hg, g2_ref[...])


def _prep(x2, gain, mu_wag, w1, w2, w0, a1, a2, a0, g1, g2, wwi, seq_len):
    m, d = x2.shape
    db = w2.shape[1]
    bm = min(128, m)
    full = lambda a: pl.BlockSpec(a.shape, lambda i: (0,) * a.ndim)
    nsub = bm // SUBLANES
    return pl.pallas_call(
        functools.partial(_prep_kernel, seq_len=seq_len, bm=bm),
        grid=(m // bm,),
        in_specs=[pl.BlockSpec((bm, d), lambda i: (i, 0)),
                  pl.BlockSpec((SUBLANES, d), lambda i: (jnp.maximum(i * nsub - 1, 0), 0)),
                  full(gain), full(mu_wag), full(w1), full(w2), full(w0), full(a1), full(a2), full(a0),
                  full(g1), full(g2), full(wwi)],
        out_specs=[pl.BlockSpec((bm, d), lambda i: (i, 0)),
                   pl.BlockSpec((bm, db), lambda i: (i, 0)),
                   pl.BlockSpec((bm, db), lambda i: (i, 0)),
                   pl.BlockSpec((bm, db), lambda i: (i, 0)),
                   pl.BlockSpec((bm, LANES), lambda i: (i, 0))],
        out_shape=[jax.ShapeDtypeStruct((m, d), BF16),
                   jax.ShapeDtypeStruct((m, db), F32),
                   jax.ShapeDtypeStruct((m, db), F32),
                   jax.ShapeDtypeStruct((m, db), F32),
                   jax.ShapeDtypeStruct((m, LANES), F32)],
        compiler_params=_cparams(("arbitrary",), 48),
        name="prep",
    )(x2, x2, gain, mu_wag, w1, w2, w0, a1, a2, a0, g1, g2, wwi)


def _proj_rope_hm_kernel(x_ref, w_ref, cos_ref, sin_ref, o_ref):
    acc = _dot(x_ref[...], w_ref[...])
    cos = cos_ref[...]
    sin = sin_ref[...]
    bm, bn = acc.shape
    for j in range(bn // HEAD_DIM):
        t = _rope(acc[:, j * HEAD_DIM:(j + 1) * HEAD_DIM], cos, sin).astype(o_ref.dtype)
        for r in range(bm // Q_BLOCK):
            o_ref[r, j] = t[r * Q_BLOCK:(r + 1) * Q_BLOCK, :]


def _proj_rope_headmajor(xn, w, cos, sin, bm=1024, bn=1024):
    m, k = xn.shape
    n = w.shape[1]
    bm = min(bm, m)
    return pl.pallas_call(
        _proj_rope_hm_kernel,
        grid=(m // bm, n // bn),
        in_specs=[pl.BlockSpec((bm, k), lambda i, j: (i, 0)),
                  pl.BlockSpec((k, bn), lambda i, j: (0, j)),
                  pl.BlockSpec((bm, HEAD_DIM), lambda i, j: (i, 0)),
                  pl.BlockSpec((bm, HEAD_DIM), lambda i, j: (i, 0))],
        out_specs=pl.BlockSpec((bm // Q_BLOCK, bn // HEAD_DIM, Q_BLOCK, HEAD_DIM),
                               lambda i, j: (i, j, 0, 0)),
        out_shape=jax.ShapeDtypeStruct((m // Q_BLOCK, n // HEAD_DIM, Q_BLOCK, HEAD_DIM), BF16),
        compiler_params=_cparams(("arbitrary", "arbitrary"), 52),
        name="proj_q",
    )(xn, w, cos, sin)


def _proj_kv_kernel(x_ref, w_ref, cos_ref, sin_ref, o_ref, *, n_rope):
    acc = _dot(x_ref[...], w_ref[...])
    cos = cos_ref[...]
    sin = sin_ref[...]
    for j in range(acc.shape[1] // HEAD_DIM):
        t = acc[:, j * HEAD_DIM:(j + 1) * HEAD_DIM]
        if j < n_rope:
            t = _rope(t, cos, sin)
        o_ref[:, j * HEAD_DIM:(j + 1) * HEAD_DIM] = t.astype(o_ref.dtype)


def _proj_kv(xn, w, cos, sin, n_rope, bm=1024):
    m, k = xn.shape
    n = w.shape[1]
    bm = min(bm, m)
    return pl.pallas_call(
        functools.partial(_proj_kv_kernel, n_rope=n_rope),
        grid=(m // bm,),
        in_specs=[pl.BlockSpec((bm, k), lambda i: (i, 0)),
                  pl.BlockSpec((k, n), lambda i: (0, 0)),
                  pl.BlockSpec((bm, HEAD_DIM), lambda i: (i, 0)),
                  pl.BlockSpec((bm, HEAD_DIM), lambda i: (i, 0))],
        out_specs=pl.BlockSpec((bm, n), lambda i: (i, 0)),
        out_shape=jax.ShapeDtypeStruct((m, n), BF16),
        compiler_params=_cparams(("arbitrary",), 52),
        name="proj_kv",
    )(xn, w, cos, sin)


def _mm_kernel(x_ref, w_ref, o_ref):
    o_ref[...] = _dot(x_ref[...], w_ref[...]).astype(o_ref.dtype)


def _matmul(x, w, out_dtype, bm=1024, bn=1024, name="matmul"):
    m, k = x.shape
    n = w.shape[1]
    bm = min(bm, m)
    return pl.pallas_call(
        _mm_kernel,
        grid=(m // bm, n // bn),
        in_specs=[pl.BlockSpec((bm, k), lambda i, j: (i, 0)),
                  pl.BlockSpec((k, bn), lambda i, j: (0, j))],
        out_specs=pl.BlockSpec((bm, bn), lambda i, j: (i, j)),
        out_shape=jax.ShapeDtypeStruct((m, n), out_dtype),
        compiler_params=_cparams(("arbitrary", "arbitrary"), 52),
        name=name,
    )(x, w)


def _attn_kernel(q_ref, qi_ref, wi_ref, kt_ref, kit_ref, v_ref, o_ref, key_ref, bias_ref, j_ref,
                 *, topk, tk, idx_bits):
    i = pl.program_id(1)
    nk = (i + 1) * Q_BLOCK
    nch = (nk + tk - 1) // tk
    n_lt = tk // LANES
    qpos = i * Q_BLOCK + lax.broadcasted_iota(I32, (Q_BLOCK, tk), 0)
    lane = lax.broadcasted_iota(I32, (Q_BLOCK, tk), 1)

    qi = qi_ref[0].reshape(N_HEADS_IDX * Q_BLOCK, HEAD_DIM)
    wi = wi_ref[...] * (HEAD_DIM ** -0.5)

    def score_body(c, carry):
        off = pl.multiple_of(c * tk, tk)
        lg = _dot(qi, kit_ref[0, :, pl.ds(off, tk)])
        sc = jnp.zeros((Q_BLOCK, tk), F32)
        for h in range(N_HEADS_IDX):
            sc = sc + wi[:, h:h + 1] * jnp.maximum(lg[h * Q_BLOCK:(h + 1) * Q_BLOCK, :], 0.0)
        bits = pltpu.bitcast(sc, I32)
        key = bits ^ ((bits >> 31) & 0x7FFFFFFF)
        key = jnp.where(key == -1, 0, key)
        key = jnp.where(off + lane <= qpos, key, INT_MIN)
        key_ref[:, pl.ds(off, tk)] = key
        return carry

    lax.fori_loop(0, nch, score_body, 0)

    def count(pred):
        def body(c, acc):
            off = pl.multiple_of(c * tk, tk)
            x = jnp.where(pred(key_ref[:, pl.ds(off, tk)], off), 1.0, 0.0)
            for j in range(n_lt):
                acc = acc + x[:, j * LANES:(j + 1) * LANES]
            return acc
        acc = lax.fori_loop(0, nch, body, jnp.zeros((Q_BLOCK, LANES), F32))
        return jnp.sum(acc, axis=1, keepdims=True)

    def bit_body(bi, thr):
        cand = thr ^ jnp.left_shift(jnp.int32(1), 31 - bi)
        cnt = count(lambda kc, off: kc >= cand)
        return jnp.where(cnt >= topk, cand, thr)

    thr = lax.fori_loop(0, 32, bit_body, jnp.full((Q_BLOCK, 1), INT_MIN, I32))
    thr = jnp.maximum(thr, INT_MIN + 1)
    cnt_ge = count(lambda kc, off: kc >= thr)
    cnt_gt = count(lambda kc, off: kc > thr)
    need = topk - cnt_gt

    j_ref[...] = jnp.full((Q_BLOCK, LANES), 2 ** idx_bits - 1, I32)

    @pl.when(jnp.max(cnt_ge) > topk)
    def _():
        def jbody(bi, jt):
            cand = jt | jnp.left_shift(jnp.int32(1), idx_bits - 1 - bi)
            f = count(lambda kc, off: jnp.where(kc == thr, off + lane, 2 ** 30) < cand)
            return jnp.where(f <= need, cand, jt)
        jt = lax.fori_loop(0, idx_bits, jbody, jnp.zeros((Q_BLOCK, 1), I32))
        j_ref[...] = jnp.broadcast_to(jt, (Q_BLOCK, LANES))

    jt = j_ref[:, 0:1]

    def bias_body(c, carry):
        off = pl.multiple_of(c * tk, tk)
        kc = key_ref[:, pl.ds(off, tk)]
        tie_pos = jnp.where(kc == thr, off + lane, 2 ** 30)
        sel = jnp.logical_or(kc > thr, tie_pos < jt)
        bias_ref[:, pl.ds(off, tk)] = jnp.where(sel, 0.0, -jnp.inf)
        return carry

    lax.fori_loop(0, nch, bias_body, 0)

    n_rep = N_HEADS_A // N_KV_A
    rows = n_rep * Q_BLOCK
    q_all = q_ref[0].reshape(N_HEADS_A * Q_BLOCK, HEAD_DIM)
    for g in range(N_KV_A):
        qg = q_all[g * rows:(g + 1) * rows, :]

        def att_body(c, carry, g=g, qg=qg):
            m_run, l_run, acc = carry
            off = pl.multiple_of(c * tk, tk)
            s = _dot(qg, kt_ref[0, g * HEAD_DIM:(g + 1) * HEAD_DIM, pl.ds(off, tk)])
            s = s * (HEAD_DIM ** -0.5)
            s = (s.reshape(n_rep, Q_BLOCK, tk) + bias_ref[:, pl.ds(off, tk)][None]).reshape(rows, tk)
            mx = s[:, 0:LANES]
            for j in range(1, n_lt):
                mx = jnp.maximum(mx, s[:, j * LANES:(j + 1) * LANES])
            m_new = jnp.maximum(m_run, jnp.max(mx, axis=1, keepdims=True))
            alpha = jnp.exp(m_run - m_new)
            p = jnp.exp(s - m_new)
            ps = p[:, 0:LANES]
            for j in range(1, n_lt):
                ps = ps + p[:, j * LANES:(j + 1) * LANES]
            l_new = alpha * l_run + ps
            pv = _dot(p.astype(BF16), v_ref[0, pl.ds(off, tk), g * HEAD_DIM:(g + 1) * HEAD_DIM])
            return m_new, l_new, alpha * acc + pv

        init = (jnp.full((rows, 1), -1e30, F32), jnp.zeros((rows, LANES), F32),
                jnp.zeros((rows, HEAD_DIM), F32))
        _, l_fin, acc = lax.fori_loop(0, nch, att_body, init)
        o = acc / jnp.sum(l_fin, axis=1, keepdims=True)
        for n in range(n_rep):
            h = g * n_rep + n
            o_ref[:, h * HEAD_DIM:(h + 1) * HEAD_DIM] = o[n * Q_BLOCK:(n + 1) * Q_BLOCK, :].astype(o_ref.dtype)


def _attention(qh, wi, kt, kit, v, batch, seq_len):
    nb = seq_len // Q_BLOCK
    topk = min(TOPK_MAX, seq_len // 4)
    tk = min(512, seq_len)
    kvd = N_KV_A * HEAD_DIM
    single = pl.Buffered(1)
    return pl.pallas_call(
        functools.partial(_attn_kernel, topk=topk, tk=tk, idx_bits=seq_len.bit_length()),
        grid=(batch, nb),
        in_specs=[pl.BlockSpec((1, N_HEADS_A, Q_BLOCK, HEAD_DIM), lambda b, i: (b * nb + i, 0, 0, 0)),
                  pl.BlockSpec((1, N_HEADS_IDX, Q_BLOCK, HEAD_DIM), lambda b, i: (b * nb + i, 1, 0, 0)),
                  pl.BlockSpec((Q_BLOCK, LANES), lambda b, i: (b * nb + i, 0)),
                  pl.BlockSpec((1, kvd, seq_len), lambda b, i: (b, 0, 0), pipeline_mode=single),
                  pl.BlockSpec((1, HEAD_DIM, seq_len), lambda b, i: (b, 0, 0), pipeline_mode=single),
                  pl.BlockSpec((1, seq_len, kvd), lambda b, i: (b, 0, 0), pipeline_mode=single)],
        out_specs=pl.BlockSpec((Q_BLOCK, N_HEADS_A * HEAD_DIM), lambda b, i: (b * nb + i, 0)),
        out_shape=jax.ShapeDtypeStruct((batch * seq_len, N_HEADS_A * HEAD_DIM), BF16),
        scratch_shapes=[pltpu.VMEM((Q_BLOCK, seq_len), I32),
                        pltpu.VMEM((Q_BLOCK, seq_len), F32),
                        pltpu.VMEM((Q_BLOCK, LANES), I32)],
        compiler_params=_cparams(("arbitrary", "arbitrary"), 52),
        name="dsa_attention",
    )(qh, qh, wi, kt, kit, v)


def _split2(x):
    hi = x.astype(BF16)
    lo = (x - hi.astype(F32)).astype(BF16)
    return hi, lo


def _dot_hp(x, w_bf):
    hi = x.astype(BF16)
    r1 = x - hi.astype(F32)
    mid = r1.astype(BF16)
    lo = (r1 - mid.astype(F32)).astype(BF16)
    return _dot(hi, w_bf) + _dot(mid, w_bf) + _dot(lo, w_bf)


def _rwkv_kernel(r_ref, k_ref, v_ref, lw_ref, a_ref, g_ref, mu_ref, kk_ref, ka_ref, rk_ref,
                 lnw_ref, lnb_ref, bd_ref, ones_ref, ltri_ref, ts_ref, ti_ref, ic_ref, eye_ref,
                 o_ref, st_ref, prev_ref):
    c = pl.program_id(2)

    @pl.when(c == 0)
    def _():
        st_ref[...] = jnp.zeros_like(st_ref)
        prev_ref[...] = jnp.zeros_like(prev_ref)

    bdm = bd_ref[...]
    row = lax.broadcasted_iota(I32, (CHUNK, TILE_B), 0)

    def shifted(x, slot):
        prev = prev_ref[slot:slot + 1, :]
        prev_ref[slot:slot + 1, :] = x[CHUNK - 1:CHUNK, :]
        return jnp.where(row == 0, prev, pltpu.roll(x, 1, axis=0))

    r0 = r_ref[...]
    k0 = k_ref[...]
    v0 = v_ref[...]
    r = r0 + (shifted(r0, 0) - r0) * mu_ref[0:1, :]
    k = k0 + (shifted(k0, 1) - k0) * mu_ref[1:2, :]
    v = v0 + (shifted(v0, 2) - v0) * mu_ref[2:3, :]
    a = a_ref[...]
    ones_bd = ones_ref[...]

    kk = k * kk_ref[...]
    ss = _dot_hp(kk * kk, ones_bd)
    kk = kk / jnp.maximum(jnp.sqrt(ss), 1e-12)
    k2 = k * (1.0 + (a - 1.0) * ka_ref[...])
    aa = -kk
    bb = kk * a

    lw = lw_ref[...]
    cs = _cumsum_rows(lw, ltri_ref[...])
    tot = cs[CHUNK - 1:CHUNK, :]
    e_in = jnp.exp(cs)
    e_out = jnp.exp(-cs)
    e_tail = jnp.exp(tot - cs)
    at = aa * jnp.exp(cs - lw)
    rt = r * e_in
    bt = (bb * e_out)
    kt = (k2 * e_out)
    bh = (bb * e_tail).astype(BF16)
    kh = (k2 * e_tail).astype(BF16)
    wc = jnp.exp(tot)

    def bd(x):
        return (jnp.concatenate([x] * HEADS_PER_TILE, axis=0) * bdm).astype(BF16)

    lhs = jnp.concatenate([at, rt], axis=0).astype(BF16)
    mb = _dot_nt(lhs, bd(bt))
    mk = _dot_nt(lhs, bd(kt))
    ts = ts_ref[...]
    ti = ti_ref[...]
    m_ab = mb[0:CHUNK] * ts
    m_rb = (mb[CHUNK:2 * CHUNK] * ti).astype(BF16)
    m_ak = (mk[0:CHUNK] * ts).astype(BF16)
    m_rk = (mk[CHUNK:2 * CHUNK] * ti).astype(BF16)

    m_pow = m_ab
    t_inv = ic_ref[...] + m_ab
    n_round = CHUNK.bit_length() - 1
    for _ in range(1, n_round):
        m_pow_bf = m_pow.astype(BF16)
        m_pow = _dot(m_pow_bf, bd(m_pow))
        t_inv = t_inv + _dot(m_pow.astype(BF16), bd(t_inv))
    t_bf = t_inv.astype(BF16)

    vbd = bd(v)
    p = _dot(t_bf, bd(at))
    u = _dot(m_ak, vbd)
    q = _dot(t_bf, bd(u))
    rp = rt + _dot(m_rb, bd(p))
    y0 = _dot(m_rb, bd(q)) + _dot(m_rk, vbd)

    st = st_ref[...]
    st_bf = st.astype(BF16)
    y = _dot(rp.astype(BF16), st_bf) + y0

    a_t = _dot_tn(bh, p.astype(BF16)) * bdm + eye_ref[...] * wc
    d_t = _dot_tn(jnp.concatenate([bh, kh], axis=0),
                  jnp.concatenate([q.astype(BF16), v.astype(BF16)], axis=0)) * bdm
    st_ref[...] = _dot(a_t.astype(BF16), st_bf) + d_t

    inv_n = 1.0 / HEAD_DIM_B
    mu = _dot_hp(y, ones_bd) * inv_n
    yc = y - mu
    var = _dot_hp(yc * yc, ones_bd) * inv_n
    yn = (yc * lax.rsqrt(var + GN_EPS)) * lnw_ref[...] + lnb_ref[...]
    bonus = _dot_hp(r * k2 * rk_ref[...], ones_bd) * v
    o_ref[...] = ((yn + bonus) * g_ref[...]).astype(o_ref.dtype)


def _cumsum_rows(x, ltri_bf):
    hi = x.astype(BF16)
    r1 = x - hi.astype(F32)
    mid = r1.astype(BF16)
    lo = (r1 - mid.astype(F32)).astype(BF16)
    return _dot(ltri_bf, hi) + _dot(ltri_bf, mid) + _dot(ltri_bf, lo)


def _rwkv_consts():
    idx = jnp.arange(TILE_B)
    head = idx // HEAD_DIM_B
    bdm = (head[:, None] == head[None, :])
    t = jnp.arange(CHUNK)[:, None]
    s = (idx % CHUNK)[None, :]
    return dict(
        bd=bdm.astype(F32),
        ones=bdm.astype(BF16),
        ltri=(jnp.arange(CHUNK)[None, :] <= t).astype(BF16),
        ts=(s < t).astype(F32),
        ti=(s <= t).astype(F32),
        ic=(s == t).astype(F32),
        eye=jnp.eye(TILE_B, dtype=F32),
    )


def _rwkv(rkv, lw, a, g, mu_rkv, k_k, k_a, r_k, ln_w, ln_b, batch, seq_len):
    m, db = lw.shape
    nc = seq_len // CHUNK
    ng = db // TILE_B
    cst = _rwkv_consts()
    blk = lambda off: pl.BlockSpec((CHUNK, TILE_B), lambda b, h, c, off=off: (b * nc + c, off + h))
    par = lambda rows: pl.BlockSpec((rows, TILE_B), lambda b, h, c: (0, h))
    full = lambda arr: pl.BlockSpec(arr.shape, lambda b, h, c: (0, 0))
    row = lambda p: p.reshape(1, db)
    return pl.pallas_call(
        _rwkv_kernel,
        grid=(batch, ng, nc),
        in_specs=[blk(0), blk(ng), blk(2 * ng), blk(0), blk(0), blk(0),
                  par(3), par(1), par(1), par(1), par(1), par(1),
                  full(cst["bd"]), full(cst["ones"]), full(cst["ltri"]), full(cst["ts"]),
                  full(cst["ti"]), full(cst["ic"]), full(cst["eye"])],
        out_specs=blk(0),
        out_shape=jax.ShapeDtypeStruct((m, db), BF16),
        scratch_shapes=[pltpu.VMEM((TILE_B, TILE_B), F32), pltpu.VMEM((SUBLANES, TILE_B), F32)],
        compiler_params=_cparams(("arbitrary", "arbitrary", "arbitrary"), 32),
        name="rwkv7",
    )(rkv, rkv, rkv, lw, a, g, mu_rkv, row(k_k), row(k_a), row(r_k), row(ln_w), row(ln_b),
      cst["bd"], cst["ones"], cst["ltri"], cst["ts"], cst["ti"], cst["ic"], cst["eye"])


def _merge_kernel(xn_ref, oa_ref, ob_ref, wga_ref, wgb_ref, bga_ref, bgb_ref, wpa_ref, wpb_ref, o_ref):
    xn = xn_ref[...]
    ga = _sigmoid(_dot(xn, wga_ref[...]) + bga_ref[...])
    gb = _sigmoid(_dot(xn, wgb_ref[...]) + bgb_ref[...])
    ya = _dot(oa_ref[...], wpa_ref[...])
    yb = _dot(ob_ref[...], wpb_ref[...])
    o_ref[...] = (ga * ya + gb * yb).astype(o_ref.dtype)


def _merge(xn, oa, ob, w_gate, b_gate, w_pa, w_pb, bm=1024, bn=256):
    m, d = xn.shape
    bm = min(bm, m)
    nj = d // bn
    return pl.pallas_call(
        _merge_kernel,
        grid=(m // bm, nj),
        in_specs=[pl.BlockSpec((bm, d), lambda i, j: (i, 0)),
                  pl.BlockSpec((bm, oa.shape[1]), lambda i, j: (i, 0)),
                  pl.BlockSpec((bm, ob.shape[1]), lambda i, j: (i, 0)),
                  pl.BlockSpec((d, bn), lambda i, j: (0, j)),
                  pl.BlockSpec((d, bn), lambda i, j: (0, j + nj)),
                  pl.BlockSpec((1, bn), lambda i, j: (0, j)),
                  pl.BlockSpec((1, bn), lambda i, j: (0, j + nj)),
                  pl.BlockSpec((w_pa.shape[0], bn), lambda i, j: (0, j)),
                  pl.BlockSpec((w_pb.shape[0], bn), lambda i, j: (0, j))],
        out_specs=pl.BlockSpec((bm, bn), lambda i, j: (i, j)),
        out_shape=jax.ShapeDtypeStruct((m, d), BF16),
        compiler_params=_cparams(("arbitrary", "arbitrary"), 52),
        name="gated_merge",
    )(xn, oa, ob, w_gate, w_gate, b_gate, b_gate, w_pa, w_pb)


def _mm_res_kernel(x_ref, w_ref, res_ref, o_ref):
    o_ref[...] = res_ref[...] + _dot(x_ref[...], w_ref[...])


def _matmul_residual(x, w, res, bm=1024, bn=1024, name="matmul_res"):
    m, k = x.shape
    n = w.shape[1]
    bm = min(bm, m)
    return pl.pallas_call(
        _mm_res_kernel,
        grid=(m // bm, n // bn),
        in_specs=[pl.BlockSpec((bm, k), lambda i, j: (i, 0)),
                  pl.BlockSpec((k, bn), lambda i, j: (0, j)),
                  pl.BlockSpec((bm, bn), lambda i, j: (i, j))],
        out_specs=pl.BlockSpec((bm, bn), lambda i, j: (i, j)),
        out_shape=jax.ShapeDtypeStruct((m, n), F32),
        compiler_params=_cparams(("arbitrary", "arbitrary"), 52),
        name=name,
    )(x, w, res)


def _mm_res_k_kernel(x_ref, w_ref, res_ref, o_ref, acc_ref, *, nk):
    kk = pl.program_id(2)

    @pl.when(kk == 0)
    def _():
        acc_ref[...] = jnp.zeros_like(acc_ref)

    acc_ref[...] += _dot(x_ref[...], w_ref[...])

    @pl.when(kk == nk - 1)
    def _():
        o_ref[...] = res_ref[...] + acc_ref[...]


def _matmul_residual_ksplit(x, w, res, bk, bm=1024, bn=1024, name="matmul_res_k"):
    m, k = x.shape
    n = w.shape[1]
    bm = min(bm, m)
    nk = k // bk
    return pl.pallas_call(
        functools.partial(_mm_res_k_kernel, nk=nk),
        grid=(m // bm, n // bn, nk),
        in_specs=[pl.BlockSpec((bm, bk), lambda i, j, kk: (i, kk)),
                  pl.BlockSpec((bk, bn), lambda i, j, kk: (kk, j)),
                  pl.BlockSpec((bm, bn), lambda i, j, kk: (i, j))],
        out_specs=pl.BlockSpec((bm, bn), lambda i, j, kk: (i, j)),
        out_shape=jax.ShapeDtypeStruct((m, n), F32),
        scratch_shapes=[pltpu.VMEM((bm, bn), F32)],
        compiler_params=_cparams(("arbitrary", "arbitrary", "arbitrary"), 52),
        name=name,
    )(x, w, res)


def _ffn_up_kernel(x_ref, w1_ref, w3_ref, o_ref):
    x = x_ref[...]
    h1 = _dot(x, w1_ref[...])
    h3 = _dot(x, w3_ref[...])
    o_ref[...] = (h1 * _sigmoid(h1) * h3).astype(o_ref.dtype)


def _ffn_up(x, w1, w3, bm=1024, bn=512):
    m, k = x.shape
    n = w1.shape[1]
    bm = min(bm, m)
    return pl.pallas_call(
        _ffn_up_kernel,
        grid=(m // bm, n // bn),
        in_specs=[pl.BlockSpec((bm, k), lambda i, j: (i, 0)),
                  pl.BlockSpec((k, bn), lambda i, j: (0, j)),
                  pl.BlockSpec((k, bn), lambda i, j: (0, j))],
        out_specs=pl.BlockSpec((bm, bn), lambda i, j: (i, j)),
        out_shape=jax.ShapeDtypeStruct((m, n), BF16),
        compiler_params=_cparams(("arbitrary", "arbitrary"), 52),
        name="ffn_up",
    )(x, w1, w3)


def _ple_kernel(hn_ref, wg_ref, p_ref, wp_ref, res_ref, o_ref):
    gate = _sigmoid(_dot(hn_ref[...], wg_ref[...]))
    o_ref[...] = res_ref[...] + gate * _dot(p_ref[...], wp_ref[...])


def _ple(hn, wg, p, wp, res, bm=1024, bn=512):
    m, k = hn.shape
    n = wg.shape[1]
    bm = min(bm, m)
    return pl.pallas_call(
        _ple_kernel,
        grid=(m // bm, n // bn),
        in_specs=[pl.BlockSpec((bm, k), lambda i, j: (i, 0)),
                  pl.BlockSpec((k, bn), lambda i, j: (0, j)),
                  pl.BlockSpec((bm, p.shape[1]), lambda i, j: (i, 0)),
                  pl.BlockSpec((p.shape[1], bn), lambda i, j: (0, j)),
                  pl.BlockSpec((bm, bn), lambda i, j: (i, j))],
        out_specs=pl.BlockSpec((bm, bn), lambda i, j: (i, j)),
        out_shape=jax.ShapeDtypeStruct((m, n), F32),
        compiler_params=_cparams(("arbitrary", "arbitrary"), 52),
        name="ple",
    )(hn, wg, p, wp, res)


def _rmsnorm_kernel(x_ref, g_ref, o_ref):
    o_ref[...] = (_rms_rows(x_ref[...]) * g_ref[...]).astype(o_ref.dtype)


def _rms_kernel(x_ref, o_ref):
    o_ref[...] = _rms_rows(x_ref[...]).astype(o_ref.dtype)


def _rmsnorm(x, gain, out_dtype, bm=512):
    m, d = x.shape
    bm = min(bm, m)
    row_spec = pl.BlockSpec((bm, d), lambda i: (i, 0))
    if gain is None:
        kern, args, specs = _rms_kernel, (x,), [row_spec]
    else:
        kern, args = _rmsnorm_kernel, (x, gain.reshape(1, d))
        specs = [row_spec, pl.BlockSpec((1, d), lambda i: (0, 0))]
    return pl.pallas_call(
        kern,
        grid=(m // bm,),
        in_specs=specs,
        out_specs=row_spec,
        out_shape=jax.ShapeDtypeStruct((m, d), out_dtype),
        compiler_params=_cparams(("arbitrary",), 48),
        name="rmsnorm",
    )(*args)


def _pad_to(a, axis, size):
    pad = [(0, 0)] * a.ndim
    pad[axis] = (0, size - a.shape[axis])
    return jnp.pad(a, pad)


def _layer(h2, p2, cos, sin, batch, seq_len, norm_mix, w_in, mu_rkv, mu_wag, w0, w1, w2, a0, a1, a2,
           g1, g2, k_k, k_a, r_k, ln_w, ln_b, w_pa, w_pb, w_gate, b_gate, w_o, norm_ffn,
           w_ffn1, w_ffn3, w_ffn2, w_ple_gate, w_ple):
    d = h2.shape[1]
    d_a = N_HEADS_A * HEAD_DIM
    kvd = N_KV_A * HEAD_DIM
    d_iq = N_HEADS_IDX * HEAD_DIM
    d_b = w_pb.shape[0]
    o_q, o_k, o_v = 0, d_a, d_a + kvd
    o_qi = o_v + kvd
    o_ki = o_qi + d_iq
    o_wi = o_ki + HEAD_DIM
    o_r = o_wi + N_HEADS_IDX

    bf = lambda a: a.astype(BF16)
    w_qqi = bf(jnp.concatenate([w_in[:, o_q:o_q + d_a], w_in[:, o_qi:o_qi + d_iq]], axis=1))
    w_kkiv = bf(jnp.concatenate([w_in[:, o_k:o_k + kvd], w_in[:, o_ki:o_ki + HEAD_DIM],
                                 w_in[:, o_v:o_v + kvd]], axis=1))
    w_wi = bf(_pad_to(w_in[:, o_wi:o_wi + N_HEADS_IDX], 1, LANES))
    w_rkv = bf(w_in[:, o_r:o_r + 3 * d_b])
    lora = LANES
    w1p, a1p = bf(_pad_to(w1, 1, lora)), bf(_pad_to(a1, 1, lora))
    w2p, a2p = bf(_pad_to(w2, 0, lora)), bf(_pad_to(a2, 0, lora))

    xn, lw, a, g, wi = _prep(h2, norm_mix.reshape(1, d), mu_wag, w1p, w2p, w0.reshape(1, d_b),
                             a1p, a2p, a0.reshape(1, d_b), bf(g1), bf(g2), w_wi, seq_len)

    qh = _proj_rope_headmajor(xn, w_qqi, cos, sin)
    kkiv = _proj_kv(xn, w_kkiv, cos, sin, n_rope=N_KV_A + 1)
    rkv = _matmul(xn, w_rkv, F32, name="proj_rkv")

    kkiv = kkiv.reshape(batch, seq_len, -1)
    kt = jnp.swapaxes(kkiv[:, :, :kvd], 1, 2)
    kit = jnp.swapaxes(kkiv[:, :, kvd:kvd + HEAD_DIM], 1, 2)
    v = kkiv[:, :, kvd + HEAD_DIM:]
    o_att = _attention(qh, wi, kt, kit, v, batch, seq_len)

    o_rwkv = _rwkv(rkv, lw, a, g, mu_rkv, k_k, k_a, r_k.reshape(-1), ln_w, ln_b, batch, seq_len)

    mixed = _merge(xn, o_att, o_rwkv, bf(w_gate), b_gate.reshape(1, -1), bf(w_pa), bf(w_pb))
    h2 = _matmul_residual(mixed, bf(w_o), h2, bn=512, name="out_proj")

    xf = _rmsnorm(h2, norm_ffn, BF16)
    d_ff = w_ffn1.shape[1]
    ff_tile = 1024
    d_ffp = -(-d_ff // ff_tile) * ff_tile
    u = _ffn_up(xf, bf(_pad_to(w_ffn1, 1, d_ffp)), bf(_pad_to(w_ffn3, 1, d_ffp)))
    h2 = _matmul_residual_ksplit(u, bf(_pad_to(w_ffn2, 0, d_ffp)), h2, bk=d_ffp // 4, name="ffn_down")

    hn = _rmsnorm(h2, None, BF16)
    h2 = _ple(hn, bf(w_ple_gate), bf(p2), bf(w_ple), h2)
    return h2


def kernel(x, p, positions, norm_mix, w_in, mu_rkv, mu_wag, w0, w1, w2, a0, a1, a2, g1, g2, k_k, k_a,
           r_k, ln_w, ln_b, w_pa, w_pb, w_gate, b_gate, w_o, norm_ffn, w_ffn1, w_ffn3, w_ffn2,
           w_ple_gate, w_ple, norm_final):
    batch, seq_len, d = x.shape
    depth = p.shape[0]
    h2 = x.reshape(batch * seq_len, d)
    cos, sin = _rope_tables(positions)
    for i in range(depth):
        h2 = _layer(h2, p[i].reshape(batch * seq_len, -1), cos, sin, batch, seq_len,
                    norm_mix[i], w_in[i], mu_rkv[i], mu_wag[i], w0[i], w1[i], w2[i], a0[i], a1[i], a2[i],
                    g1[i], g2[i], k_k[i], k_a[i], r_k[i], ln_w[i], ln_b[i], w_pa[i], w_pb[i], w_gate[i],
                    b_gate[i], w_o[i], norm_ffn[i], w_ffn1[i], w_ffn3[i], w_ffn2[i], w_ple_gate[i],
                    w_ple[i])
    out = _rmsnorm(h2, norm_final, F32)
    return out.reshape(batch, seq_len, d)
```

```python
import functools

import jax
import jax.numpy as jnp
from jax import lax
from jax.experimental import pallas as pl
from jax.experimental.pallas import tpu as pltpu

F32 = jnp.float32
BF16 = jnp.bfloat16
I32 = jnp.int32

N_HEADS_A = 16
HEAD_DIM = 128
N_KV_A = 4
N_HEADS_IDX = 16
TOPK_MAX = 256
Q_BLOCK = 128
ATT_ROW_BLOCK = 64
ROPE_THETA = 10000.0
HEAD_DIM_B = 64
GN_EPS = 64e-5
RMS_EPS = 1e-6

LANES = 128
SUBLANES = 8
MXU_DIM = 256

INT_MIN = -2 ** 31
LOG2_E = 1.4426950408889634

CHUNK = 64
HEADS_PER_TILE = MXU_DIM // HEAD_DIM_B
TILE_B = HEADS_PER_TILE * HEAD_DIM_B
RWKV_TILES_PER_STEP = 4


def _cparams(sem, vmem_mib):
    return pltpu.CompilerParams(dimension_semantics=sem, vmem_limit_bytes=vmem_mib << 20)


def _dot(a, b):
    return jnp.dot(a, b, preferred_element_type=F32)


def _dot_nt(a, b):
    return lax.dot_general(a, b, (((1,), (1,)), ((), ())), preferred_element_type=F32)


def _dot_tn(a, b):
    return lax.dot_general(a, b, (((0,), (0,)), ((), ())), preferred_element_type=F32)


def _sigmoid(x):
    return 1.0 / (1.0 + jnp.exp(-x))


def _rms_rows(x):
    return x * lax.rsqrt(jnp.mean(x * x, axis=-1, keepdims=True) + RMS_EPS)


def _rope_tab_kernel(pos_ref, freq_ref, sign_ref, cos_ref, sin_ref):
    ang = pos_ref[...].astype(F32) * freq_ref[...]
    cos_ref[...] = jnp.cos(ang)
    sin_ref[...] = jnp.sin(ang) * sign_ref[...]


def _rope_tables(positions):
    n = positions.size
    half = HEAD_DIM // 2
    inv_freq = ROPE_THETA ** (-jnp.arange(0, HEAD_DIM, 2, dtype=F32) / HEAD_DIM)
    freq2 = jnp.concatenate([inv_freq, inv_freq]).reshape(1, HEAD_DIM)
    sign = jnp.concatenate([-jnp.ones((half,), F32), jnp.ones((half,), F32)]).reshape(1, HEAD_DIM)
    bm = min(2048, n)
    return pl.pallas_call(
        _rope_tab_kernel,
        grid=(n // bm,),
        in_specs=[pl.BlockSpec((bm, 1), lambda i: (i, 0)),
                  pl.BlockSpec((1, HEAD_DIM), lambda i: (0, 0)),
                  pl.BlockSpec((1, HEAD_DIM), lambda i: (0, 0))],
        out_specs=[pl.BlockSpec((bm, HEAD_DIM), lambda i: (i, 0)),
                   pl.BlockSpec((bm, HEAD_DIM), lambda i: (i, 0))],
        out_shape=[jax.ShapeDtypeStruct((n, HEAD_DIM), F32)] * 2,
        compiler_params=_cparams(("arbitrary",), 32),
        name="rope_tables",
    )(positions.reshape(n, 1), freq2, sign)


def _rope(t, cos, sin):
    return t * cos + pltpu.roll(t, HEAD_DIM // 2, axis=1) * sin


def _prep_kernel(x_ref, xp_ref, gain_ref, mu_ref, w1_ref, w2_ref, w0_ref, a1_ref, a2_ref, a0_ref,
                 g1_ref, g2_ref, wwi_ref,
                 xn_ref, lw_ref, a_ref, g_ref, wi_ref, *, seq_len, bm):
    i = pl.program_id(0)
    gain = gain_ref[...]
    xn = _rms_rows(x_ref[...]) * gain
    prev = (_rms_rows(xp_ref[...]) * gain)[SUBLANES - 1:SUBLANES, :]
    prev = jnp.where((i * bm) % seq_len == 0, jnp.zeros_like(prev), prev)
    row = lax.broadcasted_iota(I32, xn.shape, 0)
    sh = jnp.where(row == 0, prev, pltpu.roll(xn, 1, axis=0))
    xx = sh - xn
    xn_bf = xn.astype(BF16)
    xn_ref[...] = xn_bf
    wi_ref[...] = _dot(xn_bf, wwi_ref[...]) * (N_HEADS_IDX ** -0.5)

    xw = (xn + xx * mu_ref[0:1, :]).astype(BF16)
    hw = jnp.tanh(_dot(xw, w1_ref[...])).astype(BF16)
    wl = w0_ref[...] + _dot(hw, w2_ref[...])
    z = -wl
    softplus = jnp.maximum(z, 0.0) + jnp.log(1.0 + jnp.exp(-jnp.abs(z)))
    w = -softplus - 0.5
    lw_ref[...] = -jnp.exp(w)

    xa = (xn + xx * mu_ref[1:2, :]).astype(BF16)
    ha = _dot(xa, a1_ref[...]).astype(BF16)
    a_ref[...] = _sigmoid(a0_ref[...] + _dot(ha, a2_ref[...]))

    xg = (xn + xx * mu_ref[2:3, :]).astype(BF16)
    hg = _sigmoid(_dot(xg, g1_ref[...])).astype(BF16)
    g_ref[...] = _dot(hg, g2_ref[...])


def _prep(x2, gain, mu_wag, w1, w2, w0, a1, a2, a0, g1, g2, wwi, seq_len):
    m, d = x2.shape
    db = w2.shape[1]
    bm = min(128, m)
    full = lambda a: pl.BlockSpec(a.shape, lambda i: (0,) * a.ndim)
    nsub = bm // SUBLANES
    return pl.pallas_call(
        functools.partial(_prep_kernel, seq_len=seq_len, bm=bm),
        grid=(m // bm,),
        in_specs=[pl.BlockSpec((bm, d), lambda i: (i, 0)),
                  pl.BlockSpec((SUBLANES, d), lambda i: (jnp.maximum(i * nsub - 1, 0), 0)),
                  full(gain), full(mu_wag), full(w1), full(w2), full(w0), full(a1), full(a2), full(a0),
                  full(g1), full(g2), full(wwi)],
        out_specs=[pl.BlockSpec((bm, d), lambda i: (i, 0)),
                   pl.BlockSpec((bm, db), lambda i: (i, 0)),
                   pl.BlockSpec((bm, db), lambda i: (i, 0)),
                   pl.BlockSpec((bm, db), lambda i: (i, 0)),
                   pl.BlockSpec((bm, LANES), lambda i: (i, 0))],
        out_shape=[jax.ShapeDtypeStruct((m, d), BF16),
                   jax.ShapeDtypeStruct((m, db), F32),
                   jax.ShapeDtypeStruct((m, db), F32),
                   jax.ShapeDtypeStruct((m, db), F32),
                   jax.ShapeDtypeStruct((m, LANES), F32)],
        compiler_params=_cparams(("arbitrary",), 48),
        name="prep",
    )(x2, x2, gain, mu_wag, w1, w2, w0, a1, a2, a0, g1, g2, wwi)


def _proj_rope_hm_kernel(x_ref, w_ref, cos_ref, sin_ref, o_ref):
    acc = _dot(x_ref[...], w_ref[...])
    cos = cos_ref[...]
    sin = sin_ref[...]
    bm, bn = acc.shape
    for j in range(bn // HEAD_DIM):
        t = _rope(acc[:, j * HEAD_DIM:(j + 1) * HEAD_DIM], cos, sin).astype(o_ref.dtype)
        for r in range(bm // Q_BLOCK):
            o_ref[r, j] = t[r * Q_BLOCK:(r + 1) * Q_BLOCK, :]


def _proj_rope_headmajor(xn, w, cos, sin, bm=1024, bn=1024):
    m, k = xn.shape
    n = w.shape[1]
    bm = min(bm, m)
    return pl.pallas_call(
        _proj_rope_hm_kernel,
        grid=(m // bm, n // bn),
        in_specs=[pl.BlockSpec((bm, k), lambda i, j: (i, 0)),
                  pl.BlockSpec((k, bn), lambda i, j: (0, j)),
                  pl.BlockSpec((bm, HEAD_DIM), lambda i, j: (i, 0)),
                  pl.BlockSpec((bm, HEAD_DIM), lambda i, j: (i, 0))],
        out_specs=pl.BlockSpec((bm // Q_BLOCK, bn // HEAD_DIM, Q_BLOCK, HEAD_DIM),
                               lambda i, j: (i, j, 0, 0)),
        out_shape=jax.ShapeDtypeStruct((m // Q_BLOCK, n // HEAD_DIM, Q_BLOCK, HEAD_DIM), BF16),
        compiler_params=_cparams(("arbitrary", "arbitrary"), 52),
        name="proj_q",
    )(xn, w, cos, sin)


def _proj_kv_kernel(x_ref, w_ref, cos_ref, sin_ref, o_ref, *, n_rope):
    acc = _dot(x_ref[...], w_ref[...])
    cos = cos_ref[...]
    sin = sin_ref[...]
    for j in range(acc.shape[1] // HEAD_DIM):
        t = acc[:, j * HEAD_DIM:(j + 1) * HEAD_DIM]
        if j < n_rope:
            t = _rope(t, cos, sin)
        o_ref[:, j * HEAD_DIM:(j + 1) * HEAD_DIM] = t.astype(o_ref.dtype)


def _proj_kv(xn, w, cos, sin, n_rope, bm=1024):
    m, k = xn.shape
    n = w.shape[1]
    bm = min(bm, m)
    return pl.pallas_call(
        functools.partial(_proj_kv_kernel, n_rope=n_rope),
        grid=(m // bm,),
        in_specs=[pl.BlockSpec((bm, k), lambda i: (i, 0)),
                  pl.BlockSpec((k, n), lambda i: (0, 0)),
                  pl.BlockSpec((bm, HEAD_DIM), lambda i: (i, 0)),
                  pl.BlockSpec((bm, HEAD_DIM), lambda i: (i, 0))],
        out_specs=pl.BlockSpec((bm, n), lambda i: (i, 0)),
        out_shape=jax.ShapeDtypeStruct((m, n), BF16),
        compiler_params=_cparams(("arbitrary",), 52),
        name="proj_kv",
    )(xn, w, cos, sin)


def _mm_kernel(x_ref, w_ref, o_ref):
    o_ref[...] = _dot(x_ref[...], w_ref[...]).astype(o_ref.dtype)


def _matmul(x, w, out_dtype, bm=1024, bn=1024, name="matmul"):
    m, k = x.shape
    n = w.shape[1]
    bm = min(bm, m)
    return pl.pallas_call(
        _mm_kernel,
        grid=(m // bm, n // bn),
        in_specs=[pl.BlockSpec((bm, k), lambda i, j: (i, 0)),
                  pl.BlockSpec((k, bn), lambda i, j: (0, j))],
        out_specs=pl.BlockSpec((bm, bn), lambda i, j: (i, j)),
        out_shape=jax.ShapeDtypeStruct((m, n), out_dtype),
        compiler_params=_cparams(("arbitrary", "arbitrary"), 52),
        name=name,
    )(x, w)


def _attn_kernel(q_ref, qi_ref, wi_ref, kt_ref, kit_ref, v_ref, o_ref, key_ref, bias_ref, j_ref,
                 m_ref, acc_ref, alpha_ref, s_ref, p_ref,
                 *, topk, tk, idx_bits):
    i = pl.program_id(1)
    nk = (i + 1) * Q_BLOCK
    nch = (nk + tk - 1) // tk
    n_lt = tk // LANES
    qpos = i * Q_BLOCK + lax.broadcasted_iota(I32, (Q_BLOCK, tk), 0)
    lane = lax.broadcasted_iota(I32, (Q_BLOCK, tk), 1)

    qi = qi_ref[0].reshape(N_HEADS_IDX * Q_BLOCK, HEAD_DIM)
    wi = wi_ref[...] * (HEAD_DIM ** -0.5)

    def score_body(c, carry):
        off = pl.multiple_of(c * tk, tk)
        lg = _dot(qi, kit_ref[0, :, pl.ds(off, tk)])
        sc = jnp.zeros((Q_BLOCK, tk), F32)
        for h in range(N_HEADS_IDX):
            sc = sc + wi[:, h:h + 1] * jnp.maximum(lg[h * Q_BLOCK:(h + 1) * Q_BLOCK, :], 0.0)
        bits = pltpu.bitcast(sc, I32)
        key = bits ^ ((bits >> 31) & 0x7FFFFFFF)
        key = jnp.where(key == -1, 0, key)
        key = jnp.where(off + lane <= qpos, key, INT_MIN)
        key_ref[:, pl.ds(off, tk)] = key
        return carry

    lax.fori_loop(0, nch, score_body, 0)

    def count(pred):
        def body(c, acc):
            off = pl.multiple_of(c * tk, tk)
            x = jnp.where(pred(key_ref[:, pl.ds(off, tk)], off), 1.0, 0.0)
            for j in range(n_lt):
                acc = acc + x[:, j * LANES:(j + 1) * LANES]
            return acc
        acc = lax.fori_loop(0, nch, body, jnp.zeros((Q_BLOCK, LANES), F32))
        return jnp.sum(acc, axis=1, keepdims=True)

    def bit_body(bi, thr):
        cand = thr ^ jnp.left_shift(jnp.int32(1), 31 - bi)
        cnt = count(lambda kc, off: kc >= cand)
        return jnp.where(cnt >= topk, cand, thr)

    thr = lax.fori_loop(0, 32, bit_body, jnp.full((Q_BLOCK, 1), INT_MIN, I32))
    thr = jnp.maximum(thr, INT_MIN + 1)
    cnt_ge = count(lambda kc, off: kc >= thr)
    cnt_gt = count(lambda kc, off: kc > thr)
    need = topk - cnt_gt

    j_ref[...] = jnp.full((Q_BLOCK, LANES), 2 ** idx_bits - 1, I32)

    @pl.when(jnp.max(cnt_ge) > topk)
    def _():
        def jbody(bi, jt):
            cand = jt | jnp.left_shift(jnp.int32(1), idx_bits - 1 - bi)
            f = count(lambda kc, off: jnp.where(kc == thr, off + lane, 2 ** 30) < cand)
            return jnp.where(f <= need, cand, jt)
        jt = lax.fori_loop(0, idx_bits, jbody, jnp.zeros((Q_BLOCK, 1), I32))
        j_ref[...] = jnp.broadcast_to(jt, (Q_BLOCK, LANES))

    jt = j_ref[:, 0:1]

    def bias_body(c, carry):
        off = pl.multiple_of(c * tk, tk)
        kc = key_ref[:, pl.ds(off, tk)]
        tie_pos = jnp.where(kc == thr, off + lane, 2 ** 30)
        sel = jnp.logical_or(kc > thr, tie_pos < jt)
        bias_ref[:, pl.ds(off, tk)] = jnp.where(sel, 0.0, -jnp.inf)
        return carry

    lax.fori_loop(0, nch, bias_body, 0)

    n_rep = N_HEADS_A // N_KV_A
    rows = n_rep * Q_BLOCK
    q_all = q_ref[0].reshape(N_HEADS_A * Q_BLOCK, HEAD_DIM)
    scale2 = (HEAD_DIM ** -0.5) * LOG2_E
    m_ref[...] = jnp.full(m_ref.shape, -1e30, F32)
    acc_ref[...] = jnp.zeros(acc_ref.shape, F32)
    ones_cols = jnp.ones((tk, HEAD_DIM), BF16)
    n_rb = rows // ATT_ROW_BLOCK

    def masked_scores(g, rb, ctx):
        off, slot = ctx[0], ctx[1]
        r0 = rb * ATT_ROW_BLOCK
        b0 = r0 % Q_BLOCK
        return (s_ref[slot, g, r0:r0 + ATT_ROW_BLOCK, :]
                + bias_ref[b0:b0 + ATT_ROW_BLOCK, pl.ds(off, tk)])

    def qk_into(g, off, slot):
        qg = q_all[g * rows:(g + 1) * rows, :]
        s_ref[slot, g] = _dot(qg, kt_ref[0, g * HEAD_DIM:(g + 1) * HEAD_DIM, pl.ds(off, tk)])

    def stage_qk_next(g, ctx):
        qk_into(g, ctx[2], ctx[3])

    def stage_max(g, off):
        for rb in range(n_rb):
            rsl = slice(rb * ATT_ROW_BLOCK, (rb + 1) * ATT_ROW_BLOCK)
            t = masked_scores(g, rb, off)
            mx = t[:, 0:LANES]
            for j in range(1, n_lt):
                mx = jnp.maximum(mx, t[:, j * LANES:(j + 1) * LANES])
            m_old = m_ref[g, rsl, :]
            m_new = jnp.maximum(m_old, jnp.max(mx, axis=1, keepdims=True))
            alpha_ref[g, rsl, :] = jnp.exp2((m_old - m_new) * scale2)
            m_ref[g, rsl, :] = m_new

    def stage_exp(g, off):
        for rb in range(n_rb):
            rsl = slice(rb * ATT_ROW_BLOCK, (rb + 1) * ATT_ROW_BLOCK)
            t = masked_scores(g, rb, off)
            p_ref[g, rsl, :] = jnp.exp2((t - m_ref[g, rsl, 0:1]) * scale2).astype(BF16)

    def stage_pv(g, ctx):
        off = ctx[0]
        v_ext = jnp.concatenate(
            [v_ref[0, pl.ds(off, tk), g * HEAD_DIM:(g + 1) * HEAD_DIM], ones_cols], axis=1)
        alpha = alpha_ref[g]
        acc_ref[g] = jnp.concatenate([alpha, alpha], axis=1) * acc_ref[g] + _dot(p_ref[g], v_ext)

    stages = (stage_qk_next, stage_max, stage_exp, stage_pv)

    def att_body(c, carry):
        off = pl.multiple_of(c * tk, tk)
        ctx = (off, 0, off, 0)
        for step in range(N_KV_A + len(stages) - 1):
            for g in range(N_KV_A):
                if 0 <= step - g < len(stages):
                    stages[step - g](g, ctx)
        return carry

    lax.fori_loop(0, nch, att_body, 0)
    for g in range(N_KV_A):
        acc = acc_ref[g]
        o = acc[:, 0:HEAD_DIM] / acc[:, HEAD_DIM:2 * HEAD_DIM]
        for n in range(n_rep):
            h = g * n_rep + n
            o_ref[:, h * HEAD_DIM:(h + 1) * HEAD_DIM] = o[n * Q_BLOCK:(n + 1) * Q_BLOCK, :].astype(o_ref.dtype)


def _attention(qh, wi, kt, kit, v, batch, seq_len):
    nb = seq_len // Q_BLOCK
    topk = min(TOPK_MAX, seq_len // 4)
    tk = min(512, seq_len)
    kvd = N_KV_A * HEAD_DIM
    rows = N_HEADS_A // N_KV_A * Q_BLOCK
    single = pl.Buffered(1)
    return pl.pallas_call(
        functools.partial(_attn_kernel, topk=topk, tk=tk, idx_bits=seq_len.bit_length()),
        grid=(batch, nb),
        in_specs=[pl.BlockSpec((1, N_HEADS_A, Q_BLOCK, HEAD_DIM), lambda b, i: (b * nb + i, 0, 0, 0)),
                  pl.BlockSpec((1, N_HEADS_IDX, Q_BLOCK, HEAD_DIM), lambda b, i: (b * nb + i, 1, 0, 0)),
                  pl.BlockSpec((Q_BLOCK, LANES), lambda b, i: (b * nb + i, 0)),
                  pl.BlockSpec((1, kvd, seq_len), lambda b, i: (b, 0, 0), pipeline_mode=single),
                  pl.BlockSpec((1, HEAD_DIM, seq_len), lambda b, i: (b, 0, 0), pipeline_mode=single),
                  pl.BlockSpec((1, seq_len, kvd), lambda b, i: (b, 0, 0), pipeline_mode=single)],
        out_specs=pl.BlockSpec((Q_BLOCK, N_HEADS_A * HEAD_DIM), lambda b, i: (b * nb + i, 0)),
        out_shape=jax.ShapeDtypeStruct((batch * seq_len, N_HEADS_A * HEAD_DIM), BF16),
        scratch_shapes=[pltpu.VMEM((Q_BLOCK, seq_len), I32),
                        pltpu.VMEM((Q_BLOCK, seq_len), F32),
                        pltpu.VMEM((Q_BLOCK, LANES), I32),
                        pltpu.VMEM((N_KV_A, rows, LANES), F32),
                        pltpu.VMEM((N_KV_A, rows, 2 * HEAD_DIM), F32),
                        pltpu.VMEM((N_KV_A, rows, LANES), F32),
                        pltpu.VMEM((1, N_KV_A, rows, tk), F32),
                        pltpu.VMEM((N_KV_A, rows, tk), BF16)],
        compiler_params=_cparams(("arbitrary", "arbitrary"), 52),
        name="dsa_attention",
    )(qh, qh, wi, kt, kit, v)


def _split2(x):
    hi = x.astype(BF16)
    lo = (x - hi.astype(F32)).astype(BF16)
    return hi, lo


def _dot_hp(x, w_bf):
    hi = x.astype(BF16)
    r1 = x - hi.astype(F32)
    mid = r1.astype(BF16)
    lo = (r1 - mid.astype(F32)).astype(BF16)
    return _dot(hi, w_bf) + _dot(mid, w_bf) + _dot(lo, w_bf)


def _rwkv_kernel(r_ref, k_ref, v_ref, lw_ref, a_ref, g_ref, mu_ref, kk_ref, ka_ref, rk_ref,
                 lnw_ref, lnb_ref, bd_ref, ones_ref, ltri_ref, ts_ref, ti_ref, ic_ref, eye_ref,
                 o_ref, st_ref, prev_ref):
    @pl.when(pl.program_id(2) == 0)
    def _():
        st_ref[...] = jnp.zeros_like(st_ref)
        prev_ref[...] = jnp.zeros_like(prev_ref)

    tiles = [_rwkv_tile(tile, r_ref, k_ref, v_ref, lw_ref, a_ref, g_ref, mu_ref, kk_ref, ka_ref, rk_ref,
                        lnw_ref, lnb_ref, bd_ref, ones_ref, ltri_ref, ts_ref, ti_ref, ic_ref, eye_ref,
                        o_ref, st_ref, prev_ref)
             for tile in range(r_ref.shape[1] // TILE_B)]
    while tiles:
        tiles = [t for t in tiles if next(t, "done") != "done"]


def _rwkv_tile(tile, r_ref, k_ref, v_ref, lw_ref, a_ref, g_ref, mu_ref, kk_ref, ka_ref, rk_ref,
               lnw_ref, lnb_ref, bd_ref, ones_ref, ltri_ref, ts_ref, ti_ref, ic_ref, eye_ref,
               o_ref, st_ref, prev_ref):
    cols = slice(tile * TILE_B, (tile + 1) * TILE_B)
    bdm = bd_ref[...]
    row = lax.broadcasted_iota(I32, (CHUNK, TILE_B), 0)

    def shifted(x, slot):
        prev = prev_ref[slot:slot + 1, cols]
        prev_ref[slot:slot + 1, cols] = x[CHUNK - 1:CHUNK, :]
        return jnp.where(row == 0, prev, pltpu.roll(x, 1, axis=0))

    r0 = r_ref[:, cols]
    k0 = k_ref[:, cols]
    v0 = v_ref[:, cols]
    r = r0 + (shifted(r0, 0) - r0) * mu_ref[0:1, cols]
    k = k0 + (shifted(k0, 1) - k0) * mu_ref[1:2, cols]
    v = v0 + (shifted(v0, 2) - v0) * mu_ref[2:3, cols]
    a = a_ref[:, cols]
    ones_bd = ones_ref[...]

    kk = k * kk_ref[:, cols]
    ss = _dot_hp(kk * kk, ones_bd)
    yield
    kk = kk / jnp.maximum(jnp.sqrt(ss), 1e-12)
    k2 = k * (1.0 + (a - 1.0) * ka_ref[:, cols])
    aa = -kk
    bb = kk * a

    lw = lw_ref[:, cols]
    cs = _cumsum_rows(lw, ltri_ref[...])
    yield
    tot = cs[CHUNK - 1:CHUNK, :]
    e_in = jnp.exp(cs)
    e_out = jnp.exp(-cs)
    e_tail = jnp.exp(tot - cs)
    at = aa * jnp.exp(cs - lw)
    rt = r * e_in
    bt = (bb * e_out)
    kt = (k2 * e_out)
    bh = (bb * e_tail).astype(BF16)
    kh = (k2 * e_tail).astype(BF16)
    wc = jnp.exp(tot)

    def bd(x):
        return (jnp.concatenate([x] * HEADS_PER_TILE, axis=0) * bdm).astype(BF16)

    lhs = jnp.concatenate([at, rt], axis=0).astype(BF16)
    mb = _dot_nt(lhs, bd(bt))
    mk = _dot_nt(lhs, bd(kt))
    yield
    ts = ts_ref[...]
    ti = ti_ref[...]
    m_ab = mb[0:CHUNK] * ts
    m_rb = (mb[CHUNK:2 * CHUNK] * ti).astype(BF16)
    m_ak = (mk[0:CHUNK] * ts).astype(BF16)
    m_rk = (mk[CHUNK:2 * CHUNK] * ti).astype(BF16)

    m_pow = m_ab
    t_inv = ic_ref[...] + m_ab
    n_round = CHUNK.bit_length() - 1
    for _ in range(1, n_round):
        m_pow_bf = m_pow.astype(BF16)
        m_pow = _dot(m_pow_bf, bd(m_pow))
        yield
        t_inv = t_inv + _dot(m_pow.astype(BF16), bd(t_inv))
        yield
    t_bf = t_inv.astype(BF16)

    vbd = bd(v)
    p = _dot(t_bf, bd(at))
    u = _dot(m_ak, vbd)
    yield
    q = _dot(t_bf, bd(u))
    rp = rt + _dot(m_rb, bd(p))
    yield
    y0 = _dot(m_rb, bd(q)) + _dot(m_rk, vbd)

    st = st_ref[tile]
    st_bf = st.astype(BF16)
    y = _dot(rp.astype(BF16), st_bf) + y0

    yield
    a_t = _dot_tn(bh, p.astype(BF16)) * bdm + eye_ref[...] * wc
    d_t = _dot_tn(jnp.concatenate([bh, kh], axis=0),
                  jnp.concatenate([q.astype(BF16), v.astype(BF16)], axis=0)) * bdm
    st_ref[tile] = _dot(a_t.astype(BF16), st_bf) + d_t

    yield
    inv_n = 1.0 / HEAD_DIM_B
    mu = _dot_hp(y, ones_bd) * inv_n
    yield
    yc = y - mu
    var = _dot_hp(yc * yc, ones_bd) * inv_n
    yield
    yn =(yc * lax.rsqrt(var + GN_EPS)) * lnw_ref[:, cols] + lnb_ref[:, cols]
    bonus = _dot_hp(r * k2 * rk_ref[:, cols], ones_bd) * v
    o_ref[:, cols] = ((yn + bonus) * g_ref[:, cols]).astype(o_ref.dtype)


def _cumsum_rows(x, ltri_bf):
    hi = x.astype(BF16)
    r1 = x - hi.astype(F32)
    mid = r1.astype(BF16)
    lo = (r1 - mid.astype(F32)).astype(BF16)
    return _dot(ltri_bf, hi) + _dot(ltri_bf, mid) + _dot(ltri_bf, lo)


def _rwkv_consts():
    idx = jnp.arange(TILE_B)
    head = idx // HEAD_DIM_B
    bdm = (head[:, None] == head[None, :])
    t = jnp.arange(CHUNK)[:, None]
    s = (idx % CHUNK)[None, :]
    return dict(
        bd=bdm.astype(F32),
        ones=bdm.astype(BF16),
        ltri=(jnp.arange(CHUNK)[None, :] <= t).astype(BF16),
        ts=(s < t).astype(F32),
        ti=(s <= t).astype(F32),
        ic=(s == t).astype(F32),
        eye=jnp.eye(TILE_B, dtype=F32),
    )


def _rwkv(rkv, lw, a, g, mu_rkv, k_k, k_a, r_k, ln_w, ln_b, batch, seq_len):
    m, db = lw.shape
    nc = seq_len // CHUNK
    width = RWKV_TILES_PER_STEP * TILE_B
    ng = db // width
    cst = _rwkv_consts()
    blk = lambda off: pl.BlockSpec((CHUNK, width), lambda b, h, c, off=off: (b * nc + c, off + h))
    par = lambda rows: pl.BlockSpec((rows, width), lambda b, h, c: (0, h))
    full = lambda arr: pl.BlockSpec(arr.shape, lambda b, h, c: (0, 0))
    row = lambda p: p.reshape(1, db)
    return pl.pallas_call(
        _rwkv_kernel,
        grid=(batch, ng, nc),
        in_specs=[blk(0), blk(ng), blk(2 * ng), blk(0), blk(0), blk(0),
                  par(3), par(1), par(1), par(1), par(1), par(1),
                  full(cst["bd"]), full(cst["ones"]), full(cst["ltri"]), full(cst["ts"]),
                  full(cst["ti"]), full(cst["ic"]), full(cst["eye"])],
        out_specs=blk(0),
        out_shape=jax.ShapeDtypeStruct((m, db), BF16),
        scratch_shapes=[pltpu.VMEM((RWKV_TILES_PER_STEP, TILE_B, TILE_B), F32),
                        pltpu.VMEM((SUBLANES, width), F32)],
        compiler_params=_cparams(("arbitrary", "arbitrary", "arbitrary"), 32),
        name="rwkv7",
    )(rkv, rkv, rkv, lw, a, g, mu_rkv, row(k_k), row(k_a), row(r_k), row(ln_w), row(ln_b),
      cst["bd"], cst["ones"], cst["ltri"], cst["ts"], cst["ti"], cst["ic"], cst["eye"])


def _merge_kernel(xn_ref, oa_ref, ob_ref, wga_ref, wgb_ref, bga_ref, bgb_ref, wpa_ref, wpb_ref, o_ref):
    xn = xn_ref[...]
    ga = _sigmoid(_dot(xn, wga_ref[...]) + bga_ref[...])
    gb = _sigmoid(_dot(xn, wgb_ref[...]) + bgb_ref[...])
    ya = _dot(oa_ref[...], wpa_ref[...])
    yb = _dot(ob_ref[...], wpb_ref[...])
    o_ref[...] = (ga * ya + gb * yb).astype(o_ref.dtype)


def _merge(xn, oa, ob, w_gate, b_gate, w_pa, w_pb, bm=1024, bn=256):
    m, d = xn.shape
    bm = min(bm, m)
    nj = d // bn
    return pl.pallas_call(
        _merge_kernel,
        grid=(m // bm, nj),
        in_specs=[pl.BlockSpec((bm, d), lambda i, j: (i, 0)),
                  pl.BlockSpec((bm, oa.shape[1]), lambda i, j: (i, 0)),
                  pl.BlockSpec((bm, ob.shape[1]), lambda i, j: (i, 0)),
                  pl.BlockSpec((d, bn), lambda i, j: (0, j)),
                  pl.BlockSpec((d, bn), lambda i, j: (0, j + nj)),
                  pl.BlockSpec((1, bn), lambda i, j: (0, j)),
                  pl.BlockSpec((1, bn), lambda i, j: (0, j + nj)),
                  pl.BlockSpec((w_pa.shape[0], bn), lambda i, j: (0, j)),
                  pl.BlockSpec((w_pb.shape[0], bn), lambda i, j: (0, j))],
        out_specs=pl.BlockSpec((bm, bn), lambda i, j: (i, j)),
        out_shape=jax.ShapeDtypeStruct((m, d), BF16),
        compiler_params=_cparams(("arbitrary", "arbitrary"), 52),
        name="gated_merge",
    )(xn, oa, ob, w_gate, w_gate, b_gate, b_gate, w_pa, w_pb)


def _mm_res_kernel(x_ref, w_ref, res_ref, o_ref):
    o_ref[...] = res_ref[...] + _dot(x_ref[...], w_ref[...])


def _matmul_residual(x, w, res, bm=1024, bn=1024, name="matmul_res"):
    m, k = x.shape
    n = w.shape[1]
    bm = min(bm, m)
    return pl.pallas_call(
        _mm_res_kernel,
        grid=(m // bm, n // bn),
        in_specs=[pl.BlockSpec((bm, k), lambda i, j: (i, 0)),
                  pl.BlockSpec((k, bn), lambda i, j: (0, j)),
                  pl.BlockSpec((bm, bn), lambda i, j: (i, j))],
        out_specs=pl.BlockSpec((bm, bn), lambda i, j: (i, j)),
        out_shape=jax.ShapeDtypeStruct((m, n), F32),
        compiler_params=_cparams(("arbitrary", "arbitrary"), 52),
        name=name,
    )(x, w, res)


def _mm_res_k_kernel(x_ref, w_ref, res_ref, o_ref, acc_ref, *, nk):
    kk = pl.program_id(2)

    @pl.when(kk == 0)
    def _():
        acc_ref[...] = jnp.zeros_like(acc_ref)

    acc_ref[...] += _dot(x_ref[...], w_ref[...])

    @pl.when(kk == nk - 1)
    def _():
        o_ref[...] = res_ref[...] + acc_ref[...]


def _matmul_residual_ksplit(x, w, res, bk, bm=1024, bn=1024, name="matmul_res_k"):
    m, k = x.shape
    n = w.shape[1]
    bm = min(bm, m)
    nk = k // bk
    return pl.pallas_call(
        functools.partial(_mm_res_k_kernel, nk=nk),
        grid=(m // bm, n // bn, nk),
        in_specs=[pl.BlockSpec((bm, bk), lambda i, j, kk: (i, kk)),
                  pl.BlockSpec((bk, bn), lambda i, j, kk: (kk, j)),
                  pl.BlockSpec((bm, bn), lambda i, j, kk: (i, j))],
        out_specs=pl.BlockSpec((bm, bn), lambda i, j, kk: (i, j)),
        out_shape=jax.ShapeDtypeStruct((m, n), F32),
        scratch_shapes=[pltpu.VMEM((bm, bn), F32)],
        compiler_params=_cparams(("arbitrary", "arbitrary", "arbitrary"), 52),
        name=name,
    )(x, w, res)


def _ffn_up_kernel(x_ref, w1_ref, w3_ref, o_ref):
    x = x_ref[...]
    h1 = _dot(x, w1_ref[...])
    h3 = _dot(x, w3_ref[...])
    o_ref[...] = (h1 * _sigmoid(h1) * h3).astype(o_ref.dtype)


def _ffn_up(x, w1, w3, bm=1024, bn=512):
    m, k = x.shape
    n = w1.shape[1]
    bm = min(bm, m)
    return pl.pallas_call(
        _ffn_up_kernel,
        grid=(m // bm, n // bn),
        in_specs=[pl.BlockSpec((bm, k), lambda i, j: (i, 0)),
                  pl.BlockSpec((k, bn), lambda i, j: (0, j)),
                  pl.BlockSpec((k, bn), lambda i, j: (0, j))],
        out_specs=pl.BlockSpec((bm, bn), lambda i, j: (i, j)),
        out_shape=jax.ShapeDtypeStruct((m, n), BF16),
        compiler_params=_cparams(("arbitrary", "arbitrary"), 52),
        name="ffn_up",
    )(x, w1, w3)


def _ple_kernel(hn_ref, wg_ref, p_ref, wp_ref, res_ref, o_ref):
    gate = _sigmoid(_dot(hn_ref[...], wg_ref[...]))
    o_ref[...] = res_ref[...] + gate * _dot(p_ref[...], wp_ref[...])


def _ple(hn, wg, p, wp, res, bm=1024, bn=512):
    m, k = hn.shape
    n = wg.shape[1]
    bm = min(bm, m)
    return pl.pallas_call(
        _ple_kernel,
        grid=(m // bm, n // bn),
        in_specs=[pl.BlockSpec((bm, k), lambda i, j: (i, 0)),
                  pl.BlockSpec((k, bn), lambda i, j: (0, j)),
                  pl.BlockSpec((bm, p.shape[1]), lambda i, j: (i, 0)),
                  pl.BlockSpec((p.shape[1], bn), lambda i, j: (0, j)),
                  pl.BlockSpec((bm, bn), lambda i, j: (i, j))],
        out_specs=pl.BlockSpec((bm, bn), lambda i, j: (i, j)),
        out_shape=jax.ShapeDtypeStruct((m, n), F32),
        compiler_params=_cparams(("arbitrary", "arbitrary"), 52),
        name="ple",
    )(hn, wg, p, wp, res)


def _rmsnorm_kernel(x_ref, g_ref, o_ref):
    o_ref[...] = (_rms_rows(x_ref[...]) * g_ref[...]).astype(o_ref.dtype)


def _rms_kernel(x_ref, o_ref):
    o_ref[...] = _rms_rows(x_ref[...]).astype(o_ref.dtype)


def _rmsnorm(x, gain, out_dtype, bm=512):
    m, d = x.shape
    bm = min(bm, m)
    row_spec = pl.BlockSpec((bm, d), lambda i: (i, 0))
    if gain is None:
        kern, args, specs = _rms_kernel, (x,), [row_spec]
    else:
        kern, args = _rmsnorm_kernel, (x, gain.reshape(1, d))
        specs = [row_spec, pl.BlockSpec((1, d), lambda i: (0, 0))]
    return pl.pallas_call(
        kern,
        grid=(m // bm,),
        in_specs=specs,
        out_specs=row_spec,
        out_shape=jax.ShapeDtypeStruct((m, d), out_dtype),
        compiler_params=_cparams(("arbitrary",), 48),
        name="rmsnorm",
    )(*args)


def _pad_to(a, axis, size):
    pad = [(0, 0)] * a.ndim
    pad[axis] = (0, size - a.shape[axis])
    return jnp.pad(a, pad)


def _layer(h2, p2, cos, sin, batch, seq_len, norm_mix, w_in, mu_rkv, mu_wag, w0, w1, w2, a0, a1, a2,
           g1, g2, k_k, k_a, r_k, ln_w, ln_b, w_pa, w_pb, w_gate, b_gate, w_o, norm_ffn,
           w_ffn1, w_ffn3, w_ffn2, w_ple_gate, w_ple):
    d = h2.shape[1]
    d_a = N_HEADS_A * HEAD_DIM
    kvd = N_KV_A * HEAD_DIM
    d_iq = N_HEADS_IDX * HEAD_DIM
    d_b = w_pb.shape[0]
    o_q, o_k, o_v = 0, d_a, d_a + kvd
    o_qi = o_v + kvd
    o_ki = o_qi + d_iq
    o_wi = o_ki + HEAD_DIM
    o_r = o_wi + N_HEADS_IDX

    bf = lambda a: a.astype(BF16)
    w_qqi = bf(jnp.concatenate([w_in[:, o_q:o_q + d_a], w_in[:, o_qi:o_qi + d_iq]], axis=1))
    w_kkiv = bf(jnp.concatenate([w_in[:, o_k:o_k + kvd], w_in[:, o_ki:o_ki + HEAD_DIM],
                                 w_in[:, o_v:o_v + kvd]], axis=1))
    w_wi = bf(_pad_to(w_in[:, o_wi:o_wi + N_HEADS_IDX], 1, LANES))
    w_rkv = bf(w_in[:, o_r:o_r + 3 * d_b])
    lora = LANES
    w1p, a1p = bf(_pad_to(w1, 1, lora)), bf(_pad_to(a1, 1, lora))
    w2p, a2p = bf(_pad_to(w2, 0, lora)), bf(_pad_to(a2, 0, lora))

    xn, lw, a, g, wi = _prep(h2, norm_mix.reshape(1, d), mu_wag, w1p, w2p, w0.reshape(1, d_b),
                             a1p, a2p, a0.reshape(1, d_b), bf(g1), bf(g2), w_wi, seq_len)

    qh = _proj_rope_headmajor(xn, w_qqi, cos, sin)
    kkiv = _proj_kv(xn, w_kkiv, cos, sin, n_rope=N_KV_A + 1)
    rkv = _matmul(xn, w_rkv, F32, name="proj_rkv")

    kkiv = kkiv.reshape(batch, seq_len, -1)
    kt = jnp.swapaxes(kkiv[:, :, :kvd], 1, 2)
    kit = jnp.swapaxes(kkiv[:, :, kvd:kvd + HEAD_DIM], 1, 2)
    v = kkiv[:, :, kvd + HEAD_DIM:]
    o_att = _attention(qh, wi, kt, kit, v, batch, seq_len)

    o_rwkv = _rwkv(rkv, lw, a, g, mu_rkv, k_k, k_a, r_k.reshape(-1), ln_w, ln_b, batch, seq_len)

    mixed = _merge(xn, o_att, o_rwkv, bf(w_gate), b_gate.reshape(1, -1), bf(w_pa), bf(w_pb))
    h2 = _matmul_residual(mixed, bf(w_o), h2, bn=512, name="out_proj")

    xf = _rmsnorm(h2, norm_ffn, BF16)
    d_ff = w_ffn1.shape[1]
    ff_tile = 1024
    d_ffp = -(-d_ff // ff_tile) * ff_tile
    u = _ffn_up(xf, bf(_pad_to(w_ffn1, 1, d_ffp)), bf(_pad_to(w_ffn3, 1, d_ffp)))
    h2 = _matmul_residual_ksplit(u, bf(_pad_to(w_ffn2, 0, d_ffp)), h2, bk=d_ffp // 4, name="ffn_down")

    hn = _rmsnorm(h2, None, BF16)
    h2 = _ple(hn, bf(w_ple_gate), bf(p2), bf(w_ple), h2)
    return h2


def kernel(x, p, positions, norm_mix, w_in, mu_rkv, mu_wag, w0, w1, w2, a0, a1, a2, g1, g2, k_k, k_a,
           r_k, ln_w, ln_b, w_pa, w_pb, w_gate, b_gate, w_o, norm_ffn, w_ffn1, w_ffn3, w_ffn2,
           w_ple_gate, w_ple, norm_final):
    batch, seq_len, d = x.shape
    depth = p.shape[0]
    h2 = x.reshape(batch * seq_len, d)
    cos, sin = _rope_tables(positions)
    for i in range(depth):
        h2 = _layer(h2, p[i].reshape(batch * seq_len, -1), cos, sin, batch, seq_len,
                    norm_mix[i], w_in[i], mu_rkv[i], mu_wag[i], w0[i], w1[i], w2[i], a0[i], a1[i], a2[i],
                    g1[i], g2[i], k_k[i], k_a[i], r_k[i], ln_w[i], ln_b[i], w_pa[i], w_pb[i], w_gate[i],
                    b_gate[i], w_o[i], norm_ffn[i], w_ffn1[i], w_ffn3[i], w_ffn2[i], w_ple_gate[i],
                    w_ple[i])
    out = _rmsnorm(h2, norm_final, F32)
    return out.reshape(batch, seq_len, d)
```

```python
import functools

import jax
import jax.numpy as jnp
from jax import lax
from jax.experimental import pallas as pl
from jax.experimental.pallas import tpu as pltpu

F32 = jnp.float32
BF16 = jnp.bfloat16
I32 = jnp.int32

N_HEADS_A = 16
HEAD_DIM = 128
N_KV_A = 4
N_HEADS_IDX = 16
TOPK_MAX = 256
Q_BLOCK = 128
ATT_ROW_BLOCK = 64
ROPE_THETA = 10000.0
HEAD_DIM_B = 64
GN_EPS = 64e-5
RMS_EPS = 1e-6

LANES = 128
SUBLANES = 8
MXU_DIM = 256

INT_MIN = -2 ** 31
LOG2_E = 1.4426950408889634

CHUNK = 64
HEADS_PER_TILE = MXU_DIM // HEAD_DIM_B
TILE_B = HEADS_PER_TILE * HEAD_DIM_B
RWKV_TILES_PER_STEP = 4


def _cparams(sem, vmem_mib):
    return pltpu.CompilerParams(dimension_semantics=sem, vmem_limit_bytes=vmem_mib << 20)


def _dot(a, b):
    return jnp.dot(a, b, preferred_element_type=F32)


def _dot_nt(a, b):
    return lax.dot_general(a, b, (((1,), (1,)), ((), ())), preferred_element_type=F32)


def _dot_tn(a, b):
    return lax.dot_general(a, b, (((0,), (0,)), ((), ())), preferred_element_type=F32)


def _sigmoid(x):
    return 1.0 / (1.0 + jnp.exp(-x))


def _rms_rows(x):
    return x * lax.rsqrt(jnp.mean(x * x, axis=-1, keepdims=True) + RMS_EPS)


def _rope_tab_kernel(pos_ref, freq_ref, sign_ref, cos_ref, sin_ref):
    ang = pos_ref[...].astype(F32) * freq_ref[...]
    cos_ref[...] = jnp.cos(ang)
    sin_ref[...] = jnp.sin(ang) * sign_ref[...]


def _rope_tables(positions):
    n = positions.size
    half = HEAD_DIM // 2
    inv_freq = ROPE_THETA ** (-jnp.arange(0, HEAD_DIM, 2, dtype=F32) / HEAD_DIM)
    freq2 = jnp.concatenate([inv_freq, inv_freq]).reshape(1, HEAD_DIM)
    sign = jnp.concatenate([-jnp.ones((half,), F32), jnp.ones((half,), F32)]).reshape(1, HEAD_DIM)
    bm = min(2048, n)
    return pl.pallas_call(
        _rope_tab_kernel,
        grid=(n // bm,),
        in_specs=[pl.BlockSpec((bm, 1), lambda i: (i, 0)),
                  pl.BlockSpec((1, HEAD_DIM), lambda i: (0, 0)),
                  pl.BlockSpec((1, HEAD_DIM), lambda i: (0, 0))],
        out_specs=[pl.BlockSpec((bm, HEAD_DIM), lambda i: (i, 0)),
                   pl.BlockSpec((bm, HEAD_DIM), lambda i: (i, 0))],
        out_shape=[jax.ShapeDtypeStruct((n, HEAD_DIM), F32)] * 2,
        compiler_params=_cparams(("arbitrary",), 32),
        name="rope_tables",
    )(positions.reshape(n, 1), freq2, sign)


def _rope(t, cos, sin):
    return t * cos + pltpu.roll(t, HEAD_DIM // 2, axis=1) * sin


def _prep_kernel(x_ref, xp_ref, gain_ref, mu_ref, w1_ref, w2_ref, w0_ref, a1_ref, a2_ref, a0_ref,
                 g1_ref, g2_ref, wwi_ref,
                 xn_ref, lw_ref, a_ref, g_ref, wi_ref, *, seq_len, bm):
    i = pl.program_id(0)
    gain = gain_ref[...]
    xn = _rms_rows(x_ref[...]) * gain
    prev = (_rms_rows(xp_ref[...]) * gain)[SUBLANES - 1:SUBLANES, :]
    prev = jnp.where((i * bm) % seq_len == 0, jnp.zeros_like(prev), prev)
    row = lax.broadcasted_iota(I32, xn.shape, 0)
    sh = jnp.where(row == 0, prev, pltpu.roll(xn, 1, axis=0))
    xx = sh - xn
    xn_bf = xn.astype(BF16)
    xn_ref[...] = xn_bf
    wi_ref[...] = _dot(xn_bf, wwi_ref[...]) * (N_HEADS_IDX ** -0.5)

    xw = (xn + xx * mu_ref[0:1, :]).astype(BF16)
    hw = jnp.tanh(_dot(xw, w1_ref[...])).astype(BF16)
    wl = w0_ref[...] + _dot(hw, w2_ref[...])
    z = -wl
    softplus = jnp.maximum(z, 0.0) + jnp.log(1.0 + jnp.exp(-jnp.abs(z)))
    w = -softplus - 0.5
    lw_ref[...] = -jnp.exp(w)

    xa = (xn + xx * mu_ref[1:2, :]).astype(BF16)
    ha = _dot(xa, a1_ref[...]).astype(BF16)
    a_ref[...] = _sigmoid(a0_ref[...] + _dot(ha, a2_ref[...]))

    xg = (xn + xx * mu_ref[2:3, :]).astype(BF16)
    hg = _sigmoid(_dot(xg, g1_ref[...])).astype(BF16)
    g_ref[...] = _dot(hg, g2_ref[...])


def _prep(x2, gain, mu_wag, w1, w2, w0, a1, a2, a0, g1, g2, w_in, wi_block, seq_len):
    m, d = x2.shape
    db = w2.shape[1]
    bm = min(128, m)
    full = lambda a: pl.BlockSpec(a.shape, lambda i: (0,) * a.ndim)
    nsub = bm // SUBLANES
    return pl.pallas_call(
        functools.partial(_prep_kernel, seq_len=seq_len, bm=bm),
        grid=(m // bm,),
        in_specs=[pl.BlockSpec((bm, d), lambda i: (i, 0)),
                  pl.BlockSpec((SUBLANES, d), lambda i: (jnp.maximum(i * nsub - 1, 0), 0)),
                  full(gain), full(mu_wag), full(w1), full(w2), full(w0), full(a1), full(a2), full(a0),
                  full(g1), full(g2), pl.BlockSpec((d, LANES), lambda i: (0, wi_block))],
        out_specs=[pl.BlockSpec((bm, d), lambda i: (i, 0)),
                   pl.BlockSpec((bm, db), lambda i: (i, 0)),
                   pl.BlockSpec((bm, db), lambda i: (i, 0)),
                   pl.BlockSpec((bm, db), lambda i: (i, 0)),
                   pl.BlockSpec((bm, LANES), lambda i: (i, 0))],
        out_shape=[jax.ShapeDtypeStruct((m, d), BF16),
                   jax.ShapeDtypeStruct((m, db), F32),
                   jax.ShapeDtypeStruct((m, db), F32),
                   jax.ShapeDtypeStruct((m, db), F32),
                   jax.ShapeDtypeStruct((m, LANES), F32)],
        compiler_params=_cparams(("arbitrary",), 48),
        name="prep",
    )(x2, x2, gain, mu_wag, w1, w2, w0, a1, a2, a0, g1, g2, w_in)


def _proj_rope_hm_kernel(x_ref, w_ref, cos_ref, sin_ref, o_ref):
    acc = _dot(x_ref[...], w_ref[...])
    cos = cos_ref[...]
    sin = sin_ref[...]
    bm, bn = acc.shape
    for j in range(bn // HEAD_DIM):
        t = _rope(acc[:, j * HEAD_DIM:(j + 1) * HEAD_DIM], cos, sin).astype(o_ref.dtype)
        for r in range(bm // Q_BLOCK):
            o_ref[r, j] = t[r * Q_BLOCK:(r + 1) * Q_BLOCK, :]


def _proj_rope_headmajor(xn, w, col_starts, width, cos, sin, bm=1024, bn=1024):
    m, k = xn.shape
    n = width * len(col_starts)
    bm = min(bm, m)
    per = width // bn
    first, second = (c // bn for c in col_starts)

    def w_block(i, j):
        return 0, jnp.where(j < per, first + j, second + j - per)

    return pl.pallas_call(
        _proj_rope_hm_kernel,
        grid=(m // bm, n // bn),
        in_specs=[pl.BlockSpec((bm, k), lambda i, j: (i, 0)),
                  pl.BlockSpec((k, bn), w_block),
                  pl.BlockSpec((bm, HEAD_DIM), lambda i, j: (i, 0)),
                  pl.BlockSpec((bm, HEAD_DIM), lambda i, j: (i, 0))],
        out_specs=pl.BlockSpec((bm // Q_BLOCK, bn // HEAD_DIM, Q_BLOCK, HEAD_DIM),
                               lambda i, j: (i, j, 0, 0)),
        out_shape=jax.ShapeDtypeStruct((m // Q_BLOCK, n // HEAD_DIM, Q_BLOCK, HEAD_DIM), BF16),
        compiler_params=_cparams(("arbitrary", "arbitrary"), 52),
        name="proj_q",
    )(xn, w, cos, sin)


def _proj_kv_kernel(x_ref, wk_ref, wki_ref, wv_ref, cos_ref, sin_ref, o_ref):
    x = x_ref[...]
    cos = cos_ref[...]
    sin = sin_ref[...]
    col = 0
    for w_ref, rotary in ((wk_ref, True), (wki_ref, True), (wv_ref, False)):
        acc = _dot(x, w_ref[...])
        for j in range(acc.shape[1] // HEAD_DIM):
            t = acc[:, j * HEAD_DIM:(j + 1) * HEAD_DIM]
            if rotary:
                t = _rope(t, cos, sin)
            o_ref[:, col:col + HEAD_DIM] = t.astype(o_ref.dtype)
            col += HEAD_DIM


def _proj_kv(xn, w, col_k, col_ki, col_v, cos, sin, bm=1024):
    m, k = xn.shape
    kvd = N_KV_A * HEAD_DIM
    n = 2 * kvd + HEAD_DIM
    bm = min(bm, m)
    return pl.pallas_call(
        _proj_kv_kernel,
        grid=(m // bm,),
        in_specs=[pl.BlockSpec((bm, k), lambda i: (i, 0)),
                  pl.BlockSpec((k, kvd), lambda i: (0, col_k // kvd)),
                  pl.BlockSpec((k, HEAD_DIM), lambda i: (0, col_ki // HEAD_DIM)),
                  pl.BlockSpec((k, kvd), lambda i: (0, col_v // kvd)),
                  pl.BlockSpec((bm, HEAD_DIM), lambda i: (i, 0)),
                  pl.BlockSpec((bm, HEAD_DIM), lambda i: (i, 0))],
        out_specs=pl.BlockSpec((bm, n), lambda i: (i, 0)),
        out_shape=jax.ShapeDtypeStruct((m, n), BF16),
        compiler_params=_cparams(("arbitrary",), 52),
        name="proj_kv",
    )(xn, w, w, w, cos, sin)


def _mm_kernel(x_ref, w_ref, o_ref):
    o_ref[...] = _dot(x_ref[...], w_ref[...]).astype(o_ref.dtype)


def _matmul(x, w, out_dtype, bm=1024, bn=1024, name="matmul"):
    m, k = x.shape
    n = w.shape[1]
    bm = min(bm, m)
    return pl.pallas_call(
        _mm_kernel,
        grid=(m // bm, n // bn),
        in_specs=[pl.BlockSpec((bm, k), lambda i, j: (i, 0)),
                  pl.BlockSpec((k, bn), lambda i, j: (0, j))],
        out_specs=pl.BlockSpec((bm, bn), lambda i, j: (i, j)),
        out_shape=jax.ShapeDtypeStruct((m, n), out_dtype),
        compiler_params=_cparams(("arbitrary", "arbitrary"), 52),
        name=name,
    )(x, w)


def _attn_kernel_rowmajor(q_ref, qi_ref, wi_ref, kt_ref, kit_ref, v_ref, o_ref, key_ref, bias_ref, j_ref,
                 m_ref, acc_ref, alpha_ref, s_ref, p_ref,
                 *, topk, tk, idx_bits):
    i = pl.program_id(1)
    nk = (i + 1) * Q_BLOCK
    nch = (nk + tk - 1) // tk
    n_lt = tk // LANES
    qpos = i * Q_BLOCK + lax.broadcasted_iota(I32, (Q_BLOCK, tk), 0)
    lane = lax.broadcasted_iota(I32, (Q_BLOCK, tk), 1)

    qi = qi_ref[0].reshape(N_HEADS_IDX * Q_BLOCK, HEAD_DIM)
    wi = wi_ref[...] * (HEAD_DIM ** -0.5)

    def score_body(c, carry):
        off = pl.multiple_of(c * tk, tk)
        lg = _dot(qi, kit_ref[0, :, pl.ds(off, tk)])
        sc = jnp.zeros((Q_BLOCK, tk), F32)
        for h in range(N_HEADS_IDX):
            sc = sc + wi[:, h:h + 1] * jnp.maximum(lg[h * Q_BLOCK:(h + 1) * Q_BLOCK, :], 0.0)
        bits = pltpu.bitcast(sc, I32)
        key = bits ^ ((bits >> 31) & 0x7FFFFFFF)
        key = jnp.where(key == -1, 0, key)
        key = jnp.where(off + lane <= qpos, key, INT_MIN)
        key_ref[:, pl.ds(off, tk)] = key
        return carry

    lax.fori_loop(0, nch, score_body, 0)

    def count(pred):
        def body(c, acc):
            off = pl.multiple_of(c * tk, tk)
            x = jnp.where(pred(key_ref[:, pl.ds(off, tk)], off), 1.0, 0.0)
            for j in range(n_lt):
                acc = acc + x[:, j * LANES:(j + 1) * LANES]
            return acc
        acc = lax.fori_loop(0, nch, body, jnp.zeros((Q_BLOCK, LANES), F32))
        return jnp.sum(acc, axis=1, keepdims=True)

    def bit_body(bi, thr):
        cand = thr ^ jnp.left_shift(jnp.int32(1), 31 - bi)
        cnt = count(lambda kc, off: kc >= cand)
        return jnp.where(cnt >= topk, cand, thr)

    thr = lax.fori_loop(0, 32, bit_body, jnp.full((Q_BLOCK, 1), INT_MIN, I32))
    thr = jnp.maximum(thr, INT_MIN + 1)
    cnt_ge = count(lambda kc, off: kc >= thr)
    cnt_gt = count(lambda kc, off: kc > thr)
    need = topk - cnt_gt

    j_ref[...] = jnp.full((Q_BLOCK, LANES), 2 ** idx_bits - 1, I32)

    @pl.when(jnp.max(cnt_ge) > topk)
    def _():
        def jbody(bi, jt):
            cand = jt | jnp.left_shift(jnp.int32(1), idx_bits - 1 - bi)
            f = count(lambda kc, off: jnp.where(kc == thr, off + lane, 2 ** 30) < cand)
            return jnp.where(f <= need, cand, jt)
        jt = lax.fori_loop(0, idx_bits, jbody, jnp.zeros((Q_BLOCK, 1), I32))
        j_ref[...] = jnp.broadcast_to(jt, (Q_BLOCK, LANES))

    jt = j_ref[:, 0:1]

    def bias_body(c, carry):
        off = pl.multiple_of(c * tk, tk)
        kc = key_ref[:, pl.ds(off, tk)]
        tie_pos = jnp.where(kc == thr, off + lane, 2 ** 30)
        sel = jnp.logical_or(kc > thr, tie_pos < jt)
        bias_ref[:, pl.ds(off, tk)] = jnp.where(sel, 0.0, -jnp.inf)
        return carry

    lax.fori_loop(0, nch, bias_body, 0)

    n_rep = N_HEADS_A // N_KV_A
    rows = n_rep * Q_BLOCK
    q_all = q_ref[0].reshape(N_HEADS_A * Q_BLOCK, HEAD_DIM)
    scale2 = (HEAD_DIM ** -0.5) * LOG2_E
    m_ref[...] = jnp.full(m_ref.shape, -1e30, F32)
    acc_ref[...] = jnp.zeros(acc_ref.shape, F32)
    ones_cols = jnp.ones((tk, HEAD_DIM), BF16)
    n_rb = rows // ATT_ROW_BLOCK

    def masked_scores(g, rb, ctx):
        off, slot = ctx[0], ctx[1]
        r0 = rb * ATT_ROW_BLOCK
        b0 = r0 % Q_BLOCK
        return (s_ref[slot, g, r0:r0 + ATT_ROW_BLOCK, :]
                + bias_ref[b0:b0 + ATT_ROW_BLOCK, pl.ds(off, tk)])

    def qk_into(g, off, slot):
        qg = q_all[g * rows:(g + 1) * rows, :]
        s_ref[slot, g] = _dot(qg, kt_ref[0, g * HEAD_DIM:(g + 1) * HEAD_DIM, pl.ds(off, tk)])

    def stage_qk_next(g, ctx):
        qk_into(g, ctx[2], ctx[3])

    def stage_max(g, off):
        for rb in range(n_rb):
            rsl = slice(rb * ATT_ROW_BLOCK, (rb + 1) * ATT_ROW_BLOCK)
            t = masked_scores(g, rb, off)
            mx = t[:, 0:LANES]
            for j in range(1, n_lt):
                mx = jnp.maximum(mx, t[:, j * LANES:(j + 1) * LANES])
            m_old = m_ref[g, rsl, :]
            m_new = jnp.maximum(m_old, jnp.max(mx, axis=1, keepdims=True))
            alpha_ref[g, rsl, :] = jnp.exp2((m_old - m_new) * scale2)
            m_ref[g, rsl, :] = m_new

    def stage_exp(g, off):
        for rb in range(n_rb):
            rsl = slice(rb * ATT_ROW_BLOCK, (rb + 1) * ATT_ROW_BLOCK)
            t = masked_scores(g, rb, off)
            p_ref[g, rsl, :] = jnp.exp2((t - m_ref[g, rsl, 0:1]) * scale2).astype(BF16)

    def stage_pv(g, ctx):
        off = ctx[0]
        v_ext = jnp.concatenate(
            [v_ref[0, pl.ds(off, tk), g * HEAD_DIM:(g + 1) * HEAD_DIM], ones_cols], axis=1)
        alpha = alpha_ref[g]
        acc_ref[g] = jnp.concatenate([alpha, alpha], axis=1) * acc_ref[g] + _dot(p_ref[g], v_ext)

    stages = (stage_qk_next, stage_max, stage_exp, stage_pv)

    def att_body(c, carry):
        off = pl.multiple_of(c * tk, tk)
        ctx = (off, 0, off, 0)
        for step in range(N_KV_A + len(stages) - 1):
            for g in range(N_KV_A):
                if 0 <= step - g < len(stages):
                    stages[step - g](g, ctx)
        return carry

    lax.fori_loop(0, nch, att_body, 0)
    for g in range(N_KV_A):
        acc = acc_ref[g]
        o = acc[:, 0:HEAD_DIM] / acc[:, HEAD_DIM:2 * HEAD_DIM]
        for n in range(n_rep):
            h = g * n_rep + n
            o_ref[:, h * HEAD_DIM:(h + 1) * HEAD_DIM] = o[n * Q_BLOCK:(n + 1) * Q_BLOCK, :].astype(o_ref.dtype)


def _attention_rowmajor(qh, wi, kt, kit, v, batch, seq_len):
    nb = seq_len // Q_BLOCK
    topk = min(TOPK_MAX, seq_len // 4)
    tk = min(512, seq_len)
    kvd = N_KV_A * HEAD_DIM
    rows = N_HEADS_A // N_KV_A * Q_BLOCK
    single = pl.Buffered(1)
    return pl.pallas_call(
        functools.partial(_attn_kernel_rowmajor, topk=topk, tk=tk, idx_bits=seq_len.bit_length()),
        grid=(batch, nb),
        in_specs=[pl.BlockSpec((1, N_HEADS_A, Q_BLOCK, HEAD_DIM), lambda b, i: (b * nb + i, 0, 0, 0)),
                  pl.BlockSpec((1, N_HEADS_IDX, Q_BLOCK, HEAD_DIM), lambda b, i: (b * nb + i, 1, 0, 0)),
                  pl.BlockSpec((Q_BLOCK, LANES), lambda b, i: (b * nb + i, 0)),
                  pl.BlockSpec((1, kvd, seq_len), lambda b, i: (b, 0, 0), pipeline_mode=single),
                  pl.BlockSpec((1, HEAD_DIM, seq_len), lambda b, i: (b, 0, 0), pipeline_mode=single),
                  pl.BlockSpec((1, seq_len, kvd), lambda b, i: (b, 0, 0), pipeline_mode=single)],
        out_specs=pl.BlockSpec((Q_BLOCK, N_HEADS_A * HEAD_DIM), lambda b, i: (b * nb + i, 0)),
        out_shape=jax.ShapeDtypeStruct((batch * seq_len, N_HEADS_A * HEAD_DIM), BF16),
        scratch_shapes=[pltpu.VMEM((Q_BLOCK, seq_len), I32),
                        pltpu.VMEM((Q_BLOCK, seq_len), F32),
                        pltpu.VMEM((Q_BLOCK, LANES), I32),
                        pltpu.VMEM((N_KV_A, rows, LANES), F32),
                        pltpu.VMEM((N_KV_A, rows, 2 * HEAD_DIM), F32),
                        pltpu.VMEM((N_KV_A, rows, LANES), F32),
                        pltpu.VMEM((1, N_KV_A, rows, tk), F32),
                        pltpu.VMEM((N_KV_A, rows, tk), BF16)],
        compiler_params=_cparams(("arbitrary", "arbitrary"), 52),
        name="dsa_attention",
    )(qh, qh, wi, kt, kit, v)


WORD_BITS = 32
KEYS_PER_WORD_GROUP = WORD_BITS * SUBLANES


def _bit_transpose32(words):
    a = list(words)
    j, m = 16, 0x0000FFFF
    while j:
        mask = jnp.int32(m - (1 << 32) if m >= (1 << 31) else m)
        k = 0
        while k < WORD_BITS:
            t = (a[k] ^ lax.shift_right_logical(a[k + j], jnp.full_like(a[k], j))) & mask
            a[k] = a[k] ^ t
            a[k + j] = a[k + j] ^ (t << j)
            k = (k + j + 1) & ~j
        j >>= 1
        m = (m ^ (m << j)) & 0xFFFFFFFF
    return a


def _popcount_rows(words):
    per_sublane = jnp.sum(lax.population_count(words).reshape(-1, SUBLANES, words.shape[1]), axis=0)
    return jnp.sum(per_sublane.astype(F32), axis=0, keepdims=True)


def _attn_kernel(q_ref, qi_ref, wi_ref, k_ref, ki_ref, vt_ref, o_ref,
                 plane_ref, sel_ref, bias_ref, m_ref, acc_ref, alpha_ref, s_ref, p_ref,
                 *, topk, tk, idx_bits):
    i = pl.program_id(1)
    nk = (i + 1) * Q_BLOCK
    nch = (nk + tk - 1) // tk
    n_words = plane_ref.shape[1]
    groups_per_chunk = tk // KEYS_PER_WORD_GROUP

    @pl.when(i == 0)
    def _():
        plane_ref[...] = jnp.zeros_like(plane_ref)

    qi = qi_ref[0].reshape(N_HEADS_IDX * Q_BLOCK, HEAD_DIM)
    wi_t = jnp.transpose(wi_ref[...]) * (HEAD_DIM ** -0.5)
    qpos = i * Q_BLOCK + lax.broadcasted_iota(I32, (tk, Q_BLOCK), 1)
    krow = lax.broadcasted_iota(I32, (tk, Q_BLOCK), 0)

    def score_body(c, carry):
        off = pl.multiple_of(c * tk, tk)
        lg = _dot_nt(ki_ref[0, pl.ds(off, tk), :], qi)
        sc = jnp.zeros((tk, Q_BLOCK), F32)
        for h in range(N_HEADS_IDX):
            sc = sc + wi_t[h:h + 1, :] * jnp.maximum(lg[:, h * Q_BLOCK:(h + 1) * Q_BLOCK], 0.0)
        bits = pltpu.bitcast(sc, I32)
        key = bits ^ ((bits >> 31) & 0x7FFFFFFF)
        key = jnp.where(key == -1, 0, key)
        ukey = jnp.where(off + krow <= qpos, key ^ INT_MIN, 0)
        for gi in range(groups_per_chunk):
            base = gi * KEYS_PER_WORD_GROUP
            planes = _bit_transpose32(
                [ukey[base + t * SUBLANES:base + (t + 1) * SUBLANES, :] for t in range(WORD_BITS)])
            row0 = pl.multiple_of((c * groups_per_chunk + gi) * SUBLANES, SUBLANES)
            for b in range(WORD_BITS):
                plane_ref[b, pl.ds(row0, SUBLANES), :] = planes[WORD_BITS - 1 - b]
        return carry

    lax.fori_loop(0, nch, score_body, 0)

    cand = jnp.full((n_words, Q_BLOCK), -1, I32)
    greater = jnp.zeros((n_words, Q_BLOCK), I32)
    cnt_gt = jnp.zeros((1, Q_BLOCK), F32)
    for b in range(WORD_BITS - 1, -1, -1):
        ones = cand & plane_ref[b]
        cnt = _popcount_rows(ones)
        take = cnt_gt + cnt >= topk
        greater = jnp.where(take, greater, greater | ones)
        cnt_gt = jnp.where(take, cnt_gt, cnt_gt + cnt)
        cand = jnp.where(take, ones, cand ^ ones)

    word_row = lax.broadcasted_iota(I32, (n_words, Q_BLOCK), 0)
    word_pos = (word_row >> 3) * KEYS_PER_WORD_GROUP + (word_row & (SUBLANES - 1))
    qcol = i * Q_BLOCK + lax.broadcasted_iota(I32, (n_words, Q_BLOCK), 1)

    def prefix(limit):
        nt = jnp.clip((limit - word_pos + (SUBLANES - 1)) >> 3, 0, WORD_BITS)
        top = lax.shift_right_arithmetic(jnp.full_like(nt, INT_MIN), jnp.maximum(nt, 1) - 1)
        return jnp.where(nt <= 0, 0, top)

    cand = cand & prefix(qcol + 1)
    need = topk - cnt_gt
    sel_ref[...] = greater | cand

    @pl.when(jnp.max(_popcount_rows(cand) - need) > 0)
    def _():
        def jbody(bi, jt):
            cj = jt | jnp.left_shift(jnp.int32(1), idx_bits - 1 - bi)
            return jnp.where(_popcount_rows(cand & prefix(cj)) <= need, cj, jt)
        jt = lax.fori_loop(0, idx_bits, jbody, jnp.zeros((1, Q_BLOCK), I32))
        sel_ref[...] = greater | (cand & prefix(jt))

    def bias_body(c, carry):
        off = pl.multiple_of(c * tk, tk)
        for gi in range(groups_per_chunk):
            row0 = pl.multiple_of((c * groups_per_chunk + gi) * SUBLANES, SUBLANES)
            w = sel_ref[pl.ds(row0, SUBLANES), :]
            for t in range(WORD_BITS):
                dst = pl.multiple_of(off + gi * KEYS_PER_WORD_GROUP + t * SUBLANES, SUBLANES)
                bias_ref[pl.ds(dst, SUBLANES), :] = jnp.where((w << t) < 0, 0.0, -jnp.inf)
        return carry

    lax.fori_loop(0, nch, bias_body, 0)

    n_rep = N_HEADS_A // N_KV_A
    cols = n_rep * Q_BLOCK
    q_all = q_ref[0].reshape(N_HEADS_A * Q_BLOCK, HEAD_DIM)
    scale2 = (HEAD_DIM ** -0.5) * LOG2_E
    m_ref[...] = jnp.full(m_ref.shape, -1e30, F32)
    acc_ref[...] = jnp.zeros(acc_ref.shape, F32)
    ones_rows = jnp.ones((HEAD_DIM, tk), BF16)

    def masked_scores(g, n, off):
        return s_ref[g, :, n * Q_BLOCK:(n + 1) * Q_BLOCK] + bias_ref[pl.ds(off, tk), :]

    def stage_qk(g, off):
        qg = q_all[g * cols:(g + 1) * cols, :]
        s_ref[g] = _dot_nt(k_ref[0, pl.ds(off, tk), g * HEAD_DIM:(g + 1) * HEAD_DIM], qg)

    def stage_max(g, off):
        for n in range(n_rep):
            csl = slice(n * Q_BLOCK, (n + 1) * Q_BLOCK)
            m_old = m_ref[g, :, csl]
            m_new = jnp.maximum(m_old, jnp.max(masked_scores(g, n, off), axis=0, keepdims=True))
            alpha_ref[g, :, csl] = jnp.exp2((m_old - m_new) * scale2)
            m_ref[g, :, csl] = m_new

    def stage_exp(g, off):
        for n in range(n_rep):
            csl = slice(n * Q_BLOCK, (n + 1) * Q_BLOCK)
            t = masked_scores(g, n, off)
            p_ref[g, :, csl] = jnp.exp2((t - m_ref[g, 0:1, csl]) * scale2).astype(BF16)

    def stage_pv(g, off):
        v_ext = jnp.concatenate(
            [vt_ref[0, g * HEAD_DIM:(g + 1) * HEAD_DIM, pl.ds(off, tk)], ones_rows], axis=0)
        acc_ref[g] = alpha_ref[g, 0:1, :] * acc_ref[g] + _dot(v_ext, p_ref[g])

    stages = (stage_qk, stage_max, stage_exp, stage_pv)

    def att_body(c, carry):
        off = pl.multiple_of(c * tk, tk)
        for step in range(N_KV_A + len(stages) - 1):
            for g in range(N_KV_A):
                if 0 <= step - g < len(stages):
                    stages[step - g](g, off)
        return carry

    lax.fori_loop(0, nch, att_body, 0)
    for g in range(N_KV_A):
        acc = acc_ref[g]
        o_t = acc[0:HEAD_DIM, :] / acc[HEAD_DIM:2 * HEAD_DIM, :]
        for n in range(n_rep):
            h = g * n_rep + n
            o_ref[:, h * HEAD_DIM:(h + 1) * HEAD_DIM] = jnp.transpose(
                o_t[:, n * Q_BLOCK:(n + 1) * Q_BLOCK]).astype(o_ref.dtype)


def _attention(qh, wi, kkiv, vt, batch, seq_len):
    nb = seq_len // Q_BLOCK
    topk = min(TOPK_MAX, seq_len // 4)
    tk = min(512, seq_len)
    kvd = N_KV_A * HEAD_DIM
    cols = N_HEADS_A // N_KV_A * Q_BLOCK
    single = pl.Buffered(1)
    return pl.pallas_call(
        functools.partial(_attn_kernel, topk=topk, tk=tk, idx_bits=seq_len.bit_length()),
        grid=(batch, nb),
        in_specs=[pl.BlockSpec((1, N_HEADS_A, Q_BLOCK, HEAD_DIM), lambda b, i: (b * nb + i, 0, 0, 0)),
                  pl.BlockSpec((1, N_HEADS_IDX, Q_BLOCK, HEAD_DIM), lambda b, i: (b * nb + i, 1, 0, 0)),
                  pl.BlockSpec((Q_BLOCK, LANES), lambda b, i: (b * nb + i, 0)),
                  pl.BlockSpec((1, seq_len, kvd), lambda b, i: (b, 0, 0), pipeline_mode=single),
                  pl.BlockSpec((1, seq_len, HEAD_DIM), lambda b, i: (b, 0, kvd // HEAD_DIM),
                               pipeline_mode=single),
                  pl.BlockSpec((1, kvd, seq_len), lambda b, i: (b, 0, 0), pipeline_mode=single)],
        out_specs=pl.BlockSpec((Q_BLOCK, N_HEADS_A * HEAD_DIM), lambda b, i: (b * nb + i, 0)),
        out_shape=jax.ShapeDtypeStruct((batch * seq_len, N_HEADS_A * HEAD_DIM), BF16),
        scratch_shapes=[pltpu.VMEM((WORD_BITS, seq_len // WORD_BITS, Q_BLOCK), I32),
                        pltpu.VMEM((seq_len // WORD_BITS, Q_BLOCK), I32),
                        pltpu.VMEM((seq_len, Q_BLOCK), F32),
                        pltpu.VMEM((N_KV_A, SUBLANES, cols), F32),
                        pltpu.VMEM((N_KV_A, 2 * HEAD_DIM, cols), F32),
                        pltpu.VMEM((N_KV_A, SUBLANES, cols), F32),
                        pltpu.VMEM((N_KV_A, tk, cols), F32),
                        pltpu.VMEM((N_KV_A, tk, cols), BF16)],
        compiler_params=_cparams(("arbitrary", "arbitrary"), 52),
        name="dsa_attention",
    )(qh, qh, wi, kkiv, kkiv, vt)


def _split2(x):
    hi = x.astype(BF16)
    lo = (x - hi.astype(F32)).astype(BF16)
    return hi, lo


def _dot_hp(x, w_bf):
    hi = x.astype(BF16)
    r1 = x - hi.astype(F32)
    mid = r1.astype(BF16)
    lo = (r1 - mid.astype(F32)).astype(BF16)
    return _dot(hi, w_bf) + _dot(mid, w_bf) + _dot(lo, w_bf)


def _rwkv_kernel(r_ref, k_ref, v_ref, lw_ref, a_ref, g_ref, mu_ref, kk_ref, ka_ref, rk_ref,
                 lnw_ref, lnb_ref, bd_ref, ones_ref, ltri_ref, ts_ref, ti_ref, ic_ref, eye_ref,
                 o_ref, st_ref, prev_ref):
    @pl.when(pl.program_id(2) == 0)
    def _():
        st_ref[...] = jnp.zeros_like(st_ref)
        prev_ref[...] = jnp.zeros_like(prev_ref)

    tiles = [_rwkv_tile(tile, r_ref, k_ref, v_ref, lw_ref, a_ref, g_ref, mu_ref, kk_ref, ka_ref, rk_ref,
                        lnw_ref, lnb_ref, bd_ref, ones_ref, ltri_ref, ts_ref, ti_ref, ic_ref, eye_ref,
                        o_ref, st_ref, prev_ref)
             for tile in range(r_ref.shape[1] // TILE_B)]
    while tiles:
        tiles = [t for t in tiles if next(t, "done") != "done"]


def _rwkv_tile(tile, r_ref, k_ref, v_ref, lw_ref, a_ref, g_ref, mu_ref, kk_ref, ka_ref, rk_ref,
               lnw_ref, lnb_ref, bd_ref, ones_ref, ltri_ref, ts_ref, ti_ref, ic_ref, eye_ref,
               o_ref, st_ref, prev_ref):
    cols = slice(tile * TILE_B, (tile + 1) * TILE_B)
    bdm = bd_ref[...]
    row = lax.broadcasted_iota(I32, (CHUNK, TILE_B), 0)

    def shifted(x, slot):
        prev = prev_ref[slot:slot + 1, cols]
        prev_ref[slot:slot + 1, cols] = x[CHUNK - 1:CHUNK, :]
        return jnp.where(row == 0, prev, pltpu.roll(x, 1, axis=0))

    r0 = r_ref[:, cols]
    k0 = k_ref[:, cols]
    v0 = v_ref[:, cols]
    r = r0 + (shifted(r0, 0) - r0) * mu_ref[0:1, cols]
    k = k0 + (shifted(k0, 1) - k0) * mu_ref[1:2, cols]
    v = v0 + (shifted(v0, 2) - v0) * mu_ref[2:3, cols]
    a = a_ref[:, cols]
    ones_bd = ones_ref[...]

    kk = k * kk_ref[:, cols]
    ss = _dot_hp(kk * kk, ones_bd)
    yield
    kk = kk / jnp.maximum(jnp.sqrt(ss), 1e-12)
    k2 = k * (1.0 + (a - 1.0) * ka_ref[:, cols])
    aa = -kk
    bb = kk * a

    lw = lw_ref[:, cols]
    cs = _cumsum_rows(lw, ltri_ref[...])
    yield
    tot = cs[CHUNK - 1:CHUNK, :]
    e_in = jnp.exp(cs)
    e_out = jnp.exp(-cs)
    e_tail = jnp.exp(tot - cs)
    at = aa * jnp.exp(cs - lw)
    rt = r * e_in
    bt = (bb * e_out)
    kt = (k2 * e_out)
    bh = (bb * e_tail).astype(BF16)
    kh = (k2 * e_tail).astype(BF16)
    wc = jnp.exp(tot)

    def bd(x):
        return (jnp.concatenate([x] * HEADS_PER_TILE, axis=0) * bdm).astype(BF16)

    lhs = jnp.concatenate([at, rt], axis=0).astype(BF16)
    mb = _dot_nt(lhs, bd(bt))
    mk = _dot_nt(lhs, bd(kt))
    yield
    ts = ts_ref[...]
    ti = ti_ref[...]
    m_ab = mb[0:CHUNK] * ts
    m_rb = (mb[CHUNK:2 * CHUNK] * ti).astype(BF16)
    m_ak = (mk[0:CHUNK] * ts).astype(BF16)
    m_rk = (mk[CHUNK:2 * CHUNK] * ti).astype(BF16)

    m_pow = m_ab
    t_inv = ic_ref[...] + m_ab
    n_round = CHUNK.bit_length() - 1
    for _ in range(1, n_round):
        m_pow_bf = m_pow.astype(BF16)
        m_pow = _dot(m_pow_bf, bd(m_pow))
        yield
        t_inv = t_inv + _dot(m_pow.astype(BF16), bd(t_inv))
        yield
    t_bf = t_inv.astype(BF16)

    vbd = bd(v)
    p = _dot(t_bf, bd(at))
    u = _dot(m_ak, vbd)
    yield
    q = _dot(t_bf, bd(u))
    rp = rt + _dot(m_rb, bd(p))
    yield
    y0 = _dot(m_rb, bd(q)) + _dot(m_rk, vbd)

    st = st_ref[tile]
    st_bf = st.astype(BF16)
    y = _dot(rp.astype(BF16), st_bf) + y0

    yield
    a_t = _dot_tn(bh, p.astype(BF16)) * bdm + eye_ref[...] * wc
    d_t = _dot_tn(jnp.concatenate([bh, kh], axis=0),
                  jnp.concatenate([q.astype(BF16), v.astype(BF16)], axis=0)) * bdm
    st_ref[tile] = _dot(a_t.astype(BF16), st_bf) + d_t

    yield
    inv_n = 1.0 / HEAD_DIM_B
    mu = _dot_hp(y, ones_bd) * inv_n
    yield
    yc = y - mu
    var = _dot_hp(yc * yc, ones_bd) * inv_n
    yield
    yn =(yc * lax.rsqrt(var + GN_EPS)) * lnw_ref[:, cols] + lnb_ref[:, cols]
    bonus = _dot_hp(r * k2 * rk_ref[:, cols], ones_bd) * v
    o_ref[:, cols] = ((yn + bonus) * g_ref[:, cols]).astype(o_ref.dtype)


def _cumsum_rows(x, ltri_bf):
    hi = x.astype(BF16)
    r1 = x - hi.astype(F32)
    mid = r1.astype(BF16)
    lo = (r1 - mid.astype(F32)).astype(BF16)
    return _dot(ltri_bf, hi) + _dot(ltri_bf, mid) + _dot(ltri_bf, lo)


def _rwkv_consts():
    idx = jnp.arange(TILE_B)
    head = idx // HEAD_DIM_B
    bdm = (head[:, None] == head[None, :])
    t = jnp.arange(CHUNK)[:, None]
    s = (idx % CHUNK)[None, :]
    return dict(
        bd=bdm.astype(F32),
        ones=bdm.astype(BF16),
        ltri=(jnp.arange(CHUNK)[None, :] <= t).astype(BF16),
        ts=(s < t).astype(F32),
        ti=(s <= t).astype(F32),
        ic=(s == t).astype(F32),
        eye=jnp.eye(TILE_B, dtype=F32),
    )


def _rwkv(rkv, lw, a, g, mu_rkv, k_k, k_a, r_k, ln_w, ln_b, batch, seq_len):
    m, db = lw.shape
    nc = seq_len // CHUNK
    width = RWKV_TILES_PER_STEP * TILE_B
    ng = db // width
    cst = _rwkv_consts()
    blk = lambda off: pl.BlockSpec((CHUNK, width), lambda b, h, c, off=off: (b * nc + c, off + h))
    par = lambda rows: pl.BlockSpec((rows, width), lambda b, h, c: (0, h))
    full = lambda arr: pl.BlockSpec(arr.shape, lambda b, h, c: (0, 0))
    row = lambda p: p.reshape(1, db)
    return pl.pallas_call(
        _rwkv_kernel,
        grid=(batch, ng, nc),
        in_specs=[blk(0), blk(ng), blk(2 * ng), blk(0), blk(0), blk(0),
                  par(3), par(1), par(1), par(1), par(1), par(1),
                  full(cst["bd"]), full(cst["ones"]), full(cst["ltri"]), full(cst["ts"]),
                  full(cst["ti"]), full(cst["ic"]), full(cst["eye"])],
        out_specs=blk(0),
        out_shape=jax.ShapeDtypeStruct((m, db), BF16),
        scratch_shapes=[pltpu.VMEM((RWKV_TILES_PER_STEP, TILE_B, TILE_B), F32),
                        pltpu.VMEM((SUBLANES, width), F32)],
        compiler_params=_cparams(("arbitrary", "arbitrary", "arbitrary"), 32),
        name="rwkv7",
    )(rkv, rkv, rkv, lw, a, g, mu_rkv, row(k_k), row(k_a), row(r_k), row(ln_w), row(ln_b),
      cst["bd"], cst["ones"], cst["ltri"], cst["ts"], cst["ti"], cst["ic"], cst["eye"])


def _merge_kernel(xn_ref, oa_ref, ob_ref, wga_ref, wgb_ref, bga_ref, bgb_ref, wpa_ref, wpb_ref, o_ref):
    xn = xn_ref[...]
    ga = _sigmoid(_dot(xn, wga_ref[...]) + bga_ref[...])
    gb = _sigmoid(_dot(xn, wgb_ref[...]) + bgb_ref[...])
    ya = _dot(oa_ref[...], wpa_ref[...])
    yb = _dot(ob_ref[...], wpb_ref[...])
    o_ref[...] = (ga * ya + gb * yb).astype(o_ref.dtype)


def _merge(xn, oa, ob, w_gate, b_gate, w_pa, w_pb, bm=1024, bn=256):
    m, d = xn.shape
    bm = min(bm, m)
    nj = d // bn
    return pl.pallas_call(
        _merge_kernel,
        grid=(m // bm, nj),
        in_specs=[pl.BlockSpec((bm, d), lambda i, j: (i, 0)),
                  pl.BlockSpec((bm, oa.shape[1]), lambda i, j: (i, 0)),
                  pl.BlockSpec((bm, ob.shape[1]), lambda i, j: (i, 0)),
                  pl.BlockSpec((d, bn), lambda i, j: (0, j)),
                  pl.BlockSpec((d, bn), lambda i, j: (0, j + nj)),
                  pl.BlockSpec((1, bn), lambda i, j: (0, j)),
                  pl.BlockSpec((1, bn), lambda i, j: (0, j + nj)),
                  pl.BlockSpec((w_pa.shape[0], bn), lambda i, j: (0, j)),
                  pl.BlockSpec((w_pb.shape[0], bn), lambda i, j: (0, j))],
        out_specs=pl.BlockSpec((bm, bn), lambda i, j: (i, j)),
        out_shape=jax.ShapeDtypeStruct((m, d), BF16),
        compiler_params=_cparams(("arbitrary", "arbitrary"), 52),
        name="gated_merge",
    )(xn, oa, ob, w_gate, w_gate, b_gate, b_gate, w_pa, w_pb)


def _mm_res_kernel(x_ref, w_ref, res_ref, o_ref):
    o_ref[...] = res_ref[...] + _dot(x_ref[...], w_ref[...])


def _matmul_residual(x, w, res, bm=1024, bn=1024, name="matmul_res"):
    m, k = x.shape
    n = w.shape[1]
    bm = min(bm, m)
    return pl.pallas_call(
        _mm_res_kernel,
        grid=(m // bm, n // bn),
        in_specs=[pl.BlockSpec((bm, k), lambda i, j: (i, 0)),
                  pl.BlockSpec((k, bn), lambda i, j: (0, j)),
                  pl.BlockSpec((bm, bn), lambda i, j: (i, j))],
        out_specs=pl.BlockSpec((bm, bn), lambda i, j: (i, j)),
        out_shape=jax.ShapeDtypeStruct((m, n), F32),
        compiler_params=_cparams(("arbitrary", "arbitrary"), 52),
        name=name,
    )(x, w, res)


def _mm_res_k_kernel(x_ref, w_ref, res_ref, o_ref, acc_ref, *, nk):
    kk = pl.program_id(2)

    @pl.when(kk == 0)
    def _():
        acc_ref[...] = jnp.zeros_like(acc_ref)

    acc_ref[...] += _dot(x_ref[...], w_ref[...])

    @pl.when(kk == nk - 1)
    def _():
        o_ref[...] = res_ref[...] + acc_ref[...]


def _matmul_residual_ksplit(x, w, res, bk, bm=1024, bn=1024, name="matmul_res_k"):
    m, k = x.shape
    n = w.shape[1]
    bm = min(bm, m)
    nk = k // bk
    return pl.pallas_call(
        functools.partial(_mm_res_k_kernel, nk=nk),
        grid=(m // bm, n // bn, nk),
        in_specs=[pl.BlockSpec((bm, bk), lambda i, j, kk: (i, kk)),
                  pl.BlockSpec((bk, bn), lambda i, j, kk: (kk, j)),
                  pl.BlockSpec((bm, bn), lambda i, j, kk: (i, j))],
        out_specs=pl.BlockSpec((bm, bn), lambda i, j, kk: (i, j)),
        out_shape=jax.ShapeDtypeStruct((m, n), F32),
        scratch_shapes=[pltpu.VMEM((bm, bn), F32)],
        compiler_params=_cparams(("arbitrary", "arbitrary", "arbitrary"), 52),
        name=name,
    )(x, w, res)


def _ffn_up_kernel(x_ref, w1_ref, w3_ref, o_ref):
    x = x_ref[...]
    h1 = _dot(x, w1_ref[...])
    h3 = _dot(x, w3_ref[...])
    o_ref[...] = (h1 * _sigmoid(h1) * h3).astype(o_ref.dtype)


def _ffn_up(x, w1, w3, bm=1024, bn=512):
    m, k = x.shape
    n = w1.shape[1]
    bm = min(bm, m)
    return pl.pallas_call(
        _ffn_up_kernel,
        grid=(m // bm, pl.cdiv(n, bn)),
        in_specs=[pl.BlockSpec((bm, k), lambda i, j: (i, 0)),
                  pl.BlockSpec((k, bn), lambda i, j: (0, j)),
                  pl.BlockSpec((k, bn), lambda i, j: (0, j))],
        out_specs=pl.BlockSpec((bm, bn), lambda i, j: (i, j)),
        out_shape=jax.ShapeDtypeStruct((m, n), BF16),
        compiler_params=_cparams(("arbitrary", "arbitrary"), 52),
        name="ffn_up",
    )(x, w1, w3)


def _ple_kernel(hn_ref, wg_ref, p_ref, wp_ref, res_ref, o_ref):
    gate = _sigmoid(_dot(hn_ref[...], wg_ref[...]))
    o_ref[...] = res_ref[...] + gate * _dot(p_ref[...], wp_ref[...])


def _ple(hn, wg, p, wp, res, bm=1024, bn=512):
    m, k = hn.shape
    n = wg.shape[1]
    bm = min(bm, m)
    return pl.pallas_call(
        _ple_kernel,
        grid=(m // bm, n // bn),
        in_specs=[pl.BlockSpec((bm, k), lambda i, j: (i, 0)),
                  pl.BlockSpec((k, bn), lambda i, j: (0, j)),
                  pl.BlockSpec((bm, p.shape[1]), lambda i, j: (i, 0)),
                  pl.BlockSpec((p.shape[1], bn), lambda i, j: (0, j)),
                  pl.BlockSpec((bm, bn), lambda i, j: (i, j))],
        out_specs=pl.BlockSpec((bm, bn), lambda i, j: (i, j)),
        out_shape=jax.ShapeDtypeStruct((m, n), F32),
        compiler_params=_cparams(("arbitrary", "arbitrary"), 52),
        name="ple",
    )(hn, wg, p, wp, res)


def _rmsnorm_kernel(x_ref, g_ref, o_ref):
    o_ref[...] = (_rms_rows(x_ref[...]) * g_ref[...]).astype(o_ref.dtype)


def _rms_kernel(x_ref, o_ref):
    o_ref[...] = _rms_rows(x_ref[...]).astype(o_ref.dtype)


def _rmsnorm(x, gain, out_dtype, bm=512):
    m, d = x.shape
    bm = min(bm, m)
    row_spec = pl.BlockSpec((bm, d), lambda i: (i, 0))
    if gain is None:
        kern, args, specs = _rms_kernel, (x,), [row_spec]
    else:
        kern, args = _rmsnorm_kernel, (x, gain.reshape(1, d))
        specs = [row_spec, pl.BlockSpec((1, d), lambda i: (0, 0))]
    return pl.pallas_call(
        kern,
        grid=(m // bm,),
        in_specs=specs,
        out_specs=row_spec,
        out_shape=jax.ShapeDtypeStruct((m, d), out_dtype),
        compiler_params=_cparams(("arbitrary",), 48),
        name="rmsnorm",
    )(*args)


def _pad_to(a, axis, size):
    pad = [(0, 0)] * a.ndim
    pad[axis] = (0, size - a.shape[axis])
    return jnp.pad(a, pad)


def _layer(h2, p2, cos, sin, batch, seq_len, norm_mix, w_in, mu_rkv, mu_wag, w0, w1, w2, a0, a1, a2,
           g1, g2, k_k, k_a, r_k, ln_w, ln_b, w_pa, w_pb, w_gate, b_gate, w_o, norm_ffn,
           w_ffn1, w_ffn3, w_ffn2, w_ple_gate, w_ple):
    d = h2.shape[1]
    d_a = N_HEADS_A * HEAD_DIM
    kvd = N_KV_A * HEAD_DIM
    d_iq = N_HEADS_IDX * HEAD_DIM
    d_b = w_pb.shape[0]
    o_q, o_k, o_v = 0, d_a, d_a + kvd
    o_qi = o_v + kvd
    o_ki = o_qi + d_iq
    o_wi = o_ki + HEAD_DIM
    o_r = o_wi + N_HEADS_IDX

    bf = lambda a: a.astype(BF16)
    w_in_bf = bf(w_in)
    assert d_a == d_iq and o_wi % LANES == 0
    w_rkv = w_in_bf[:, o_r:o_r + 3 * d_b]
    lora = LANES
    w1p, a1p = bf(_pad_to(w1, 1, lora)), bf(_pad_to(a1, 1, lora))
    w2p, a2p = bf(_pad_to(w2, 0, lora)), bf(_pad_to(a2, 0, lora))

    xn, lw, a, g, wi = _prep(h2, norm_mix.reshape(1, d), mu_wag, w1p, w2p, w0.reshape(1, d_b),
                             a1p, a2p, a0.reshape(1, d_b), bf(g1), bf(g2), w_in_bf, o_wi // LANES,
                             seq_len)

    qh = _proj_rope_headmajor(xn, w_in_bf, (o_q, o_qi), d_a, cos, sin)
    kkiv = _proj_kv(xn, w_in_bf, o_k, o_ki, o_v, cos, sin)
    rkv = _matmul(xn, w_rkv, F32, name="proj_rkv")

    kkiv = kkiv.reshape(batch, seq_len, -1)
    vt = jnp.swapaxes(kkiv[:, :, kvd + HEAD_DIM:], 1, 2)
    o_att = _attention(qh, wi, kkiv, vt, batch, seq_len)

    o_rwkv = _rwkv(rkv, lw, a, g, mu_rkv, k_k, k_a, r_k.reshape(-1), ln_w, ln_b, batch, seq_len)

    mixed = _merge(xn, o_att, o_rwkv, bf(w_gate), b_gate.reshape(1, -1), bf(w_pa), bf(w_pb))
    h2 = _matmul_residual(mixed, bf(w_o), h2, bn=512, name="out_proj")

    xf = _rmsnorm(h2, norm_ffn, BF16)
    d_ff = w_ffn1.shape[1]
    u = _ffn_up(xf, bf(w_ffn1), bf(w_ffn3))
    h2 = _matmul_residual_ksplit(u, bf(w_ffn2), h2, bk=d_ff // 2, bn=512, name="ffn_down")

    hn = _rmsnorm(h2, None, BF16)
    h2 = _ple(hn, bf(w_ple_gate), bf(p2), bf(w_ple), h2)
    return h2


def kernel(x, p, positions, norm_mix, w_in, mu_rkv, mu_wag, w0, w1, w2, a0, a1, a2, g1, g2, k_k, k_a,
           r_k, ln_w, ln_b, w_pa, w_pb, w_gate, b_gate, w_o, norm_ffn, w_ffn1, w_ffn3, w_ffn2,
           w_ple_gate, w_ple, norm_final):
    batch, seq_len, d = x.shape
    depth = p.shape[0]
    h2 = x.reshape(batch * seq_len, d)
    cos, sin = _rope_tables(positions)
    for i in range(depth):
        h2 = _layer(h2, p[i].reshape(batch * seq_len, -1), cos, sin, batch, seq_len,
                    norm_mix[i], w_in[i], mu_rkv[i], mu_wag[i], w0[i], w1[i], w2[i], a0[i], a1[i], a2[i],
                    g1[i], g2[i], k_k[i], k_a[i], r_k[i], ln_w[i], ln_b[i], w_pa[i], w_pb[i], w_gate[i],
                    b_gate[i], w_o[i], norm_ffn[i], w_ffn1[i], w_ffn3[i], w_ffn2[i], w_ple_gate[i],
                    w_ple[i])
    out = _rmsnorm(h2, norm_final, F32)
    return out.reshape(batch, seq_len, d)
```

```python
import functools

import jax
import jax.numpy as jnp
from jax import lax
from jax.experimental import pallas as pl
from jax.experimental.pallas import tpu as pltpu

F32 = jnp.float32
BF16 = jnp.bfloat16
I32 = jnp.int32

N_HEADS_A = 16
HEAD_DIM = 128
N_KV_A = 4
N_HEADS_IDX = 16
TOPK_MAX = 256
Q_BLOCK = 128
ROPE_THETA = 10000.0
HEAD_DIM_B = 64
GN_EPS = 64e-5
RMS_EPS = 1e-6

LANES = 128
SUBLANES = 8
MXU_DIM = 256

INT_MIN = -2 ** 31
LOG2_E = 1.4426950408889634
DECAY_SCALE = 0.6065306597126334

CHUNK = 64
HEADS_PER_TILE = MXU_DIM // HEAD_DIM_B
TILE_B = HEADS_PER_TILE * HEAD_DIM_B
RWKV_TILES_PER_STEP = 4


def _cparams(sem, vmem_mib):
    return pltpu.CompilerParams(dimension_semantics=sem, vmem_limit_bytes=vmem_mib << 20)


def _dot(a, b):
    return jnp.dot(a, b, preferred_element_type=F32)


def _dot_nt(a, b):
    return lax.dot_general(a, b, (((1,), (1,)), ((), ())), preferred_element_type=F32)


def _dot_tn(a, b):
    return lax.dot_general(a, b, (((0,), (0,)), ((), ())), preferred_element_type=F32)


def _sigmoid(x):
    return 1.0 / (1.0 + jnp.exp(-x))


def _rms_rows(x):
    return x * lax.rsqrt(jnp.mean(x * x, axis=-1, keepdims=True) + RMS_EPS)


def _rope_tab_kernel(pos_ref, freq_ref, sign_ref, cos_ref, sin_ref):
    ang = pos_ref[...].astype(F32) * freq_ref[...]
    cos_ref[...] = jnp.cos(ang)
    sin_ref[...] = jnp.sin(ang) * sign_ref[...]


def _rope_tables(positions):
    n = positions.size
    half = HEAD_DIM // 2
    inv_freq = ROPE_THETA ** (-jnp.arange(0, HEAD_DIM, 2, dtype=F32) / HEAD_DIM)
    freq2 = jnp.concatenate([inv_freq, inv_freq]).reshape(1, HEAD_DIM)
    sign = jnp.concatenate([-jnp.ones((half,), F32), jnp.ones((half,), F32)]).reshape(1, HEAD_DIM)
    bm = min(2048, n)
    return pl.pallas_call(
        _rope_tab_kernel,
        grid=(n // bm,),
        in_specs=[pl.BlockSpec((bm, 1), lambda i: (i, 0)),
                  pl.BlockSpec((1, HEAD_DIM), lambda i: (0, 0)),
                  pl.BlockSpec((1, HEAD_DIM), lambda i: (0, 0))],
        out_specs=[pl.BlockSpec((bm, HEAD_DIM), lambda i: (i, 0)),
                   pl.BlockSpec((bm, HEAD_DIM), lambda i: (i, 0))],
        out_shape=[jax.ShapeDtypeStruct((n, HEAD_DIM), F32)] * 2,
        compiler_params=_cparams(("arbitrary",), 32),
        name="rope_tables",
    )(positions.reshape(n, 1), freq2, sign)


def _rope(t, cos, sin):
    return t * cos + pltpu.roll(t, HEAD_DIM // 2, axis=1) * sin


def _prep_kernel(x_ref, xp_ref, gain_ref, mu_ref, w1_ref, w2_ref, w0_ref, a1_ref, a2_ref, a0_ref,
                 g1_ref, g2_ref, wwi_ref,
                 xn_ref, lw_ref, a_ref, g_ref, wi_ref, *, seq_len, bm):
    i = pl.program_id(0)
    gain = gain_ref[...]
    xn = _rms_rows(x_ref[...]) * gain
    prev = (_rms_rows(xp_ref[...]) * gain)[SUBLANES - 1:SUBLANES, :]
    prev = jnp.where((i * bm) % seq_len == 0, jnp.zeros_like(prev), prev)
    row = lax.broadcasted_iota(I32, xn.shape, 0)
    sh = jnp.where(row == 0, prev, pltpu.roll(xn, 1, axis=0))
    xx = sh - xn
    xn_bf = xn.astype(BF16)
    xn_ref[...] = xn_bf
    wi_ref[...] = _dot(xn_bf, wwi_ref[...]) * (N_HEADS_IDX ** -0.5)

    xw = (xn + xx * mu_ref[0:1, :]).astype(BF16)
    hw = jnp.tanh(_dot(xw, w1_ref[...])).astype(BF16)
    wl = w0_ref[...] + _dot(hw, w2_ref[...])
    lw_ref[...] = -DECAY_SCALE * _sigmoid(wl)

    xa = (xn + xx * mu_ref[1:2, :]).astype(BF16)
    ha = _dot(xa, a1_ref[...]).astype(BF16)
    a_ref[...] = _sigmoid(a0_ref[...] + _dot(ha, a2_ref[...]))

    xg = (xn + xx * mu_ref[2:3, :]).astype(BF16)
    hg = _sigmoid(_dot(xg, g1_ref[...])).astype(BF16)
    g_ref[...] = _dot(hg, g2_ref[...])


def _prep(x2, gain, mu_wag, w1, w2, w0, a1, a2, a0, g1, g2, w_in, wi_block, seq_len):
    m, d = x2.shape
    db = w2.shape[1]
    bm = min(128, m)
    full = lambda a: pl.BlockSpec(a.shape, lambda i: (0,) * a.ndim)
    nsub = bm // SUBLANES
    return pl.pallas_call(
        functools.partial(_prep_kernel, seq_len=seq_len, bm=bm),
        grid=(m // bm,),
        in_specs=[pl.BlockSpec((bm, d), lambda i: (i, 0)),
                  pl.BlockSpec((SUBLANES, d), lambda i: (jnp.maximum(i * nsub - 1, 0), 0)),
                  full(gain), full(mu_wag), full(w1), full(w2), full(w0), full(a1), full(a2), full(a0),
                  full(g1), full(g2), pl.BlockSpec((d, LANES), lambda i: (0, wi_block))],
        out_specs=[pl.BlockSpec((bm, d), lambda i: (i, 0)),
                   pl.BlockSpec((bm, db), lambda i: (i, 0)),
                   pl.BlockSpec((bm, db), lambda i: (i, 0)),
                   pl.BlockSpec((bm, db), lambda i: (i, 0)),
                   pl.BlockSpec((bm, LANES), lambda i: (i, 0))],
        out_shape=[jax.ShapeDtypeStruct((m, d), BF16),
                   jax.ShapeDtypeStruct((m, db), F32),
                   jax.ShapeDtypeStruct((m, db), F32),
                   jax.ShapeDtypeStruct((m, db), F32),
                   jax.ShapeDtypeStruct((m, LANES), F32)],
        compiler_params=_cparams(("arbitrary",), 48),
        name="prep",
    )(x2, x2, gain, mu_wag, w1, w2, w0, a1, a2, a0, g1, g2, w_in)


def _proj_rope_hm_kernel(x_ref, w_ref, cos_ref, sin_ref, o_ref):
    acc = _dot(x_ref[...], w_ref[...])
    cos = cos_ref[...]
    sin = sin_ref[...]
    bm, bn = acc.shape
    for j in range(bn // HEAD_DIM):
        t = _rope(acc[:, j * HEAD_DIM:(j + 1) * HEAD_DIM], cos, sin).astype(o_ref.dtype)
        for r in range(bm // Q_BLOCK):
            o_ref[r, j] = t[r * Q_BLOCK:(r + 1) * Q_BLOCK, :]


def _proj_rope_headmajor(xn, w, col_starts, width, cos, sin, bm=1024, bn=1024):
    m, k = xn.shape
    n = width * len(col_starts)
    bm = min(bm, m)
    per = width // bn
    first, second = (c // bn for c in col_starts)

    def w_block(i, j):
        return 0, jnp.where(j < per, first + j, second + j - per)

    return pl.pallas_call(
        _proj_rope_hm_kernel,
        grid=(m // bm, n // bn),
        in_specs=[pl.BlockSpec((bm, k), lambda i, j: (i, 0)),
                  pl.BlockSpec((k, bn), w_block),
                  pl.BlockSpec((bm, HEAD_DIM), lambda i, j: (i, 0)),
                  pl.BlockSpec((bm, HEAD_DIM), lambda i, j: (i, 0))],
        out_specs=pl.BlockSpec((bm // Q_BLOCK, bn // HEAD_DIM, Q_BLOCK, HEAD_DIM),
                               lambda i, j: (i, j, 0, 0)),
        out_shape=jax.ShapeDtypeStruct((m // Q_BLOCK, n // HEAD_DIM, Q_BLOCK, HEAD_DIM), BF16),
        compiler_params=_cparams(("arbitrary", "arbitrary"), 52),
        name="proj_q",
    )(xn, w, cos, sin)


def _proj_kv_kernel(x_ref, wk_ref, wki_ref, wv_ref, cos_ref, sin_ref, o_ref):
    x = x_ref[...]
    cos = cos_ref[...]
    sin = sin_ref[...]
    col = 0
    for w_ref, rotary in ((wk_ref, True), (wki_ref, True), (wv_ref, False)):
        acc = _dot(x, w_ref[...])
        for j in range(acc.shape[1] // HEAD_DIM):
            t = acc[:, j * HEAD_DIM:(j + 1) * HEAD_DIM]
            if rotary:
                t = _rope(t, cos, sin)
            o_ref[:, col:col + HEAD_DIM] = t.astype(o_ref.dtype)
            col += HEAD_DIM


def _proj_kv(xn, w, col_k, col_ki, col_v, cos, sin, bm=1024):
    m, k = xn.shape
    kvd = N_KV_A * HEAD_DIM
    n = 2 * kvd + HEAD_DIM
    bm = min(bm, m)
    return pl.pallas_call(
        _proj_kv_kernel,
        grid=(m // bm,),
        in_specs=[pl.BlockSpec((bm, k), lambda i: (i, 0)),
                  pl.BlockSpec((k, kvd), lambda i: (0, col_k // kvd)),
                  pl.BlockSpec((k, HEAD_DIM), lambda i: (0, col_ki // HEAD_DIM)),
                  pl.BlockSpec((k, kvd), lambda i: (0, col_v // kvd)),
                  pl.BlockSpec((bm, HEAD_DIM), lambda i: (i, 0)),
                  pl.BlockSpec((bm, HEAD_DIM), lambda i: (i, 0))],
        out_specs=pl.BlockSpec((bm, n), lambda i: (i, 0)),
        out_shape=jax.ShapeDtypeStruct((m, n), BF16),
        compiler_params=_cparams(("arbitrary",), 52),
        name="proj_kv",
    )(xn, w, w, w, cos, sin)


def _mm_kernel(x_ref, w_ref, o_ref):
    o_ref[...] = _dot(x_ref[...], w_ref[...]).astype(o_ref.dtype)


def _matmul(x, w, out_dtype, bm=1024, bn=1024, name="matmul"):
    m, k = x.shape
    n = w.shape[1]
    bm = min(bm, m)
    return pl.pallas_call(
        _mm_kernel,
        grid=(m // bm, n // bn),
        in_specs=[pl.BlockSpec((bm, k), lambda i, j: (i, 0)),
                  pl.BlockSpec((k, bn), lambda i, j: (0, j))],
        out_specs=pl.BlockSpec((bm, bn), lambda i, j: (i, j)),
        out_shape=jax.ShapeDtypeStruct((m, n), out_dtype),
        compiler_params=_cparams(("arbitrary", "arbitrary"), 52),
        name=name,
    )(x, w)


WORD_BITS = 32
SUM_ROWS = 16
KEYS_PER_WORD_GROUP = WORD_BITS * SUBLANES


def _bit_transpose32(words):
    a = list(words)
    j, m = 16, 0x0000FFFF
    while j:
        mask = jnp.int32(m - (1 << 32) if m >= (1 << 31) else m)
        k = 0
        while k < WORD_BITS:
            t = (a[k] ^ lax.shift_right_logical(a[k + j], jnp.full_like(a[k], j))) & mask
            a[k] = a[k] ^ t
            a[k + j] = a[k + j] ^ (t << j)
            k = (k + j + 1) & ~j
        j >>= 1
        m = (m ^ (m << j)) & 0xFFFFFFFF
    return a


def _popcount_rows(words):
    per_sublane = jnp.sum(lax.population_count(words).reshape(-1, SUBLANES, words.shape[1]), axis=0)
    return jnp.sum(per_sublane.astype(F32), axis=0, keepdims=True)


def _attn_kernel(q_ref, qi_ref, wi_ref, k_ref, ki_ref, vt_ref, o_ref,
                 plane_ref, sel_ref, bias_ref, m_ref, acc_ref, alpha_ref, s_ref, p_ref,
                 *, topk, tk, idx_bits):
    i = pl.program_id(1)
    nk = (i + 1) * Q_BLOCK
    nch = (nk + tk - 1) // tk
    n_words = plane_ref.shape[1]
    groups_per_chunk = tk // KEYS_PER_WORD_GROUP

    @pl.when(i == 0)
    def _():
        plane_ref[...] = jnp.zeros_like(plane_ref)

    qi = qi_ref[0].reshape(N_HEADS_IDX * Q_BLOCK, HEAD_DIM)
    wi_t = jnp.transpose(wi_ref[...]) * (HEAD_DIM ** -0.5)
    qpos = i * Q_BLOCK + lax.broadcasted_iota(I32, (tk, Q_BLOCK), 1)
    krow = lax.broadcasted_iota(I32, (tk, Q_BLOCK), 0)

    def score_body(c, carry):
        off = pl.multiple_of(c * tk, tk)
        lg = _dot_nt(ki_ref[0, pl.ds(off, tk), :], qi)
        sc = jnp.zeros((tk, Q_BLOCK), F32)
        for h in range(N_HEADS_IDX):
            sc = sc + wi_t[h:h + 1, :] * jnp.maximum(lg[:, h * Q_BLOCK:(h + 1) * Q_BLOCK], 0.0)
        bits = pltpu.bitcast(sc, I32)
        key = bits ^ ((bits >> 31) & 0x7FFFFFFF)
        key = jnp.where(key == -1, 0, key)
        ukey = jnp.where(off + krow <= qpos, key ^ INT_MIN, 0)
        for gi in range(groups_per_chunk):
            base = gi * KEYS_PER_WORD_GROUP
            planes = _bit_transpose32(
                [ukey[base + t * SUBLANES:base + (t + 1) * SUBLANES, :] for t in range(WORD_BITS)])
            row0 = pl.multiple_of((c * groups_per_chunk + gi) * SUBLANES, SUBLANES)
            for b in range(WORD_BITS):
                plane_ref[b, pl.ds(row0, SUBLANES), :] = planes[WORD_BITS - 1 - b]
        return carry

    lax.fori_loop(0, nch, score_body, 0)

    cand = jnp.full((n_words, Q_BLOCK), -1, I32)
    greater = jnp.zeros((n_words, Q_BLOCK), I32)
    cnt_gt = jnp.zeros((1, Q_BLOCK), F32)
    for b in range(WORD_BITS - 1, -1, -1):
        ones = cand & plane_ref[b]
        cnt = _popcount_rows(ones)
        take = cnt_gt + cnt >= topk
        greater = jnp.where(take, greater, greater | ones)
        cnt_gt = jnp.where(take, cnt_gt, cnt_gt + cnt)
        cand = jnp.where(take, ones, cand ^ ones)

    word_row = lax.broadcasted_iota(I32, (n_words, Q_BLOCK), 0)
    word_pos = (word_row >> 3) * KEYS_PER_WORD_GROUP + (word_row & (SUBLANES - 1))
    qcol = i * Q_BLOCK + lax.broadcasted_iota(I32, (n_words, Q_BLOCK), 1)

    def prefix(limit):
        nt = jnp.clip((limit - word_pos + (SUBLANES - 1)) >> 3, 0, WORD_BITS)
        top = lax.shift_right_arithmetic(jnp.full_like(nt, INT_MIN), jnp.maximum(nt, 1) - 1)
        return jnp.where(nt <= 0, 0, top)

    cand = cand & prefix(qcol + 1)
    need = topk - cnt_gt
    sel_ref[...] = greater | cand

    @pl.when(jnp.max(_popcount_rows(cand) - need) > 0)
    def _():
        def jbody(bi, jt):
            cj = jt | jnp.left_shift(jnp.int32(1), idx_bits - 1 - bi)
            return jnp.where(_popcount_rows(cand & prefix(cj)) <= need, cj, jt)
        jt = lax.fori_loop(0, idx_bits, jbody, jnp.zeros((1, Q_BLOCK), I32))
        sel_ref[...] = greater | (cand & prefix(jt))

    def bias_body(c, carry):
        off = pl.multiple_of(c * tk, tk)
        for gi in range(groups_per_chunk):
            row0 = pl.multiple_of((c * groups_per_chunk + gi) * SUBLANES, SUBLANES)
            w = sel_ref[pl.ds(row0, SUBLANES), :]
            for t in range(WORD_BITS):
                dst = pl.multiple_of(off + gi * KEYS_PER_WORD_GROUP + t * SUBLANES, SUBLANES)
                bias_ref[pl.ds(dst, SUBLANES), :] = jnp.where((w << t) < 0, 0.0, -jnp.inf)
        return carry

    lax.fori_loop(0, nch, bias_body, 0)

    n_rep = N_HEADS_A // N_KV_A
    cols = n_rep * Q_BLOCK
    q_all = q_ref[0].reshape(N_HEADS_A * Q_BLOCK, HEAD_DIM)
    scale2 = (HEAD_DIM ** -0.5) * LOG2_E
    m_ref[...] = jnp.full(m_ref.shape, -1e30, F32)
    acc_ref[...] = jnp.zeros(acc_ref.shape, F32)
    ones_rows = jnp.ones((SUM_ROWS, tk), BF16)

    def stage_qk(g, off):
        qg = q_all[g * cols:(g + 1) * cols, :]
        s_ref[g] = _dot_nt(k_ref[0, pl.ds(off, tk), g * HEAD_DIM:(g + 1) * HEAD_DIM], qg)

    def stage_max(g, off):
        for n in range(n_rep):
            csl = slice(n * Q_BLOCK, (n + 1) * Q_BLOCK)
            t = s_ref[g, :, csl] + bias_ref[pl.ds(off, tk), :]
            s_ref[g, :, csl] = t
            m_old = m_ref[g, :, csl]
            m_new = jnp.maximum(m_old, jnp.max(t, axis=0, keepdims=True))
            alpha_ref[g, :, csl] = jnp.exp2((m_old - m_new) * scale2)
            m_ref[g, :, csl] = m_new

    def stage_exp(g, off):
        for n in range(n_rep):
            csl = slice(n * Q_BLOCK, (n + 1) * Q_BLOCK)
            p_ref[g, :, csl] = jnp.exp2((s_ref[g, :, csl] - m_ref[g, 0:1, csl]) * scale2).astype(BF16)

    def stage_pv(g, off):
        v_ext = jnp.concatenate(
            [vt_ref[0, g * HEAD_DIM:(g + 1) * HEAD_DIM, pl.ds(off, tk)], ones_rows], axis=0)
        acc_ref[g] = alpha_ref[g, 0:1, :] * acc_ref[g] + _dot(v_ext, p_ref[g])

    stages = (stage_qk, stage_max, stage_exp, stage_pv)

    def att_body(c, carry):
        off = pl.multiple_of(c * tk, tk)
        for step in range(N_KV_A + len(stages) - 1):
            for g in range(N_KV_A):
                if 0 <= step - g < len(stages):
                    stages[step - g](g, off)
        return carry

    lax.fori_loop(0, nch, att_body, 0)
    for g in range(N_KV_A):
        acc = acc_ref[g]
        o_t = acc[0:HEAD_DIM, :] / acc[HEAD_DIM:HEAD_DIM + 1, :]
        for n in range(n_rep):
            h = g * n_rep + n
            o_ref[:, h * HEAD_DIM:(h + 1) * HEAD_DIM] = jnp.transpose(
                o_t[:, n * Q_BLOCK:(n + 1) * Q_BLOCK]).astype(o_ref.dtype)


def _attention(qh, wi, kkiv, vt, batch, seq_len):
    nb = seq_len // Q_BLOCK
    topk = min(TOPK_MAX, seq_len // 4)
    tk = min(512, seq_len)
    kvd = N_KV_A * HEAD_DIM
    cols = N_HEADS_A // N_KV_A * Q_BLOCK
    single = pl.Buffered(1)
    return pl.pallas_call(
        functools.partial(_attn_kernel, topk=topk, tk=tk, idx_bits=seq_len.bit_length()),
        grid=(batch, nb),
        in_specs=[pl.BlockSpec((1, N_HEADS_A, Q_BLOCK, HEAD_DIM), lambda b, i: (b * nb + i, 0, 0, 0)),
                  pl.BlockSpec((1, N_HEADS_IDX, Q_BLOCK, HEAD_DIM), lambda b, i: (b * nb + i, 1, 0, 0)),
                  pl.BlockSpec((Q_BLOCK, LANES), lambda b, i: (b * nb + i, 0)),
                  pl.BlockSpec((1, seq_len, kvd), lambda b, i: (b, 0, 0), pipeline_mode=single),
                  pl.BlockSpec((1, seq_len, HEAD_DIM), lambda b, i: (b, 0, kvd // HEAD_DIM),
                               pipeline_mode=single),
                  pl.BlockSpec((1, kvd, seq_len), lambda b, i: (b, 0, 0), pipeline_mode=single)],
        out_specs=pl.BlockSpec((Q_BLOCK, N_HEADS_A * HEAD_DIM), lambda b, i: (b * nb + i, 0)),
        out_shape=jax.ShapeDtypeStruct((batch * seq_len, N_HEADS_A * HEAD_DIM), BF16),
        scratch_shapes=[pltpu.VMEM((WORD_BITS, seq_len // WORD_BITS, Q_BLOCK), I32),
                        pltpu.VMEM((seq_len // WORD_BITS, Q_BLOCK), I32),
                        pltpu.VMEM((seq_len, Q_BLOCK), F32),
                        pltpu.VMEM((N_KV_A, SUBLANES, cols), F32),
                        pltpu.VMEM((N_KV_A, HEAD_DIM + SUM_ROWS, cols), F32),
                        pltpu.VMEM((N_KV_A, SUBLANES, cols), F32),
                        pltpu.VMEM((N_KV_A, tk, cols), F32),
                        pltpu.VMEM((N_KV_A, tk, cols), BF16)],
        compiler_params=_cparams(("arbitrary", "arbitrary"), 52),
        name="dsa_attention",
    )(qh, qh, wi, kkiv, kkiv, vt)


def _split2(x):
    hi = x.astype(BF16)
    lo = (x - hi.astype(F32)).astype(BF16)
    return hi, lo


def _dot_hp(x, w_bf):
    hi = x.astype(BF16)
    r1 = x - hi.astype(F32)
    mid = r1.astype(BF16)
    lo = (r1 - mid.astype(F32)).astype(BF16)
    return _dot(hi, w_bf) + _dot(mid, w_bf) + _dot(lo, w_bf)


def _rwkv_kernel(r_ref, k_ref, v_ref, lw_ref, a_ref, g_ref, mu_ref, kk_ref, ka_ref, rk_ref,
                 lnw_ref, lnb_ref, bd_ref, ones_ref, ltri_ref, ts_ref, ti_ref, ic_ref, eye_ref,
                 o_ref, st_ref, prev_ref):
    @pl.when(pl.program_id(2) == 0)
    def _():
        st_ref[...] = jnp.zeros_like(st_ref)
        prev_ref[...] = jnp.zeros_like(prev_ref)

    tiles = [_rwkv_tile(tile, r_ref, k_ref, v_ref, lw_ref, a_ref, g_ref, mu_ref, kk_ref, ka_ref, rk_ref,
                        lnw_ref, lnb_ref, bd_ref, ones_ref, ltri_ref, ts_ref, ti_ref, ic_ref, eye_ref,
                        o_ref, st_ref, prev_ref)
             for tile in range(r_ref.shape[1] // TILE_B)]
    while tiles:
        tiles = [t for t in tiles if next(t, "done") != "done"]


def _rwkv_tile(tile, r_ref, k_ref, v_ref, lw_ref, a_ref, g_ref, mu_ref, kk_ref, ka_ref, rk_ref,
               lnw_ref, lnb_ref, bd_ref, ones_ref, ltri_ref, ts_ref, ti_ref, ic_ref, eye_ref,
               o_ref, st_ref, prev_ref):
    cols = slice(tile * TILE_B, (tile + 1) * TILE_B)
    bdm = bd_ref[...]
    row = lax.broadcasted_iota(I32, (CHUNK, TILE_B), 0)

    def shifted(x, slot):
        prev = prev_ref[slot:slot + 1, cols]
        prev_ref[slot:slot + 1, cols] = x[CHUNK - 1:CHUNK, :]
        return jnp.where(row == 0, prev, pltpu.roll(x, 1, axis=0))

    r0 = r_ref[:, cols]
    k0 = k_ref[:, cols]
    v0 = v_ref[:, cols]
    r = r0 + (shifted(r0, 0) - r0) * mu_ref[0:1, cols]
    k = k0 + (shifted(k0, 1) - k0) * mu_ref[1:2, cols]
    v = v0 + (shifted(v0, 2) - v0) * mu_ref[2:3, cols]
    a = a_ref[:, cols]
    ones_bd = ones_ref[...]

    kk = k * kk_ref[:, cols]
    ss = _dot_hp(kk * kk, ones_bd)
    yield
    kk = kk / jnp.maximum(jnp.sqrt(ss), 1e-12)
    k2 = k * (1.0 + (a - 1.0) * ka_ref[:, cols])
    aa = -kk
    bb = kk * a

    lw = lw_ref[:, cols]
    cs = _cumsum_rows(lw, ltri_ref[...])
    yield
    tot = cs[CHUNK - 1:CHUNK, :]
    e_in = jnp.exp(cs)
    e_out = jnp.exp(-cs)
    e_tail = jnp.exp(tot - cs)
    at = aa * jnp.exp(cs - lw)
    rt = r * e_in
    bt = (bb * e_out)
    kt = (k2 * e_out)
    bh = (bb * e_tail).astype(BF16)
    kh = (k2 * e_tail).astype(BF16)
    wc = jnp.exp(tot)

    def bd(x):
        return jnp.concatenate([x.astype(BF16)] * HEADS_PER_TILE, axis=0) * ones_bd

    lhs = jnp.concatenate([at, rt], axis=0).astype(BF16)
    mb = _dot_nt(lhs, bd(bt))
    mk = _dot_nt(lhs, bd(kt))
    yield
    ts = ts_ref[...]
    ti = ti_ref[...]
    m_ab = mb[0:CHUNK] * ts
    m_rb = (mb[CHUNK:2 * CHUNK] * ti).astype(BF16)
    m_ak = (mk[0:CHUNK] * ts).astype(BF16)
    m_rk = (mk[CHUNK:2 * CHUNK] * ti).astype(BF16)

    m_pow = m_ab
    t_inv = ic_ref[...] + m_ab
    n_round = CHUNK.bit_length() - 1
    for _ in range(1, n_round):
        m_pow_bf = m_pow.astype(BF16)
        m_pow = _dot(m_pow_bf, bd(m_pow))
        yield
        t_inv = t_inv + _dot(m_pow.astype(BF16), bd(t_inv))
        yield
    t_bf = t_inv.astype(BF16)

    vbd = bd(v)
    p = _dot(t_bf, bd(at))
    u = _dot(m_ak, vbd)
    yield
    q = _dot(t_bf, bd(u))
    rp = rt + _dot(m_rb, bd(p))
    yield
    y0 = _dot(m_rb, bd(q)) + _dot(m_rk, vbd)

    st = st_ref[tile]
    st_bf = st.astype(BF16)
    y = _dot(rp.astype(BF16), st_bf) + y0

    yield
    a_t = _dot_tn(bh, p.astype(BF16)) * bdm + eye_ref[...] * wc
    d_t = _dot_tn(jnp.concatenate([bh, kh], axis=0),
                  jnp.concatenate([q.astype(BF16), v.astype(BF16)], axis=0)) * bdm
    st_ref[tile] = _dot(a_t.astype(BF16), st_bf) + d_t

    yield
    inv_n = 1.0 / HEAD_DIM_B
    mu = _dot_hp(y, ones_bd) * inv_n
    yield
    yc = y - mu
    var = _dot_hp(yc * yc, ones_bd) * inv_n
    yield
    yn =(yc * lax.rsqrt(var + GN_EPS)) * lnw_ref[:, cols] + lnb_ref[:, cols]
    bonus = _dot_hp(r * k2 * rk_ref[:, cols], ones_bd) * v
    o_ref[:, cols] = ((yn + bonus) * g_ref[:, cols]).astype(o_ref.dtype)


def _cumsum_rows(x, ltri_bf):
    hi = x.astype(BF16)
    r1 = x - hi.astype(F32)
    mid = r1.astype(BF16)
    lo = (r1 - mid.astype(F32)).astype(BF16)
    return _dot(ltri_bf, hi) + _dot(ltri_bf, mid) + _dot(ltri_bf, lo)


def _rwkv_consts():
    idx = jnp.arange(TILE_B)
    head = idx // HEAD_DIM_B
    bdm = (head[:, None] == head[None, :])
    t = jnp.arange(CHUNK)[:, None]
    s = (idx % CHUNK)[None, :]
    return dict(
        bd=bdm.astype(F32),
        ones=bdm.astype(BF16),
        ltri=(jnp.arange(CHUNK)[None, :] <= t).astype(BF16),
        ts=(s < t).astype(F32),
        ti=(s <= t).astype(F32),
        ic=(s == t).astype(F32),
        eye=jnp.eye(TILE_B, dtype=F32),
    )


def _rwkv(rkv, lw, a, g, mu_rkv, k_k, k_a, r_k, ln_w, ln_b, batch, seq_len):
    m, db = lw.shape
    nc = seq_len // CHUNK
    width = RWKV_TILES_PER_STEP * TILE_B
    ng = db // width
    cst = _rwkv_consts()
    blk = lambda off: pl.BlockSpec((CHUNK, width), lambda b, h, c, off=off: (b * nc + c, off + h))
    par = lambda rows: pl.BlockSpec((rows, width), lambda b, h, c: (0, h))
    full = lambda arr: pl.BlockSpec(arr.shape, lambda b, h, c: (0, 0))
    row = lambda p: p.reshape(1, db)
    return pl.pallas_call(
        _rwkv_kernel,
        grid=(batch, ng, nc),
        in_specs=[blk(0), blk(ng), blk(2 * ng), blk(0), blk(0), blk(0),
                  par(3), par(1), par(1), par(1), par(1), par(1),
                  full(cst["bd"]), full(cst["ones"]), full(cst["ltri"]), full(cst["ts"]),
                  full(cst["ti"]), full(cst["ic"]), full(cst["eye"])],
        out_specs=blk(0),
        out_shape=jax.ShapeDtypeStruct((m, db), BF16),
        scratch_shapes=[pltpu.VMEM((RWKV_TILES_PER_STEP, TILE_B, TILE_B), F32),
                        pltpu.VMEM((SUBLANES, width), F32)],
        compiler_params=_cparams(("arbitrary", "arbitrary", "arbitrary"), 32),
        name="rwkv7",
    )(rkv, rkv, rkv, lw, a, g, mu_rkv, row(k_k), row(k_a), row(r_k), row(ln_w), row(ln_b),
      cst["bd"], cst["ones"], cst["ltri"], cst["ts"], cst["ti"], cst["ic"], cst["eye"])


def _merge_kernel(xn_ref, oa_ref, ob_ref, wga_ref, wgb_ref, bga_ref, bgb_ref, wpa_ref, wpb_ref, o_ref):
    xn = xn_ref[...]
    ga = _sigmoid(_dot(xn, wga_ref[...]) + bga_ref[...])
    gb = _sigmoid(_dot(xn, wgb_ref[...]) + bgb_ref[...])
    ya = _dot(oa_ref[...], wpa_ref[...])
    yb = _dot(ob_ref[...], wpb_ref[...])
    o_ref[...] = (ga * ya + gb * yb).astype(o_ref.dtype)


def _merge(xn, oa, ob, w_gate, b_gate, w_pa, w_pb, bm=1024, bn=256):
    m, d = xn.shape
    bm = min(bm, m)
    nj = d // bn
    return pl.pallas_call(
        _merge_kernel,
        grid=(m // bm, nj),
        in_specs=[pl.BlockSpec((bm, d), lambda i, j: (i, 0)),
                  pl.BlockSpec((bm, oa.shape[1]), lambda i, j: (i, 0)),
                  pl.BlockSpec((bm, ob.shape[1]), lambda i, j: (i, 0)),
                  pl.BlockSpec((d, bn), lambda i, j: (0, j)),
                  pl.BlockSpec((d, bn), lambda i, j: (0, j + nj)),
                  pl.BlockSpec((1, bn), lambda i, j: (0, j)),
                  pl.BlockSpec((1, bn), lambda i, j: (0, j + nj)),
                  pl.BlockSpec((w_pa.shape[0], bn), lambda i, j: (0, j)),
                  pl.BlockSpec((w_pb.shape[0], bn), lambda i, j: (0, j))],
        out_specs=pl.BlockSpec((bm, bn), lambda i, j: (i, j)),
        out_shape=jax.ShapeDtypeStruct((m, d), BF16),
        compiler_params=_cparams(("arbitrary", "arbitrary"), 52),
        name="gated_merge",
    )(xn, oa, ob, w_gate, w_gate, b_gate, b_gate, w_pa, w_pb)


def _mm_res_kernel(x_ref, w_ref, res_ref, o_ref):
    o_ref[...] = res_ref[...] + _dot(x_ref[...], w_ref[...])


def _matmul_residual(x, w, res, bm=1024, bn=1024, name="matmul_res"):
    m, k = x.shape
    n = w.shape[1]
    bm = min(bm, m)
    return pl.pallas_call(
        _mm_res_kernel,
        grid=(m // bm, n // bn),
        in_specs=[pl.BlockSpec((bm, k), lambda i, j: (i, 0)),
                  pl.BlockSpec((k, bn), lambda i, j: (0, j)),
                  pl.BlockSpec((bm, bn), lambda i, j: (i, j))],
        out_specs=pl.BlockSpec((bm, bn), lambda i, j: (i, j)),
        out_shape=jax.ShapeDtypeStruct((m, n), F32),
        compiler_params=_cparams(("arbitrary", "arbitrary"), 52),
        name=name,
    )(x, w, res)


def _ffn_up_kernel(x_ref, w1_ref, w3_ref, o_ref):
    x = x_ref[...]
    h1 = _dot(x, w1_ref[...])
    h3 = _dot(x, w3_ref[...])
    o_ref[...] = (h1 * _sigmoid(h1) * h3).astype(o_ref.dtype)


def _ffn_up(x, w1, w3, bm=1024, bn=512):
    m, k = x.shape
    n = w1.shape[1]
    bm = min(bm, m)
    return pl.pallas_call(
        _ffn_up_kernel,
        grid=(m // bm, pl.cdiv(n, bn)),
        in_specs=[pl.BlockSpec((bm, k), lambda i, j: (i, 0)),
                  pl.BlockSpec((k, bn), lambda i, j: (0, j)),
                  pl.BlockSpec((k, bn), lambda i, j: (0, j))],
        out_specs=pl.BlockSpec((bm, bn), lambda i, j: (i, j)),
        out_shape=jax.ShapeDtypeStruct((m, n), BF16),
        compiler_params=_cparams(("arbitrary", "arbitrary"), 52),
        name="ffn_up",
    )(x, w1, w3)


def _ple_kernel(hn_ref, wg_ref, p_ref, wp_ref, res_ref, o_ref):
    gate = _sigmoid(_dot(hn_ref[...], wg_ref[...]))
    o_ref[...] = res_ref[...] + gate * _dot(p_ref[...], wp_ref[...])


def _ple(hn, wg, p, wp, res, bm=1024, bn=512):
    m, k = hn.shape
    n = wg.shape[1]
    bm = min(bm, m)
    return pl.pallas_call(
        _ple_kernel,
        grid=(m // bm, n // bn),
        in_specs=[pl.BlockSpec((bm, k), lambda i, j: (i, 0)),
                  pl.BlockSpec((k, bn), lambda i, j: (0, j)),
                  pl.BlockSpec((bm, p.shape[1]), lambda i, j: (i, 0)),
                  pl.BlockSpec((p.shape[1], bn), lambda i, j: (0, j)),
                  pl.BlockSpec((bm, bn), lambda i, j: (i, j))],
        out_specs=pl.BlockSpec((bm, bn), lambda i, j: (i, j)),
        out_shape=jax.ShapeDtypeStruct((m, n), F32),
        compiler_params=_cparams(("arbitrary", "arbitrary"), 52),
        name="ple",
    )(hn, wg, p, wp, res)


def _rmsnorm_kernel(x_ref, g_ref, o_ref):
    o_ref[...] = (_rms_rows(x_ref[...]) * g_ref[...]).astype(o_ref.dtype)


def _rms_kernel(x_ref, o_ref):
    o_ref[...] = _rms_rows(x_ref[...]).astype(o_ref.dtype)


def _rmsnorm(x, gain, out_dtype, bm=512):
    m, d = x.shape
    bm = min(bm, m)
    row_spec = pl.BlockSpec((bm, d), lambda i: (i, 0))
    if gain is None:
        kern, args, specs = _rms_kernel, (x,), [row_spec]
    else:
        kern, args = _rmsnorm_kernel, (x, gain.reshape(1, d))
        specs = [row_spec, pl.BlockSpec((1, d), lambda i: (0, 0))]
    return pl.pallas_call(
        kern,
        grid=(m // bm,),
        in_specs=specs,
        out_specs=row_spec,
        out_shape=jax.ShapeDtypeStruct((m, d), out_dtype),
        compiler_params=_cparams(("arbitrary",), 48),
        name="rmsnorm",
    )(*args)


def _pad_to(a, axis, size):
    pad = [(0, 0)] * a.ndim
    pad[axis] = (0, size - a.shape[axis])
    return jnp.pad(a, pad)


def _layer(h2, p2, cos, sin, batch, seq_len, norm_mix, w_in, mu_rkv, mu_wag, w0, w1, w2, a0, a1, a2,
           g1, g2, k_k, k_a, r_k, ln_w, ln_b, w_pa, w_pb, w_gate, b_gate, w_o, norm_ffn,
           w_ffn1, w_ffn3, w_ffn2, w_ple_gate, w_ple):
    d = h2.shape[1]
    d_a = N_HEADS_A * HEAD_DIM
    kvd = N_KV_A * HEAD_DIM
    d_iq = N_HEADS_IDX * HEAD_DIM
    d_b = w_pb.shape[0]
    o_q, o_k, o_v = 0, d_a, d_a + kvd
    o_qi = o_v + kvd
    o_ki = o_qi + d_iq
    o_wi = o_ki + HEAD_DIM
    o_r = o_wi + N_HEADS_IDX

    bf = lambda a: a.astype(BF16)
    w_in_bf = bf(w_in)
    assert d_a == d_iq and o_wi % LANES == 0
    w_rkv = w_in_bf[:, o_r:o_r + 3 * d_b]
    lora = LANES
    w1p, a1p = bf(_pad_to(w1, 1, lora)), bf(_pad_to(a1, 1, lora))
    w2p, a2p = bf(_pad_to(w2, 0, lora)), bf(_pad_to(a2, 0, lora))

    xn, lw, a, g, wi = _prep(h2, norm_mix.reshape(1, d), mu_wag, w1p, w2p, w0.reshape(1, d_b),
                             a1p, a2p, a0.reshape(1, d_b), bf(g1), bf(g2), w_in_bf, o_wi // LANES,
                             seq_len)

    qh = _proj_rope_headmajor(xn, w_in_bf, (o_q, o_qi), d_a, cos, sin)
    kkiv = _proj_kv(xn, w_in_bf, o_k, o_ki, o_v, cos, sin)
    rkv = _matmul(xn, w_rkv, F32, name="proj_rkv")

    kkiv = kkiv.reshape(batch, seq_len, -1)
    vt = jnp.swapaxes(kkiv[:, :, kvd + HEAD_DIM:], 1, 2)
    o_att = _attention(qh, wi, kkiv, vt, batch, seq_len)

    o_rwkv = _rwkv(rkv, lw, a, g, mu_rkv, k_k, k_a, r_k.reshape(-1), ln_w, ln_b, batch, seq_len)

    mixed = _merge(xn, o_att, o_rwkv, bf(w_gate), b_gate.reshape(1, -1), bf(w_pa), bf(w_pb))
    h2 = _matmul_residual(mixed, bf(w_o), h2, bn=512, name="out_proj")

    xf = _rmsnorm(h2, norm_ffn, BF16)
    u = _ffn_up(xf, bf(w_ffn1), bf(w_ffn3))
    h2 = _matmul_residual(u, bf(w_ffn2), h2, bm=512, bn=512, name="ffn_down")

    hn = _rmsnorm(h2, None, BF16)
    h2 = _ple(hn, bf(w_ple_gate), bf(p2), bf(w_ple), h2)
    return h2


def kernel(x, p, positions, norm_mix, w_in, mu_rkv, mu_wag, w0, w1, w2, a0, a1, a2, g1, g2, k_k, k_a,
           r_k, ln_w, ln_b, w_pa, w_pb, w_gate, b_gate, w_o, norm_ffn, w_ffn1, w_ffn3, w_ffn2,
           w_ple_gate, w_ple, norm_final):
    batch, seq_len, d = x.shape
    depth = p.shape[0]
    h2 = x.reshape(batch * seq_len, d)
    cos, sin = _rope_tables(positions)
    for i in range(depth):
        h2 = _layer(h2, p[i].reshape(batch * seq_len, -1), cos, sin, batch, seq_len,
                    norm_mix[i], w_in[i], mu_rkv[i], mu_wag[i], w0[i], w1[i], w2[i], a0[i], a1[i], a2[i],
                    g1[i], g2[i], k_k[i], k_a[i], r_k[i], ln_w[i], ln_b[i], w_pa[i], w_pb[i], w_gate[i],
                    b_gate[i], w_o[i], norm_ffn[i], w_ffn1[i], w_ffn3[i], w_ffn2[i], w_ple_gate[i],
                    w_ple[i])
    out = _rmsnorm(h2, norm_final, F32)
    return out.reshape(batch, seq_len, d)
```

```python
import functools

import jax
import jax.numpy as jnp
from jax import lax
from jax.experimental import pallas as pl
from jax.experimental.pallas import tpu as pltpu

F32 = jnp.float32
BF16 = jnp.bfloat16
I32 = jnp.int32

N_HEADS_A = 16
HEAD_DIM = 128
N_KV_A = 4
N_HEADS_IDX = 16
TOPK_MAX = 256
Q_BLOCK = 128
ROPE_THETA = 10000.0
HEAD_DIM_B = 64
GN_EPS = 64e-5
RMS_EPS = 1e-6

LANES = 128
SUBLANES = 8
MXU_DIM = 256

INT_MIN = -2 ** 31
LOG2_E = 1.4426950408889634
NORM_ROW_CHUNK = 64
DECAY_SCALE = 0.6065306597126334

CHUNK = 64
HEADS_PER_TILE = MXU_DIM // HEAD_DIM_B
TILE_B = HEADS_PER_TILE * HEAD_DIM_B
RWKV_TILES_PER_STEP = 4


def _cparams(sem, vmem_mib):
    return pltpu.CompilerParams(dimension_semantics=sem, vmem_limit_bytes=vmem_mib << 20)


def _dot(a, b):
    return jnp.dot(a, b, preferred_element_type=F32)


def _dot_nt(a, b):
    return lax.dot_general(a, b, (((1,), (1,)), ((), ())), preferred_element_type=F32)


def _dot_tn(a, b):
    return lax.dot_general(a, b, (((0,), (0,)), ((), ())), preferred_element_type=F32)


def _sigmoid(x):
    return 1.0 / (1.0 + jnp.exp(-x))


def _rms_rows(x):
    return x * lax.rsqrt(jnp.mean(x * x, axis=-1, keepdims=True) + RMS_EPS)


def _rope_tab_kernel(pos_ref, freq_ref, sign_ref, cos_ref, sin_ref):
    ang = pos_ref[...].astype(F32) * freq_ref[...]
    cos_ref[...] = jnp.cos(ang)
    sin_ref[...] = jnp.sin(ang) * sign_ref[...]


def _rope_tables(positions):
    n = positions.size
    half = HEAD_DIM // 2
    inv_freq = ROPE_THETA ** (-jnp.arange(0, HEAD_DIM, 2, dtype=F32) / HEAD_DIM)
    freq2 = jnp.concatenate([inv_freq, inv_freq]).reshape(1, HEAD_DIM)
    sign = jnp.concatenate([-jnp.ones((half,), F32), jnp.ones((half,), F32)]).reshape(1, HEAD_DIM)
    bm = min(2048, n)
    return pl.pallas_call(
        _rope_tab_kernel,
        grid=(n // bm,),
        in_specs=[pl.BlockSpec((bm, 1), lambda i: (i, 0)),
                  pl.BlockSpec((1, HEAD_DIM), lambda i: (0, 0)),
                  pl.BlockSpec((1, HEAD_DIM), lambda i: (0, 0))],
        out_specs=[pl.BlockSpec((bm, HEAD_DIM), lambda i: (i, 0)),
                   pl.BlockSpec((bm, HEAD_DIM), lambda i: (i, 0))],
        out_shape=[jax.ShapeDtypeStruct((n, HEAD_DIM), F32)] * 2,
        compiler_params=_cparams(("arbitrary",), 32),
        name="rope_tables",
    )(positions.reshape(n, 1), freq2, sign)


def _rope(t, cos, sin):
    return t * cos + pltpu.roll(t, HEAD_DIM // 2, axis=1) * sin


def _prep_kernel(x_ref, xp_ref, gain_ref, mu_ref, w1_ref, w2_ref, w0_ref, a1_ref, a2_ref, a0_ref,
                 g1_ref, g2_ref, wwi_ref,
                 xn_ref, lw_ref, a_ref, g_ref, wi_ref, *, seq_len, bm):
    i = pl.program_id(0)
    gain = gain_ref[...]
    xn = _rms_rows(x_ref[...]) * gain
    prev = (_rms_rows(xp_ref[...]) * gain)[SUBLANES - 1:SUBLANES, :]
    prev = jnp.where((i * bm) % seq_len == 0, jnp.zeros_like(prev), prev)
    row = lax.broadcasted_iota(I32, xn.shape, 0)
    sh = jnp.where(row == 0, prev, pltpu.roll(xn, 1, axis=0))
    xx = sh - xn
    xn_bf = xn.astype(BF16)
    xn_ref[...] = xn_bf
    wi_ref[...] = _dot(xn_bf, wwi_ref[...]) * (N_HEADS_IDX ** -0.5)

    xw = (xn + xx * mu_ref[0:1, :]).astype(BF16)
    hw = jnp.tanh(_dot(xw, w1_ref[...])).astype(BF16)
    wl = w0_ref[...] + _dot(hw, w2_ref[...])
    lw_ref[...] = -DECAY_SCALE * _sigmoid(wl)

    xa = (xn + xx * mu_ref[1:2, :]).astype(BF16)
    ha = _dot(xa, a1_ref[...]).astype(BF16)
    a_ref[...] = _sigmoid(a0_ref[...] + _dot(ha, a2_ref[...]))

    xg = (xn + xx * mu_ref[2:3, :]).astype(BF16)
    hg = _sigmoid(_dot(xg, g1_ref[...])).astype(BF16)
    g_ref[...] = _dot(hg, g2_ref[...])


def _prep(x2, gain, mu_wag, w1, w2, w0, a1, a2, a0, g1, g2, w_in, wi_block, seq_len):
    m, d = x2.shape
    db = w2.shape[1]
    bm = min(128, m)
    full = lambda a: pl.BlockSpec(a.shape, lambda i: (0,) * a.ndim)
    nsub = bm // SUBLANES
    return pl.pallas_call(
        functools.partial(_prep_kernel, seq_len=seq_len, bm=bm),
        grid=(m // bm,),
        in_specs=[pl.BlockSpec((bm, d), lambda i: (i, 0)),
                  pl.BlockSpec((SUBLANES, d), lambda i: (jnp.maximum(i * nsub - 1, 0), 0)),
                  full(gain), full(mu_wag), full(w1), full(w2), full(w0), full(a1), full(a2), full(a0),
                  full(g1), full(g2), pl.BlockSpec((d, LANES), lambda i: (0, wi_block))],
        out_specs=[pl.BlockSpec((bm, d), lambda i: (i, 0)),
                   pl.BlockSpec((bm, db), lambda i: (i, 0)),
                   pl.BlockSpec((bm, db), lambda i: (i, 0)),
                   pl.BlockSpec((bm, db), lambda i: (i, 0)),
                   pl.BlockSpec((bm, LANES), lambda i: (i, 0))],
        out_shape=[jax.ShapeDtypeStruct((m, d), BF16),
                   jax.ShapeDtypeStruct((m, db), F32),
                   jax.ShapeDtypeStruct((m, db), F32),
                   jax.ShapeDtypeStruct((m, db), F32),
                   jax.ShapeDtypeStruct((m, LANES), F32)],
        compiler_params=_cparams(("arbitrary",), 48),
        name="prep",
    )(x2, x2, gain, mu_wag, w1, w2, w0, a1, a2, a0, g1, g2, w_in)


def _proj_rope_hm_kernel(x_ref, w_ref, cos_ref, sin_ref, o_ref):
    acc = _dot(x_ref[...], w_ref[...])
    cos = cos_ref[...]
    sin = sin_ref[...]
    bm, bn = acc.shape
    for j in range(bn // HEAD_DIM):
        t = _rope(acc[:, j * HEAD_DIM:(j + 1) * HEAD_DIM], cos, sin).astype(o_ref.dtype)
        for r in range(bm // Q_BLOCK):
            o_ref[r, j] = t[r * Q_BLOCK:(r + 1) * Q_BLOCK, :]


def _proj_rope_headmajor(xn, w, col_starts, width, cos, sin, bm=1024, bn=1024):
    m, k = xn.shape
    n = width * len(col_starts)
    bm = min(bm, m)
    per = width // bn
    first, second = (c // bn for c in col_starts)

    def w_block(i, j):
        return 0, jnp.where(j < per, first + j, second + j - per)

    return pl.pallas_call(
        _proj_rope_hm_kernel,
        grid=(m // bm, n // bn),
        in_specs=[pl.BlockSpec((bm, k), lambda i, j: (i, 0)),
                  pl.BlockSpec((k, bn), w_block),
                  pl.BlockSpec((bm, HEAD_DIM), lambda i, j: (i, 0)),
                  pl.BlockSpec((bm, HEAD_DIM), lambda i, j: (i, 0))],
        out_specs=pl.BlockSpec((bm // Q_BLOCK, bn // HEAD_DIM, Q_BLOCK, HEAD_DIM),
                               lambda i, j: (i, j, 0, 0)),
        out_shape=jax.ShapeDtypeStruct((m // Q_BLOCK, n // HEAD_DIM, Q_BLOCK, HEAD_DIM), BF16),
        compiler_params=_cparams(("arbitrary", "arbitrary"), 52),
        name="proj_q",
    )(xn, w, cos, sin)


def _proj_kv_kernel(x_ref, wk_ref, wki_ref, wv_ref, cos_ref, sin_ref, o_ref):
    x = x_ref[...]
    cos = cos_ref[...]
    sin = sin_ref[...]
    col = 0
    for w_ref, rotary in ((wk_ref, True), (wki_ref, True), (wv_ref, False)):
        acc = _dot(x, w_ref[...])
        for j in range(acc.shape[1] // HEAD_DIM):
            t = acc[:, j * HEAD_DIM:(j + 1) * HEAD_DIM]
            if rotary:
                t = _rope(t, cos, sin)
            o_ref[:, col:col + HEAD_DIM] = t.astype(o_ref.dtype)
            col += HEAD_DIM


def _proj_kv(xn, w, col_k, col_ki, col_v, cos, sin, bm=1024):
    m, k = xn.shape
    kvd = N_KV_A * HEAD_DIM
    n = 2 * kvd + HEAD_DIM
    bm = min(bm, m)
    return pl.pallas_call(
        _proj_kv_kernel,
        grid=(m // bm,),
        in_specs=[pl.BlockSpec((bm, k), lambda i: (i, 0)),
                  pl.BlockSpec((k, kvd), lambda i: (0, col_k // kvd)),
                  pl.BlockSpec((k, HEAD_DIM), lambda i: (0, col_ki // HEAD_DIM)),
                  pl.BlockSpec((k, kvd), lambda i: (0, col_v // kvd)),
                  pl.BlockSpec((bm, HEAD_DIM), lambda i: (i, 0)),
                  pl.BlockSpec((bm, HEAD_DIM), lambda i: (i, 0))],
        out_specs=pl.BlockSpec((bm, n), lambda i: (i, 0)),
        out_shape=jax.ShapeDtypeStruct((m, n), BF16),
        compiler_params=_cparams(("arbitrary",), 52),
        name="proj_kv",
    )(xn, w, w, w, cos, sin)


def _mm_kernel(x_ref, w_ref, o_ref):
    o_ref[...] = _dot(x_ref[...], w_ref[...]).astype(o_ref.dtype)


def _matmul(x, w, out_dtype, bm=1024, bn=1024, name="matmul"):
    m, k = x.shape
    n = w.shape[1]
    bm = min(bm, m)
    return pl.pallas_call(
        _mm_kernel,
        grid=(m // bm, n // bn),
        in_specs=[pl.BlockSpec((bm, k), lambda i, j: (i, 0)),
                  pl.BlockSpec((k, bn), lambda i, j: (0, j))],
        out_specs=pl.BlockSpec((bm, bn), lambda i, j: (i, j)),
        out_shape=jax.ShapeDtypeStruct((m, n), out_dtype),
        compiler_params=_cparams(("arbitrary", "arbitrary"), 52),
        name=name,
    )(x, w)


WORD_BITS = 32
SUM_ROWS = 16
KEYS_PER_WORD_GROUP = WORD_BITS * SUBLANES


def _bit_transpose32(words):
    a = list(words)
    j, m = 16, 0x0000FFFF
    while j:
        mask = jnp.int32(m - (1 << 32) if m >= (1 << 31) else m)
        k = 0
        while k < WORD_BITS:
            t = (a[k] ^ lax.shift_right_logical(a[k + j], jnp.full_like(a[k], j))) & mask
            a[k] = a[k] ^ t
            a[k + j] = a[k + j] ^ (t << j)
            k = (k + j + 1) & ~j
        j >>= 1
        m = (m ^ (m << j)) & 0xFFFFFFFF
    return a


def _popcount_rows(words):
    per_sublane = jnp.sum(lax.population_count(words).reshape(-1, SUBLANES, words.shape[1]), axis=0)
    return jnp.sum(per_sublane.astype(F32), axis=0, keepdims=True)


def _attn_kernel(q_ref, qi_ref, wi_ref, k_ref, ki_ref, vt_ref, o_ref,
                 plane_ref, sel_ref, bias_ref, m_ref, acc_ref, alpha_ref, s_ref, p_ref,
                 *, topk, tk, idx_bits):
    i = pl.program_id(1)
    nk = (i + 1) * Q_BLOCK
    nch = (nk + tk - 1) // tk
    n_words = plane_ref.shape[1]
    groups_per_chunk = tk // KEYS_PER_WORD_GROUP

    @pl.when(i == 0)
    def _():
        plane_ref[...] = jnp.zeros_like(plane_ref)

    qi = qi_ref[0].reshape(N_HEADS_IDX * Q_BLOCK, HEAD_DIM)
    wi_t = jnp.transpose(wi_ref[...]) * (HEAD_DIM ** -0.5)
    qpos = i * Q_BLOCK + lax.broadcasted_iota(I32, (tk, Q_BLOCK), 1)
    krow = lax.broadcasted_iota(I32, (tk, Q_BLOCK), 0)

    def score_body(c, carry):
        off = pl.multiple_of(c * tk, tk)
        lg = _dot_nt(ki_ref[0, pl.ds(off, tk), :], qi)
        sc = jnp.zeros((tk, Q_BLOCK), F32)
        for h in range(N_HEADS_IDX):
            sc = sc + wi_t[h:h + 1, :] * jnp.maximum(lg[:, h * Q_BLOCK:(h + 1) * Q_BLOCK], 0.0)
        bits = pltpu.bitcast(sc, I32)
        key = bits ^ ((bits >> 31) & 0x7FFFFFFF)
        key = jnp.where(key == -1, 0, key)
        ukey = jnp.where(off + krow <= qpos, key ^ INT_MIN, 0)
        for gi in range(groups_per_chunk):
            base = gi * KEYS_PER_WORD_GROUP
            planes = _bit_transpose32(
                [ukey[base + t * SUBLANES:base + (t + 1) * SUBLANES, :] for t in range(WORD_BITS)])
            row0 = pl.multiple_of((c * groups_per_chunk + gi) * SUBLANES, SUBLANES)
            for b in range(WORD_BITS):
                plane_ref[b, pl.ds(row0, SUBLANES), :] = planes[WORD_BITS - 1 - b]
        return carry

    lax.fori_loop(0, nch, score_body, 0)

    cand = jnp.full((n_words, Q_BLOCK), -1, I32)
    greater = jnp.zeros((n_words, Q_BLOCK), I32)
    cnt_gt = jnp.zeros((1, Q_BLOCK), F32)
    for b in range(WORD_BITS - 1, -1, -1):
        ones = cand & plane_ref[b]
        cnt = _popcount_rows(ones)
        take = cnt_gt + cnt >= topk
        greater = jnp.where(take, greater, greater | ones)
        cnt_gt = jnp.where(take, cnt_gt, cnt_gt + cnt)
        cand = jnp.where(take, ones, cand ^ ones)

    word_row = lax.broadcasted_iota(I32, (n_words, Q_BLOCK), 0)
    word_pos = (word_row >> 3) * KEYS_PER_WORD_GROUP + (word_row & (SUBLANES - 1))
    qcol = i * Q_BLOCK + lax.broadcasted_iota(I32, (n_words, Q_BLOCK), 1)

    def prefix(limit):
        nt = jnp.clip((limit - word_pos + (SUBLANES - 1)) >> 3, 0, WORD_BITS)
        top = lax.shift_right_arithmetic(jnp.full_like(nt, INT_MIN), jnp.maximum(nt, 1) - 1)
        return jnp.where(nt <= 0, 0, top)

    cand = cand & prefix(qcol + 1)
    need = topk - cnt_gt
    sel_ref[...] = greater | cand

    @pl.when(jnp.max(_popcount_rows(cand) - need) > 0)
    def _():
        def jbody(bi, jt):
            cj = jt | jnp.left_shift(jnp.int32(1), idx_bits - 1 - bi)
            return jnp.where(_popcount_rows(cand & prefix(cj)) <= need, cj, jt)
        jt = lax.fori_loop(0, idx_bits, jbody, jnp.zeros((1, Q_BLOCK), I32))
        sel_ref[...] = greater | (cand & prefix(jt))

    def bias_body(c, carry):
        off = pl.multiple_of(c * tk, tk)
        for gi in range(groups_per_chunk):
            row0 = pl.multiple_of((c * groups_per_chunk + gi) * SUBLANES, SUBLANES)
            w = sel_ref[pl.ds(row0, SUBLANES), :]
            for t in range(WORD_BITS):
                dst = pl.multiple_of(off + gi * KEYS_PER_WORD_GROUP + t * SUBLANES, SUBLANES)
                bias_ref[pl.ds(dst, SUBLANES), :] = jnp.where((w << t) < 0, 0.0, -jnp.inf)
        return carry

    lax.fori_loop(0, nch, bias_body, 0)

    n_rep = N_HEADS_A // N_KV_A
    cols = n_rep * Q_BLOCK
    q_all = q_ref[0].reshape(N_HEADS_A * Q_BLOCK, HEAD_DIM)
    scale2 = (HEAD_DIM ** -0.5) * LOG2_E
    m_ref[...] = jnp.full(m_ref.shape, -1e30, F32)
    acc_ref[...] = jnp.zeros(acc_ref.shape, F32)
    ones_rows = jnp.ones((SUM_ROWS, tk), BF16)

    def stage_qk(g, off):
        qg = q_all[g * cols:(g + 1) * cols, :]
        s_ref[g] = _dot_nt(k_ref[0, pl.ds(off, tk), g * HEAD_DIM:(g + 1) * HEAD_DIM], qg)

    def stage_max(g, off):
        for n in range(n_rep):
            csl = slice(n * Q_BLOCK, (n + 1) * Q_BLOCK)
            t = s_ref[g, :, csl] + bias_ref[pl.ds(off, tk), :]
            s_ref[g, :, csl] = t
            m_old = m_ref[g, :, csl]
            m_new = jnp.maximum(m_old, jnp.max(t, axis=0, keepdims=True))
            alpha_ref[g, :, csl] = jnp.exp2((m_old - m_new) * scale2)
            m_ref[g, :, csl] = m_new

    def stage_exp(g, off):
        for n in range(n_rep):
            csl = slice(n * Q_BLOCK, (n + 1) * Q_BLOCK)
            p_ref[g, :, csl] = jnp.exp2((s_ref[g, :, csl] - m_ref[g, 0:1, csl]) * scale2).astype(BF16)

    def stage_pv(g, off):
        v_ext = jnp.concatenate(
            [vt_ref[0, g * HEAD_DIM:(g + 1) * HEAD_DIM, pl.ds(off, tk)], ones_rows], axis=0)
        acc_ref[g] = alpha_ref[g, 0:1, :] * acc_ref[g] + _dot(v_ext, p_ref[g])

    stages = (stage_qk, stage_max, stage_exp, stage_pv)

    def att_body(c, carry):
        off = pl.multiple_of(c * tk, tk)
        for step in range(N_KV_A + len(stages) - 1):
            for g in range(N_KV_A):
                if 0 <= step - g < len(stages):
                    stages[step - g](g, off)
        return carry

    lax.fori_loop(0, nch, att_body, 0)
    for g in range(N_KV_A):
        acc = acc_ref[g]
        o_t = acc[0:HEAD_DIM, :] / acc[HEAD_DIM:HEAD_DIM + 1, :]
        for n in range(n_rep):
            h = g * n_rep + n
            o_ref[:, h * HEAD_DIM:(h + 1) * HEAD_DIM] = jnp.transpose(
                o_t[:, n * Q_BLOCK:(n + 1) * Q_BLOCK]).astype(o_ref.dtype)


def _attention(qh, wi, kkiv, vt, batch, seq_len):
    nb = seq_len // Q_BLOCK
    topk = min(TOPK_MAX, seq_len // 4)
    tk = min(512, seq_len)
    kvd = N_KV_A * HEAD_DIM
    cols = N_HEADS_A // N_KV_A * Q_BLOCK
    single = pl.Buffered(1)
    return pl.pallas_call(
        functools.partial(_attn_kernel, topk=topk, tk=tk, idx_bits=seq_len.bit_length()),
        grid=(batch, nb),
        in_specs=[pl.BlockSpec((1, N_HEADS_A, Q_BLOCK, HEAD_DIM), lambda b, i: (b * nb + i, 0, 0, 0)),
                  pl.BlockSpec((1, N_HEADS_IDX, Q_BLOCK, HEAD_DIM), lambda b, i: (b * nb + i, 1, 0, 0)),
                  pl.BlockSpec((Q_BLOCK, LANES), lambda b, i: (b * nb + i, 0)),
                  pl.BlockSpec((1, seq_len, kvd), lambda b, i: (b, 0, 0), pipeline_mode=single),
                  pl.BlockSpec((1, seq_len, HEAD_DIM), lambda b, i: (b, 0, kvd // HEAD_DIM),
                               pipeline_mode=single),
                  pl.BlockSpec((1, kvd, seq_len), lambda b, i: (b, 0, 0), pipeline_mode=single)],
        out_specs=pl.BlockSpec((Q_BLOCK, N_HEADS_A * HEAD_DIM), lambda b, i: (b * nb + i, 0)),
        out_shape=jax.ShapeDtypeStruct((batch * seq_len, N_HEADS_A * HEAD_DIM), BF16),
        scratch_shapes=[pltpu.VMEM((WORD_BITS, seq_len // WORD_BITS, Q_BLOCK), I32),
                        pltpu.VMEM((seq_len // WORD_BITS, Q_BLOCK), I32),
                        pltpu.VMEM((seq_len, Q_BLOCK), F32),
                        pltpu.VMEM((N_KV_A, SUBLANES, cols), F32),
                        pltpu.VMEM((N_KV_A, HEAD_DIM + SUM_ROWS, cols), F32),
                        pltpu.VMEM((N_KV_A, SUBLANES, cols), F32),
                        pltpu.VMEM((N_KV_A, tk, cols), F32),
                        pltpu.VMEM((N_KV_A, tk, cols), BF16)],
        compiler_params=_cparams(("arbitrary", "arbitrary"), 52),
        name="dsa_attention",
    )(qh, qh, wi, kkiv, kkiv, vt)


def _dot_hp(x, w_bf):
    hi = x.astype(BF16)
    r1 = x - hi.astype(F32)
    mid = r1.astype(BF16)
    lo = (r1 - mid.astype(F32)).astype(BF16)
    n = x.shape[0]
    parts = _dot(jnp.concatenate([hi, mid, lo], axis=0), w_bf)
    return parts[0:n] + parts[n:2 * n] + parts[2 * n:3 * n]


def _rwkv_kernel(r_ref, k_ref, v_ref, lw_ref, a_ref, g_ref, mu_ref, kk_ref, ka_ref, rk_ref,
                 lnw_ref, lnb_ref, bd_ref, ones_ref, ltri_ref, ts_ref, ti_ref, ic_ref, eye_ref,
                 o_ref, st_ref, prev_ref):
    @pl.when(pl.program_id(2) == 0)
    def _():
        st_ref[...] = jnp.zeros_like(st_ref)
        prev_ref[...] = jnp.zeros_like(prev_ref)

    tiles = [_rwkv_tile(tile, r_ref, k_ref, v_ref, lw_ref, a_ref, g_ref, mu_ref, kk_ref, ka_ref, rk_ref,
                        lnw_ref, lnb_ref, bd_ref, ones_ref, ltri_ref, ts_ref, ti_ref, ic_ref, eye_ref,
                        o_ref, st_ref, prev_ref)
             for tile in range(r_ref.shape[1] // TILE_B)]
    while tiles:
        tiles = [t for t in tiles if next(t, "done") != "done"]


def _rwkv_tile(tile, r_ref, k_ref, v_ref, lw_ref, a_ref, g_ref, mu_ref, kk_ref, ka_ref, rk_ref,
               lnw_ref, lnb_ref, bd_ref, ones_ref, ltri_ref, ts_ref, ti_ref, ic_ref, eye_ref,
               o_ref, st_ref, prev_ref):
    cols = slice(tile * TILE_B, (tile + 1) * TILE_B)
    bdm = bd_ref[...]
    row = lax.broadcasted_iota(I32, (CHUNK, TILE_B), 0)

    def shifted(x, slot):
        prev = prev_ref[slot:slot + 1, cols]
        prev_ref[slot:slot + 1, cols] = x[CHUNK - 1:CHUNK, :]
        return jnp.where(row == 0, prev, pltpu.roll(x, 1, axis=0))

    r0 = r_ref[:, cols]
    k0 = k_ref[:, cols]
    v0 = v_ref[:, cols]
    r = r0 + (shifted(r0, 0) - r0) * mu_ref[0:1, cols]
    k = k0 + (shifted(k0, 1) - k0) * mu_ref[1:2, cols]
    v = v0 + (shifted(v0, 2) - v0) * mu_ref[2:3, cols]
    a = a_ref[:, cols]
    ones_bd = ones_ref[...]

    kk = k * kk_ref[:, cols]
    k2 = k * (1.0 + (a - 1.0) * ka_ref[:, cols])
    head_sums = _dot_hp(jnp.concatenate([kk * kk, r * k2 * rk_ref[:, cols]], axis=0), ones_bd)
    ss = head_sums[0:CHUNK]
    bonus = head_sums[CHUNK:2 * CHUNK] * v
    yield
    kk = kk / jnp.maximum(jnp.sqrt(ss), 1e-12)
    aa = -kk
    bb = kk * a

    lw = lw_ref[:, cols]
    cs = _cumsum_rows(lw, ltri_ref[...])
    yield
    tot = cs[CHUNK - 1:CHUNK, :]
    e_in = jnp.exp(cs)
    e_out = jnp.exp(-cs)
    e_tail = jnp.exp(tot - cs)
    at = aa * jnp.exp(cs - lw)
    rt = r * e_in
    bt = (bb * e_out)
    kt = (k2 * e_out)
    bh = (bb * e_tail).astype(BF16)
    kh = (k2 * e_tail).astype(BF16)
    wc = jnp.exp(tot)

    def bd(x):
        return jnp.concatenate([x.astype(BF16)] * HEADS_PER_TILE, axis=0) * ones_bd

    lhs = jnp.concatenate([at, rt], axis=0).astype(BF16)
    mb = _dot_nt(lhs, bd(bt))
    mk = _dot_nt(lhs, bd(kt))
    yield
    ts = ts_ref[...]
    ti = ti_ref[...]
    m_ab = mb[0:CHUNK] * ts
    m_rb = (mb[CHUNK:2 * CHUNK] * ti).astype(BF16)
    m_ak = (mk[0:CHUNK] * ts).astype(BF16)
    m_rk = (mk[CHUNK:2 * CHUNK] * ti).astype(BF16)

    n_round = CHUNK.bit_length() - 1
    t_inv = ic_ref[...] + m_ab
    m_pow = _dot(m_ab.astype(BF16), bd(m_ab))
    yield
    for rnd in range(1, n_round):
        last = rnd == n_round - 1
        lhs_rows = [t_inv] if last else [t_inv, m_pow]
        prod = _dot(jnp.concatenate(lhs_rows, axis=0).astype(BF16), bd(m_pow))
        yield
        t_inv = t_inv + prod[0:CHUNK]
        if not last:
            m_pow = prod[CHUNK:2 * CHUNK]
    t_bf = t_inv.astype(BF16)

    p = _dot(t_bf, bd(at))
    uv = _dot(jnp.concatenate([m_ak, m_rk], axis=0), bd(v))
    u = uv[0:CHUNK]
    yield
    q = _dot(t_bf, bd(u))
    rp = rt + _dot(m_rb, bd(p))
    yield
    y0 = _dot(m_rb, bd(q)) + uv[CHUNK:2 * CHUNK]

    yield
    a_t = _dot_tn(bh, p.astype(BF16)) * bdm + eye_ref[...] * wc
    d_t = _dot_tn(jnp.concatenate([bh, kh], axis=0),
                  jnp.concatenate([q.astype(BF16), v.astype(BF16)], axis=0)) * bdm
    st = st_ref[tile]
    from_state = _dot(jnp.concatenate([rp.astype(BF16), a_t.astype(BF16)], axis=0), st.astype(BF16))
    y = from_state[0:CHUNK] + y0
    st_ref[tile] = from_state[CHUNK:CHUNK + TILE_B] + d_t

    yield
    inv_n = 1.0 / HEAD_DIM_B
    mu = _dot_hp(y, ones_bd) * inv_n
    yield
    yc = y - mu
    var = _dot_hp(yc * yc, ones_bd) * inv_n
    yield
    yn =(yc * lax.rsqrt(var + GN_EPS)) * lnw_ref[:, cols] + lnb_ref[:, cols]
    o_ref[:, cols] = ((yn + bonus) * g_ref[:, cols]).astype(o_ref.dtype)


def _cumsum_rows(x, ltri_bf):
    hi = x.astype(BF16)
    r1 = x - hi.astype(F32)
    mid = r1.astype(BF16)
    lo = (r1 - mid.astype(F32)).astype(BF16)
    return _dot(ltri_bf, hi) + _dot(ltri_bf, mid) + _dot(ltri_bf, lo)


def _rwkv_consts():
    idx = jnp.arange(TILE_B)
    head = idx // HEAD_DIM_B
    bdm = (head[:, None] == head[None, :])
    t = jnp.arange(CHUNK)[:, None]
    s = (idx % CHUNK)[None, :]
    return dict(
        bd=bdm.astype(F32),
        ones=bdm.astype(BF16),
        ltri=(jnp.arange(CHUNK)[None, :] <= t).astype(BF16),
        ts=(s < t).astype(F32),
        ti=(s <= t).astype(F32),
        ic=(s == t).astype(F32),
        eye=jnp.eye(TILE_B, dtype=F32),
    )


def _rwkv(rkv, lw, a, g, mu_rkv, k_k, k_a, r_k, ln_w, ln_b, batch, seq_len):
    m, db = lw.shape
    nc = seq_len // CHUNK
    width = RWKV_TILES_PER_STEP * TILE_B
    ng = db // width
    cst = _rwkv_consts()
    blk = lambda off: pl.BlockSpec((CHUNK, width), lambda b, h, c, off=off: (b * nc + c, off + h))
    par = lambda rows: pl.BlockSpec((rows, width), lambda b, h, c: (0, h))
    full = lambda arr: pl.BlockSpec(arr.shape, lambda b, h, c: (0, 0))
    row = lambda p: p.reshape(1, db)
    return pl.pallas_call(
        _rwkv_kernel,
        grid=(batch, ng, nc),
        in_specs=[blk(0), blk(ng), blk(2 * ng), blk(0), blk(0), blk(0),
                  par(3), par(1), par(1), par(1), par(1), par(1),
                  full(cst["bd"]), full(cst["ones"]), full(cst["ltri"]), full(cst["ts"]),
                  full(cst["ti"]), full(cst["ic"]), full(cst["eye"])],
        out_specs=blk(0),
        out_shape=jax.ShapeDtypeStruct((m, db), BF16),
        scratch_shapes=[pltpu.VMEM((RWKV_TILES_PER_STEP, TILE_B, TILE_B), F32),
                        pltpu.VMEM((SUBLANES, width), F32)],
        compiler_params=_cparams(("arbitrary", "arbitrary", "arbitrary"), 32),
        name="rwkv7",
    )(rkv, rkv, rkv, lw, a, g, mu_rkv, row(k_k), row(k_a), row(r_k), row(ln_w), row(ln_b),
      cst["bd"], cst["ones"], cst["ltri"], cst["ts"], cst["ti"], cst["ic"], cst["eye"])


def _merge_kernel(xn_ref, oa_ref, ob_ref, wga_ref, wgb_ref, bga_ref, bgb_ref, wpa_ref, wpb_ref, o_ref):
    xn = xn_ref[...]
    ga = _sigmoid(_dot(xn, wga_ref[...]) + bga_ref[...])
    gb = _sigmoid(_dot(xn, wgb_ref[...]) + bgb_ref[...])
    ya = _dot(oa_ref[...], wpa_ref[...])
    yb = _dot(ob_ref[...], wpb_ref[...])
    o_ref[...] = (ga * ya + gb * yb).astype(o_ref.dtype)


def _merge(xn, oa, ob, w_gate, b_gate, w_pa, w_pb, bm=1024, bn=256):
    m, d = xn.shape
    bm = min(bm, m)
    nj = d // bn
    return pl.pallas_call(
        _merge_kernel,
        grid=(m // bm, nj),
        in_specs=[pl.BlockSpec((bm, d), lambda i, j: (i, 0)),
                  pl.BlockSpec((bm, oa.shape[1]), lambda i, j: (i, 0)),
                  pl.BlockSpec((bm, ob.shape[1]), lambda i, j: (i, 0)),
                  pl.BlockSpec((d, bn), lambda i, j: (0, j)),
                  pl.BlockSpec((d, bn), lambda i, j: (0, j + nj)),
                  pl.BlockSpec((1, bn), lambda i, j: (0, j)),
                  pl.BlockSpec((1, bn), lambda i, j: (0, j + nj)),
                  pl.BlockSpec((w_pa.shape[0], bn), lambda i, j: (0, j)),
                  pl.BlockSpec((w_pb.shape[0], bn), lambda i, j: (0, j))],
        out_specs=pl.BlockSpec((bm, bn), lambda i, j: (i, j)),
        out_shape=jax.ShapeDtypeStruct((m, d), BF16),
        compiler_params=_cparams(("arbitrary", "arbitrary"), 52),
        name="gated_merge",
    )(xn, oa, ob, w_gate, w_gate, b_gate, b_gate, w_pa, w_pb)


def _mm_res_kernel(x_ref, w_ref, res_ref, o_ref):
    o_ref[...] = res_ref[...] + _dot(x_ref[...], w_ref[...])


def _matmul_residual(x, w, res, bm=1024, bn=1024, name="matmul_res"):
    m, k = x.shape
    n = w.shape[1]
    bm = min(bm, m)
    return pl.pallas_call(
        _mm_res_kernel,
        grid=(m // bm, n // bn),
        in_specs=[pl.BlockSpec((bm, k), lambda i, j: (i, 0)),
                  pl.BlockSpec((k, bn), lambda i, j: (0, j)),
                  pl.BlockSpec((bm, bn), lambda i, j: (i, j))],
        out_specs=pl.BlockSpec((bm, bn), lambda i, j: (i, j)),
        out_shape=jax.ShapeDtypeStruct((m, n), F32),
        compiler_params=_cparams(("arbitrary", "arbitrary"), 52),
        name=name,
    )(x, w, res)


def _ffn_up_kernel(x_ref, w1_ref, w3_ref, o_ref):
    x = x_ref[...]
    h1 = _dot(x, w1_ref[...])
    h3 = _dot(x, w3_ref[...])
    o_ref[...] = (h1 * _sigmoid(h1) * h3).astype(o_ref.dtype)


def _ffn_up(x, w1, w3, bm=1024, bn=512):
    m, k = x.shape
    n = w1.shape[1]
    bm = min(bm, m)
    return pl.pallas_call(
        _ffn_up_kernel,
        grid=(m // bm, pl.cdiv(n, bn)),
        in_specs=[pl.BlockSpec((bm, k), lambda i, j: (i, 0)),
                  pl.BlockSpec((k, bn), lambda i, j: (0, j)),
                  pl.BlockSpec((k, bn), lambda i, j: (0, j))],
        out_specs=pl.BlockSpec((bm, bn), lambda i, j: (i, j)),
        out_shape=jax.ShapeDtypeStruct((m, n), BF16),
        compiler_params=_cparams(("arbitrary", "arbitrary"), 52),
        name="ffn_up",
    )(x, w1, w3)


def _ple_kernel(h_ref, wg_ref, p_ref, wp_ref, gain_ref, o_ref, hn_ref, *, bn, final_norm):
    j = pl.program_id(1)
    n_row_chunks = h_ref.shape[0] // NORM_ROW_CHUNK

    def row_chunk(r):
        return pl.ds(pl.multiple_of(r * NORM_ROW_CHUNK, NORM_ROW_CHUNK), NORM_ROW_CHUNK)

    @pl.when(j == 0)
    def _():
        def body(r, carry):
            hn_ref[row_chunk(r), :] = _rms_rows(h_ref[row_chunk(r), :]).astype(BF16)
            return carry
        lax.fori_loop(0, n_row_chunks, body, 0)

    col = pl.multiple_of(j * bn, bn)
    gate = _sigmoid(_dot(hn_ref[...], wg_ref[...]))
    o_ref[:, pl.ds(col, bn)] = h_ref[:, pl.ds(col, bn)] + gate * _dot(p_ref[...], wp_ref[...])

    if final_norm:
        @pl.when(j == pl.num_programs(1) - 1)
        def _():
            def body(r, carry):
                o_ref[row_chunk(r), :] = _rms_rows(o_ref[row_chunk(r), :]) * gain_ref[...]
                return carry
            lax.fori_loop(0, n_row_chunks, body, 0)


def _ple(h, wg, p, wp, final_gain, bm=512, bn=512):
    m, d = h.shape
    bm = min(bm, m)
    final_norm = final_gain is not None
    gain = (final_gain if final_norm else jnp.ones((d,), F32)).reshape(1, d)
    return pl.pallas_call(
        functools.partial(_ple_kernel, bn=bn, final_norm=final_norm),
        grid=(m // bm, d // bn),
        in_specs=[pl.BlockSpec((bm, d), lambda i, j: (i, 0)),
                  pl.BlockSpec((d, bn), lambda i, j: (0, j)),
                  pl.BlockSpec((bm, p.shape[1]), lambda i, j: (i, 0)),
                  pl.BlockSpec((p.shape[1], bn), lambda i, j: (0, j)),
                  pl.BlockSpec((1, d), lambda i, j: (0, 0))],
        out_specs=pl.BlockSpec((bm, d), lambda i, j: (i, 0)),
        out_shape=jax.ShapeDtypeStruct((m, d), F32),
        scratch_shapes=[pltpu.VMEM((bm, d), BF16)],
        compiler_params=_cparams(("arbitrary", "arbitrary"), 52),
        name="ple",
    )(h, wg, p, wp, gain)


def _rmsnorm_kernel(x_ref, g_ref, o_ref):
    o_ref[...] = (_rms_rows(x_ref[...]) * g_ref[...]).astype(o_ref.dtype)


def _rmsnorm(x, gain, out_dtype, bm=512):
    m, d = x.shape
    bm = min(bm, m)
    row_spec = pl.BlockSpec((bm, d), lambda i: (i, 0))
    return pl.pallas_call(
        _rmsnorm_kernel,
        grid=(m // bm,),
        in_specs=[row_spec, pl.BlockSpec((1, d), lambda i: (0, 0))],
        out_specs=row_spec,
        out_shape=jax.ShapeDtypeStruct((m, d), out_dtype),
        compiler_params=_cparams(("arbitrary",), 48),
        name="rmsnorm",
    )(x, gain.reshape(1, d))


def _pad_to(a, axis, size):
    pad = [(0, 0)] * a.ndim
    pad[axis] = (0, size - a.shape[axis])
    return jnp.pad(a, pad)


def _layer(h2, p2, cos, sin, batch, seq_len, norm_mix, w_in, mu_rkv, mu_wag, w0, w1, w2, a0, a1, a2,
           g1, g2, k_k, k_a, r_k, ln_w, ln_b, w_pa, w_pb, w_gate, b_gate, w_o, norm_ffn,
           w_ffn1, w_ffn3, w_ffn2, w_ple_gate, w_ple, final_gain):
    d = h2.shape[1]
    d_a = N_HEADS_A * HEAD_DIM
    kvd = N_KV_A * HEAD_DIM
    d_iq = N_HEADS_IDX * HEAD_DIM
    d_b = w_pb.shape[0]
    o_q, o_k, o_v = 0, d_a, d_a + kvd
    o_qi = o_v + kvd
    o_ki = o_qi + d_iq
    o_wi = o_ki + HEAD_DIM
    o_r = o_wi + N_HEADS_IDX

    bf = lambda a: a.astype(BF16)
    w_in_bf = bf(w_in)
    assert d_a == d_iq and o_wi % LANES == 0
    w_rkv = w_in_bf[:, o_r:o_r + 3 * d_b]
    lora = LANES
    w1p, a1p = bf(_pad_to(w1, 1, lora)), bf(_pad_to(a1, 1, lora))
    w2p, a2p = bf(_pad_to(w2, 0, lora)), bf(_pad_to(a2, 0, lora))

    xn, lw, a, g, wi = _prep(h2, norm_mix.reshape(1, d), mu_wag, w1p, w2p, w0.reshape(1, d_b),
                             a1p, a2p, a0.reshape(1, d_b), bf(g1), bf(g2), w_in_bf, o_wi // LANES,
                             seq_len)

    qh = _proj_rope_headmajor(xn, w_in_bf, (o_q, o_qi), d_a, cos, sin)
    kkiv = _proj_kv(xn, w_in_bf, o_k, o_ki, o_v, cos, sin)
    rkv = _matmul(xn, w_rkv, F32, name="proj_rkv")

    kkiv = kkiv.reshape(batch, seq_len, -1)
    vt = jnp.swapaxes(kkiv[:, :, kvd + HEAD_DIM:], 1, 2)
    o_att = _attention(qh, wi, kkiv, vt, batch, seq_len)

    o_rwkv = _rwkv(rkv, lw, a, g, mu_rkv, k_k, k_a, r_k.reshape(-1), ln_w, ln_b, batch, seq_len)

    mixed = _merge(xn, o_att, o_rwkv, bf(w_gate), b_gate.reshape(1, -1), bf(w_pa), bf(w_pb))
    h2 = _matmul_residual(mixed, bf(w_o), h2, bn=512, name="out_proj")

    xf = _rmsnorm(h2, norm_ffn, BF16)
    u = _ffn_up(xf, bf(w_ffn1), bf(w_ffn3))
    h2 = _matmul_residual(u, bf(w_ffn2), h2, bm=512, bn=512, name="ffn_down")

    return _ple(h2, bf(w_ple_gate), bf(p2), bf(w_ple), final_gain)


def kernel(x, p, positions, norm_mix, w_in, mu_rkv, mu_wag, w0, w1, w2, a0, a1, a2, g1, g2, k_k, k_a,
           r_k, ln_w, ln_b, w_pa, w_pb, w_gate, b_gate, w_o, norm_ffn, w_ffn1, w_ffn3, w_ffn2,
           w_ple_gate, w_ple, norm_final):
    batch, seq_len, d = x.shape
    depth = p.shape[0]
    h2 = x.reshape(batch * seq_len, d)
    cos, sin = _rope_tables(positions)
    for i in range(depth):
        h2 = _layer(h2, p[i].reshape(batch * seq_len, -1), cos, sin, batch, seq_len,
                    norm_mix[i], w_in[i], mu_rkv[i], mu_wag[i], w0[i], w1[i], w2[i], a0[i], a1[i], a2[i],
                    g1[i], g2[i], k_k[i], k_a[i], r_k[i], ln_w[i], ln_b[i], w_pa[i], w_pb[i], w_gate[i],
                    b_gate[i], w_o[i], norm_ffn[i], w_ffn1[i], w_ffn3[i], w_ffn2[i], w_ple_gate[i],
                    w_ple[i], norm_final if i == depth - 1 else None)
    return h2.reshape(batch, seq_len, d)
```

```python
import functools

import jax
import jax.numpy as jnp
from jax import lax
from jax.experimental import pallas as pl
from jax.experimental.pallas import tpu as pltpu

F32 = jnp.float32
BF16 = jnp.bfloat16
I32 = jnp.int32

N_HEADS_A = 16
HEAD_DIM = 128
N_KV_A = 4
N_HEADS_IDX = 16
TOPK_MAX = 256
Q_BLOCK = 128
ROPE_THETA = 10000.0
HEAD_DIM_B = 64
GN_EPS = 64e-5
RMS_EPS = 1e-6

LANES = 128
SUBLANES = 8
BF16_SUBLANES = 16
MXU_DIM = 256

INT_MIN = -2 ** 31
LOG2_E = 1.4426950408889634
NORM_ROW_CHUNK = 64
DECAY_SCALE = 0.6065306597126334

CHUNK = 64
HEADS_PER_TILE = MXU_DIM // HEAD_DIM_B
TILE_B = HEADS_PER_TILE * HEAD_DIM_B
RWKV_TILES_PER_STEP = 4


def _cparams(sem, vmem_mib):
    return pltpu.CompilerParams(dimension_semantics=sem, vmem_limit_bytes=vmem_mib << 20)


def _sidecar_cast(body, n_in):
    def kern(*refs):
        refs[n_in + 2][...] = refs[n_in][...].astype(BF16)
        body(*refs[:n_in], refs[n_in + 1])
    return kern


def _matmul_call(body, grid, in_specs, out_spec, out_shape, args, name, vmem_mib=52, side=None):
    params = _cparams(("arbitrary", "arbitrary"), vmem_mib)
    if side is None:
        out = pl.pallas_call(body, grid=grid, in_specs=in_specs, out_specs=out_spec,
                             out_shape=out_shape, compiler_params=params, name=name)(*args)
        return out, None
    steps, n_inner = grid[0] * grid[1], grid[1]
    rows = side.shape[0]
    rb = -(-(-(-rows // steps)) // BF16_SUBLANES) * BF16_SUBLANES
    last = -(-rows // rb) - 1
    side_spec = pl.BlockSpec((rb, side.shape[1]), lambda i, j: (jnp.minimum(i * n_inner + j, last), 0))
    return pl.pallas_call(
        _sidecar_cast(body, len(in_specs)), grid=grid, in_specs=[*in_specs, side_spec],
        out_specs=[out_spec, side_spec],
        out_shape=[out_shape, jax.ShapeDtypeStruct(side.shape, BF16)],
        compiler_params=params, name=name)(*args, side)


def _dot(a, b):
    return jnp.dot(a, b, preferred_element_type=F32)


def _dot_nt(a, b):
    return lax.dot_general(a, b, (((1,), (1,)), ((), ())), preferred_element_type=F32)


def _dot_tn(a, b):
    return lax.dot_general(a, b, (((0,), (0,)), ((), ())), preferred_element_type=F32)


def _sigmoid(x):
    return 1.0 / (1.0 + jnp.exp(-x))


def _rms_rows(x):
    return x * lax.rsqrt(jnp.mean(x * x, axis=-1, keepdims=True) + RMS_EPS)


def _rope_tab_kernel(pos_ref, freq_ref, sign_ref, cos_ref, sin_ref):
    ang = pos_ref[...].astype(F32) * freq_ref[...]
    cos_ref[...] = jnp.cos(ang)
    sin_ref[...] = jnp.sin(ang) * sign_ref[...]


def _rope_tables(positions):
    n = positions.size
    half = HEAD_DIM // 2
    inv_freq = ROPE_THETA ** (-jnp.arange(0, HEAD_DIM, 2, dtype=F32) / HEAD_DIM)
    freq2 = jnp.concatenate([inv_freq, inv_freq]).reshape(1, HEAD_DIM)
    sign = jnp.concatenate([-jnp.ones((half,), F32), jnp.ones((half,), F32)]).reshape(1, HEAD_DIM)
    bm = min(2048, n)
    return pl.pallas_call(
        _rope_tab_kernel,
        grid=(n // bm,),
        in_specs=[pl.BlockSpec((bm, 1), lambda i: (i, 0)),
                  pl.BlockSpec((1, HEAD_DIM), lambda i: (0, 0)),
                  pl.BlockSpec((1, HEAD_DIM), lambda i: (0, 0))],
        out_specs=[pl.BlockSpec((bm, HEAD_DIM), lambda i: (i, 0)),
                   pl.BlockSpec((bm, HEAD_DIM), lambda i: (i, 0))],
        out_shape=[jax.ShapeDtypeStruct((n, HEAD_DIM), F32)] * 2,
        compiler_params=_cparams(("arbitrary",), 32),
        name="rope_tables",
    )(positions.reshape(n, 1), freq2, sign)


def _rope(t, cos, sin):
    return t * cos + pltpu.roll(t, HEAD_DIM // 2, axis=1) * sin


def _prep_kernel(x_ref, xp_ref, gain_ref, mu_ref, w1_ref, w2_ref, w0_ref, a1_ref, a2_ref, a0_ref,
                 g1_ref, g2_ref, wwi_ref,
                 xn_ref, lw_ref, a_ref, g_ref, wi_ref, *, seq_len, bm):
    i = pl.program_id(0)
    gain = gain_ref[...]
    xn = _rms_rows(x_ref[...]) * gain
    prev = (_rms_rows(xp_ref[...]) * gain)[SUBLANES - 1:SUBLANES, :]
    prev = jnp.where((i * bm) % seq_len == 0, jnp.zeros_like(prev), prev)
    row = lax.broadcasted_iota(I32, xn.shape, 0)
    sh = jnp.where(row == 0, prev, pltpu.roll(xn, 1, axis=0))
    xx = sh - xn
    xn_bf = xn.astype(BF16)
    xn_ref[...] = xn_bf
    wi_ref[...] = _dot(xn_bf, wwi_ref[...]) * (N_HEADS_IDX ** -0.5)

    xw = (xn + xx * mu_ref[0:1, :]).astype(BF16)
    hw = jnp.tanh(_dot(xw, w1_ref[...])).astype(BF16)
    wl = w0_ref[...] + _dot(hw, w2_ref[...])
    lw_ref[...] = -DECAY_SCALE * _sigmoid(wl)

    xa = (xn + xx * mu_ref[1:2, :]).astype(BF16)
    ha = _dot(xa, a1_ref[...]).astype(BF16)
    a_ref[...] = _sigmoid(a0_ref[...] + _dot(ha, a2_ref[...]))

    xg = (xn + xx * mu_ref[2:3, :]).astype(BF16)
    hg = _sigmoid(_dot(xg, g1_ref[...])).astype(BF16)
    g_ref[...] = _dot(hg, g2_ref[...])


def _prep(x2, gain, mu_wag, w1, w2, w0, a1, a2, a0, g1, g2, w_in, wi_block, seq_len):
    m, d = x2.shape
    db = w2.shape[1]
    bm = min(128, m)
    full = lambda a: pl.BlockSpec(a.shape, lambda i: (0,) * a.ndim)
    nsub = bm // SUBLANES
    return pl.pallas_call(
        functools.partial(_prep_kernel, seq_len=seq_len, bm=bm),
        grid=(m // bm,),
        in_specs=[pl.BlockSpec((bm, d), lambda i: (i, 0)),
                  pl.BlockSpec((SUBLANES, d), lambda i: (jnp.maximum(i * nsub - 1, 0), 0)),
                  full(gain), full(mu_wag), full(w1), full(w2), full(w0), full(a1), full(a2), full(a0),
                  full(g1), full(g2), pl.BlockSpec((d, LANES), lambda i: (0, wi_block))],
        out_specs=[pl.BlockSpec((bm, d), lambda i: (i, 0)),
                   pl.BlockSpec((bm, db), lambda i: (i, 0)),
                   pl.BlockSpec((bm, db), lambda i: (i, 0)),
                   pl.BlockSpec((bm, db), lambda i: (i, 0)),
                   pl.BlockSpec((bm, LANES), lambda i: (i, 0))],
        out_shape=[jax.ShapeDtypeStruct((m, d), BF16),
                   jax.ShapeDtypeStruct((m, db), F32),
                   jax.ShapeDtypeStruct((m, db), F32),
                   jax.ShapeDtypeStruct((m, db), F32),
                   jax.ShapeDtypeStruct((m, LANES), F32)],
        compiler_params=_cparams(("arbitrary",), 48),
        name="prep",
    )(x2, x2, gain, mu_wag, w1, w2, w0, a1, a2, a0, g1, g2, w_in)


def _proj_rope_hm_kernel(x_ref, w_ref, cos_ref, sin_ref, o_ref):
    acc = _dot(x_ref[...], w_ref[...])
    cos = cos_ref[...]
    sin = sin_ref[...]
    bm, bn = acc.shape
    for j in range(bn // HEAD_DIM):
        t = _rope(acc[:, j * HEAD_DIM:(j + 1) * HEAD_DIM], cos, sin).astype(o_ref.dtype)
        for r in range(bm // Q_BLOCK):
            o_ref[r, j] = t[r * Q_BLOCK:(r + 1) * Q_BLOCK, :]


def _proj_rope_headmajor(xn, w, col_starts, width, cos, sin, bm=1024, bn=1024, side=None):
    m, k = xn.shape
    n = width * len(col_starts)
    bm = min(bm, m)
    per = width // bn
    first, second = (c // bn for c in col_starts)

    def w_block(i, j):
        return 0, jnp.where(j < per, first + j, second + j - per)

    return _matmul_call(
        _proj_rope_hm_kernel, (m // bm, n // bn),
        [pl.BlockSpec((bm, k), lambda i, j: (i, 0)),
         pl.BlockSpec((k, bn), w_block),
         pl.BlockSpec((bm, HEAD_DIM), lambda i, j: (i, 0)),
         pl.BlockSpec((bm, HEAD_DIM), lambda i, j: (i, 0))],
        pl.BlockSpec((bm // Q_BLOCK, bn // HEAD_DIM, Q_BLOCK, HEAD_DIM), lambda i, j: (i, j, 0, 0)),
        jax.ShapeDtypeStruct((m // Q_BLOCK, n // HEAD_DIM, Q_BLOCK, HEAD_DIM), BF16),
        (xn, w, cos, sin), "proj_q", side=side)


def _proj_kv_kernel(x_ref, wk_ref, wki_ref, wv_ref, cos_ref, sin_ref, o_ref):
    x = x_ref[...]
    cos = cos_ref[...]
    sin = sin_ref[...]
    col = 0
    for w_ref, rotary in ((wk_ref, True), (wki_ref, True), (wv_ref, False)):
        acc = _dot(x, w_ref[...])
        for j in range(acc.shape[1] // HEAD_DIM):
            t = acc[:, j * HEAD_DIM:(j + 1) * HEAD_DIM]
            if rotary:
                t = _rope(t, cos, sin)
            o_ref[:, col:col + HEAD_DIM] = t.astype(o_ref.dtype)
            col += HEAD_DIM


def _proj_kv(xn, w, col_k, col_ki, col_v, cos, sin, bm=1024):
    m, k = xn.shape
    kvd = N_KV_A * HEAD_DIM
    n = 2 * kvd + HEAD_DIM
    bm = min(bm, m)
    return pl.pallas_call(
        _proj_kv_kernel,
        grid=(m // bm,),
        in_specs=[pl.BlockSpec((bm, k), lambda i: (i, 0)),
                  pl.BlockSpec((k, kvd), lambda i: (0, col_k // kvd)),
                  pl.BlockSpec((k, HEAD_DIM), lambda i: (0, col_ki // HEAD_DIM)),
                  pl.BlockSpec((k, kvd), lambda i: (0, col_v // kvd)),
                  pl.BlockSpec((bm, HEAD_DIM), lambda i: (i, 0)),
                  pl.BlockSpec((bm, HEAD_DIM), lambda i: (i, 0))],
        out_specs=pl.BlockSpec((bm, n), lambda i: (i, 0)),
        out_shape=jax.ShapeDtypeStruct((m, n), BF16),
        compiler_params=_cparams(("arbitrary",), 52),
        name="proj_kv",
    )(xn, w, w, w, cos, sin)


def _mm_kernel(x_ref, w_ref, o_ref):
    o_ref[...] = _dot(x_ref[...], w_ref[...]).astype(o_ref.dtype)


def _matmul(x, w, out_dtype, bm=1024, bn=1024, name="matmul", side=None):
    m, k = x.shape
    n = w.shape[1]
    bm = min(bm, m)
    return _matmul_call(
        _mm_kernel, (m // bm, n // bn),
        [pl.BlockSpec((bm, k), lambda i, j: (i, 0)),
         pl.BlockSpec((k, bn), lambda i, j: (0, j))],
        pl.BlockSpec((bm, bn), lambda i, j: (i, j)),
        jax.ShapeDtypeStruct((m, n), out_dtype), (x, w), name, side=side)


WORD_BITS = 32
SUM_ROWS = 16
KEYS_PER_WORD_GROUP = WORD_BITS * SUBLANES


def _bit_transpose32(words):
    a = list(words)
    j, m = 16, 0x0000FFFF
    while j:
        mask = jnp.int32(m - (1 << 32) if m >= (1 << 31) else m)
        k = 0
        while k < WORD_BITS:
            t = (a[k] ^ lax.shift_right_logical(a[k + j], jnp.full_like(a[k], j))) & mask
            a[k] = a[k] ^ t
            a[k + j] = a[k + j] ^ (t << j)
            k = (k + j + 1) & ~j
        j >>= 1
        m = (m ^ (m << j)) & 0xFFFFFFFF
    return a


def _popcount_rows(words):
    per_sublane = jnp.sum(lax.population_count(words).reshape(-1, SUBLANES, words.shape[1]), axis=0)
    return jnp.sum(per_sublane.astype(F32), axis=0, keepdims=True)


def _attn_kernel(q_ref, qi_ref, wi_ref, k_ref, ki_ref, vt_ref, o_ref,
                 plane_ref, sel_ref, bias_ref, m_ref, acc_ref, alpha_ref, s_ref, p_ref,
                 *, topk, tk, idx_bits):
    i = pl.program_id(1)
    nk = (i + 1) * Q_BLOCK
    nch = (nk + tk - 1) // tk
    n_words = plane_ref.shape[1]
    groups_per_chunk = tk // KEYS_PER_WORD_GROUP

    @pl.when(i == 0)
    def _():
        plane_ref[...] = jnp.zeros_like(plane_ref)

    qi = qi_ref[0].reshape(N_HEADS_IDX * Q_BLOCK, HEAD_DIM)
    wi_t = jnp.transpose(wi_ref[...]) * (HEAD_DIM ** -0.5)
    qpos = i * Q_BLOCK + lax.broadcasted_iota(I32, (tk, Q_BLOCK), 1)
    krow = lax.broadcasted_iota(I32, (tk, Q_BLOCK), 0)

    def score_body(c, carry):
        off = pl.multiple_of(c * tk, tk)
        lg = _dot_nt(ki_ref[0, pl.ds(off, tk), :], qi)
        sc = jnp.zeros((tk, Q_BLOCK), F32)
        for h in range(N_HEADS_IDX):
            sc = sc + wi_t[h:h + 1, :] * jnp.maximum(lg[:, h * Q_BLOCK:(h + 1) * Q_BLOCK], 0.0)
        bits = pltpu.bitcast(sc, I32)
        key = bits ^ ((bits >> 31) & 0x7FFFFFFF)
        key = jnp.where(key == -1, 0, key)
        ukey = jnp.where(off + krow <= qpos, key ^ INT_MIN, 0)
        for gi in range(groups_per_chunk):
            base = gi * KEYS_PER_WORD_GROUP
            planes = _bit_transpose32(
                [ukey[base + t * SUBLANES:base + (t + 1) * SUBLANES, :] for t in range(WORD_BITS)])
            row0 = pl.multiple_of((c * groups_per_chunk + gi) * SUBLANES, SUBLANES)
            for b in range(WORD_BITS):
                plane_ref[b, pl.ds(row0, SUBLANES), :] = planes[WORD_BITS - 1 - b]
        return carry

    lax.fori_loop(0, nch, score_body, 0)

    cand = jnp.full((n_words, Q_BLOCK), -1, I32)
    greater = jnp.zeros((n_words, Q_BLOCK), I32)
    cnt_gt = jnp.zeros((1, Q_BLOCK), F32)
    for b in range(WORD_BITS - 1, -1, -1):
        ones = cand & plane_ref[b]
        cnt = _popcount_rows(ones)
        take = cnt_gt + cnt >= topk
        greater = jnp.where(take, greater, greater | ones)
        cnt_gt = jnp.where(take, cnt_gt, cnt_gt + cnt)
        cand = jnp.where(take, ones, cand ^ ones)

    word_row = lax.broadcasted_iota(I32, (n_words, Q_BLOCK), 0)
    word_pos = (word_row >> 3) * KEYS_PER_WORD_GROUP + (word_row & (SUBLANES - 1))
    qcol = i * Q_BLOCK + lax.broadcasted_iota(I32, (n_words, Q_BLOCK), 1)

    def prefix(limit):
        nt = jnp.clip((limit - word_pos + (SUBLANES - 1)) >> 3, 0, WORD_BITS)
        top = lax.shift_right_arithmetic(jnp.full_like(nt, INT_MIN), jnp.maximum(nt, 1) - 1)
        return jnp.where(nt <= 0, 0, top)

    cand = cand & prefix(qcol + 1)
    need = topk - cnt_gt
    sel_ref[...] = greater | cand

    @pl.when(jnp.max(_popcount_rows(cand) - need) > 0)
    def _():
        def jbody(bi, jt):
            cj = jt | jnp.left_shift(jnp.int32(1), idx_bits - 1 - bi)
            return jnp.where(_popcount_rows(cand & prefix(cj)) <= need, cj, jt)
        jt = lax.fori_loop(0, idx_bits, jbody, jnp.zeros((1, Q_BLOCK), I32))
        sel_ref[...] = greater | (cand & prefix(jt))

    def bias_body(c, carry):
        off = pl.multiple_of(c * tk, tk)
        for gi in range(groups_per_chunk):
            row0 = pl.multiple_of((c * groups_per_chunk + gi) * SUBLANES, SUBLANES)
            w = sel_ref[pl.ds(row0, SUBLANES), :]
            for t in range(WORD_BITS):
                dst = pl.multiple_of(off + gi * KEYS_PER_WORD_GROUP + t * SUBLANES, SUBLANES)
                bias_ref[pl.ds(dst, SUBLANES), :] = jnp.where((w << t) < 0, 0.0, -jnp.inf)
        return carry

    lax.fori_loop(0, nch, bias_body, 0)

    n_rep = N_HEADS_A // N_KV_A
    cols = n_rep * Q_BLOCK
    q_all = q_ref[0].reshape(N_HEADS_A * Q_BLOCK, HEAD_DIM)
    scale2 = (HEAD_DIM ** -0.5) * LOG2_E
    m_ref[...] = jnp.full(m_ref.shape, -1e30, F32)
    acc_ref[...] = jnp.zeros(acc_ref.shape, F32)
    ones_rows = jnp.ones((SUM_ROWS, tk), BF16)

    def stage_qk(g, off):
        qg = q_all[g * cols:(g + 1) * cols, :]
        s_ref[g] = _dot_nt(k_ref[0, pl.ds(off, tk), g * HEAD_DIM:(g + 1) * HEAD_DIM], qg)

    def stage_max(g, off):
        for n in range(n_rep):
            csl = slice(n * Q_BLOCK, (n + 1) * Q_BLOCK)
            t = s_ref[g, :, csl] + bias_ref[pl.ds(off, tk), :]
            s_ref[g, :, csl] = t
            m_old = m_ref[g, :, csl]
            m_new = jnp.maximum(m_old, jnp.max(t, axis=0, keepdims=True))
            alpha_ref[g, :, csl] = jnp.exp2((m_old - m_new) * scale2)
            m_ref[g, :, csl] = m_new

    def stage_exp(g, off):
        for n in range(n_rep):
            csl = slice(n * Q_BLOCK, (n + 1) * Q_BLOCK)
            p_ref[g, :, csl] = jnp.exp2((s_ref[g, :, csl] - m_ref[g, 0:1, csl]) * scale2).astype(BF16)

    def stage_pv(g, off):
        v_ext = jnp.concatenate(
            [vt_ref[0, g * HEAD_DIM:(g + 1) * HEAD_DIM, pl.ds(off, tk)], ones_rows], axis=0)
        acc_ref[g] = alpha_ref[g, 0:1, :] * acc_ref[g] + _dot(v_ext, p_ref[g])

    stages = (stage_qk, stage_max, stage_exp, stage_pv)

    def att_body(c, carry):
        off = pl.multiple_of(c * tk, tk)
        for step in range(N_KV_A + len(stages) - 1):
            for g in range(N_KV_A):
                if 0 <= step - g < len(stages):
                    stages[step - g](g, off)
        return carry

    lax.fori_loop(0, nch, att_body, 0)
    for g in range(N_KV_A):
        acc = acc_ref[g]
        o_t = acc[0:HEAD_DIM, :] / acc[HEAD_DIM:HEAD_DIM + 1, :]
        for n in range(n_rep):
            h = g * n_rep + n
            o_ref[:, h * HEAD_DIM:(h + 1) * HEAD_DIM] = jnp.transpose(
                o_t[:, n * Q_BLOCK:(n + 1) * Q_BLOCK]).astype(o_ref.dtype)


def _attention(qh, wi, kkiv, vt, batch, seq_len):
    nb = seq_len // Q_BLOCK
    topk = min(TOPK_MAX, seq_len // 4)
    tk = min(512, seq_len)
    kvd = N_KV_A * HEAD_DIM
    cols = N_HEADS_A // N_KV_A * Q_BLOCK
    single = pl.Buffered(1)
    return pl.pallas_call(
        functools.partial(_attn_kernel, topk=topk, tk=tk, idx_bits=seq_len.bit_length()),
        grid=(batch, nb),
        in_specs=[pl.BlockSpec((1, N_HEADS_A, Q_BLOCK, HEAD_DIM), lambda b, i: (b * nb + i, 0, 0, 0)),
                  pl.BlockSpec((1, N_HEADS_IDX, Q_BLOCK, HEAD_DIM), lambda b, i: (b * nb + i, 1, 0, 0)),
                  pl.BlockSpec((Q_BLOCK, LANES), lambda b, i: (b * nb + i, 0)),
                  pl.BlockSpec((1, seq_len, kvd), lambda b, i: (b, 0, 0), pipeline_mode=single),
                  pl.BlockSpec((1, seq_len, HEAD_DIM), lambda b, i: (b, 0, kvd // HEAD_DIM),
                               pipeline_mode=single),
                  pl.BlockSpec((1, kvd, seq_len), lambda b, i: (b, 0, 0), pipeline_mode=single)],
        out_specs=pl.BlockSpec((Q_BLOCK, N_HEADS_A * HEAD_DIM), lambda b, i: (b * nb + i, 0)),
        out_shape=jax.ShapeDtypeStruct((batch * seq_len, N_HEADS_A * HEAD_DIM), BF16),
        scratch_shapes=[pltpu.VMEM((WORD_BITS, seq_len // WORD_BITS, Q_BLOCK), I32),
                        pltpu.VMEM((seq_len // WORD_BITS, Q_BLOCK), I32),
                        pltpu.VMEM((seq_len, Q_BLOCK), F32),
                        pltpu.VMEM((N_KV_A, SUBLANES, cols), F32),
                        pltpu.VMEM((N_KV_A, HEAD_DIM + SUM_ROWS, cols), F32),
                        pltpu.VMEM((N_KV_A, SUBLANES, cols), F32),
                        pltpu.VMEM((N_KV_A, tk, cols), F32),
                        pltpu.VMEM((N_KV_A, tk, cols), BF16)],
        compiler_params=_cparams(("arbitrary", "arbitrary"), 52),
        name="dsa_attention",
    )(qh, qh, wi, kkiv, kkiv, vt)


def _dot_hp(x, w_bf):
    hi = x.astype(BF16)
    r1 = x - hi.astype(F32)
    mid = r1.astype(BF16)
    lo = (r1 - mid.astype(F32)).astype(BF16)
    n = x.shape[0]
    parts = _dot(jnp.concatenate([hi, mid, lo], axis=0), w_bf)
    return parts[0:n] + parts[n:2 * n] + parts[2 * n:3 * n]


def _rwkv_kernel(r_ref, k_ref, v_ref, lw_ref, a_ref, g_ref, mu_ref, kk_ref, ka_ref, rk_ref,
                 lnw_ref, lnb_ref, bd_ref, ones_ref, ltri_ref, ts_ref, ti_ref, ic_ref, eye_ref,
                 o_ref, st_ref, prev_ref):
    @pl.when(pl.program_id(2) == 0)
    def _():
        st_ref[...] = jnp.zeros_like(st_ref)
        prev_ref[...] = jnp.zeros_like(prev_ref)

    tiles = [_rwkv_tile(tile, r_ref, k_ref, v_ref, lw_ref, a_ref, g_ref, mu_ref, kk_ref, ka_ref, rk_ref,
                        lnw_ref, lnb_ref, bd_ref, ones_ref, ltri_ref, ts_ref, ti_ref, ic_ref, eye_ref,
                        o_ref, st_ref, prev_ref)
             for tile in range(r_ref.shape[1] // TILE_B)]
    while tiles:
        tiles = [t for t in tiles if next(t, "done") != "done"]


def _rwkv_tile(tile, r_ref, k_ref, v_ref, lw_ref, a_ref, g_ref, mu_ref, kk_ref, ka_ref, rk_ref,
               lnw_ref, lnb_ref, bd_ref, ones_ref, ltri_ref, ts_ref, ti_ref, ic_ref, eye_ref,
               o_ref, st_ref, prev_ref):
    cols = slice(tile * TILE_B, (tile + 1) * TILE_B)
    bdm = bd_ref[...]
    row = lax.broadcasted_iota(I32, (CHUNK, TILE_B), 0)

    def shifted(x, slot):
        prev = prev_ref[slot:slot + 1, cols]
        prev_ref[slot:slot + 1, cols] = x[CHUNK - 1:CHUNK, :]
        return jnp.where(row == 0, prev, pltpu.roll(x, 1, axis=0))

    r0 = r_ref[:, cols]
    k0 = k_ref[:, cols]
    v0 = v_ref[:, cols]
    r = r0 + (shifted(r0, 0) - r0) * mu_ref[0:1, cols]
    k = k0 + (shifted(k0, 1) - k0) * mu_ref[1:2, cols]
    v = v0 + (shifted(v0, 2) - v0) * mu_ref[2:3, cols]
    a = a_ref[:, cols]
    ones_bd = ones_ref[...]

    kk = k * kk_ref[:, cols]
    k2 = k * (1.0 + (a - 1.0) * ka_ref[:, cols])
    head_sums = _dot_hp(jnp.concatenate([kk * kk, r * k2 * rk_ref[:, cols]], axis=0), ones_bd)
    ss = head_sums[0:CHUNK]
    bonus = head_sums[CHUNK:2 * CHUNK] * v
    yield
    kk = kk / jnp.maximum(jnp.sqrt(ss), 1e-12)
    aa = -kk
    bb = kk * a

    lw = lw_ref[:, cols]
    cs = _cumsum_rows(lw, ltri_ref[...])
    yield
    tot = cs[CHUNK - 1:CHUNK, :]
    e_in = jnp.exp(cs)
    e_out = jnp.exp(-cs)
    e_tail = jnp.exp(tot - cs)
    at = aa * jnp.exp(cs - lw)
    rt = r * e_in
    bt = (bb * e_out)
    kt = (k2 * e_out)
    bh = (bb * e_tail).astype(BF16)
    kh = (k2 * e_tail).astype(BF16)
    wc = jnp.exp(tot)

    def bd(x):
        return jnp.concatenate([x.astype(BF16)] * HEADS_PER_TILE, axis=0) * ones_bd

    lhs = jnp.concatenate([at, rt], axis=0).astype(BF16)
    mb = _dot_nt(lhs, bd(bt))
    mk = _dot_nt(lhs, bd(kt))
    yield
    ts = ts_ref[...]
    ti = ti_ref[...]
    m_ab = mb[0:CHUNK] * ts
    m_rb = (mb[CHUNK:2 * CHUNK] * ti).astype(BF16)
    m_ak = (mk[0:CHUNK] * ts).astype(BF16)
    m_rk = (mk[CHUNK:2 * CHUNK] * ti).astype(BF16)

    n_round = CHUNK.bit_length() - 1
    t_inv = ic_ref[...] + m_ab
    m_pow = _dot(m_ab.astype(BF16), bd(m_ab))
    yield
    for rnd in range(1, n_round):
        last = rnd == n_round - 1
        lhs_rows = [t_inv] if last else [t_inv, m_pow]
        prod = _dot(jnp.concatenate(lhs_rows, axis=0).astype(BF16), bd(m_pow))
        yield
        t_inv = t_inv + prod[0:CHUNK]
        if not last:
            m_pow = prod[CHUNK:2 * CHUNK]
    t_bf = t_inv.astype(BF16)

    p = _dot(t_bf, bd(at))
    uv = _dot(jnp.concatenate([m_ak, m_rk], axis=0), bd(v))
    u = uv[0:CHUNK]
    yield
    q = _dot(t_bf, bd(u))
    rp = rt + _dot(m_rb, bd(p))
    yield
    y0 = _dot(m_rb, bd(q)) + uv[CHUNK:2 * CHUNK]

    yield
    a_t = _dot_tn(bh, p.astype(BF16)) * bdm + eye_ref[...] * wc
    d_t = _dot_tn(jnp.concatenate([bh, kh], axis=0),
                  jnp.concatenate([q.astype(BF16), v.astype(BF16)], axis=0)) * bdm
    st = st_ref[tile]
    from_state = _dot(jnp.concatenate([rp.astype(BF16), a_t.astype(BF16)], axis=0), st.astype(BF16))
    y = from_state[0:CHUNK] + y0
    st_ref[tile] = from_state[CHUNK:CHUNK + TILE_B] + d_t

    yield
    inv_n = 1.0 / HEAD_DIM_B
    mu = _dot_hp(y, ones_bd) * inv_n
    yield
    yc = y - mu
    var = _dot_hp(yc * yc, ones_bd) * inv_n
    yield
    yn =(yc * lax.rsqrt(var + GN_EPS)) * lnw_ref[:, cols] + lnb_ref[:, cols]
    o_ref[:, cols] = ((yn + bonus) * g_ref[:, cols]).astype(o_ref.dtype)


def _cumsum_rows(x, ltri_bf):
    hi = x.astype(BF16)
    r1 = x - hi.astype(F32)
    mid = r1.astype(BF16)
    lo = (r1 - mid.astype(F32)).astype(BF16)
    return _dot(ltri_bf, hi) + _dot(ltri_bf, mid) + _dot(ltri_bf, lo)


def _rwkv_consts():
    idx = jnp.arange(TILE_B)
    head = idx // HEAD_DIM_B
    bdm = (head[:, None] == head[None, :])
    t = jnp.arange(CHUNK)[:, None]
    s = (idx % CHUNK)[None, :]
    return dict(
        bd=bdm.astype(F32),
        ones=bdm.astype(BF16),
        ltri=(jnp.arange(CHUNK)[None, :] <= t).astype(BF16),
        ts=(s < t).astype(F32),
        ti=(s <= t).astype(F32),
        ic=(s == t).astype(F32),
        eye=jnp.eye(TILE_B, dtype=F32),
    )


def _rwkv(rkv, lw, a, g, mu_rkv, k_k, k_a, r_k, ln_w, ln_b, batch, seq_len):
    m, db = lw.shape
    nc = seq_len // CHUNK
    width = RWKV_TILES_PER_STEP * TILE_B
    ng = db // width
    cst = _rwkv_consts()
    blk = lambda off: pl.BlockSpec((CHUNK, width), lambda b, h, c, off=off: (b * nc + c, off + h))
    par = lambda rows: pl.BlockSpec((rows, width), lambda b, h, c: (0, h))
    full = lambda arr: pl.BlockSpec(arr.shape, lambda b, h, c: (0, 0))
    row = lambda p: p.reshape(1, db)
    return pl.pallas_call(
        _rwkv_kernel,
        grid=(batch, ng, nc),
        in_specs=[blk(0), blk(ng), blk(2 * ng), blk(0), blk(0), blk(0),
                  par(3), par(1), par(1), par(1), par(1), par(1),
                  full(cst["bd"]), full(cst["ones"]), full(cst["ltri"]), full(cst["ts"]),
                  full(cst["ti"]), full(cst["ic"]), full(cst["eye"])],
        out_specs=blk(0),
        out_shape=jax.ShapeDtypeStruct((m, db), BF16),
        scratch_shapes=[pltpu.VMEM((RWKV_TILES_PER_STEP, TILE_B, TILE_B), F32),
                        pltpu.VMEM((SUBLANES, width), F32)],
        compiler_params=_cparams(("arbitrary", "arbitrary", "arbitrary"), 32),
        name="rwkv7",
    )(rkv, rkv, rkv, lw, a, g, mu_rkv, row(k_k), row(k_a), row(r_k), row(ln_w), row(ln_b),
      cst["bd"], cst["ones"], cst["ltri"], cst["ts"], cst["ti"], cst["ic"], cst["eye"])


def _merge_kernel(xn_ref, oa_ref, ob_ref, wga_ref, wgb_ref, bga_ref, bgb_ref, wpa_ref, wpb_ref, o_ref):
    xn = xn_ref[...]
    ga = _sigmoid(_dot(xn, wga_ref[...]) + bga_ref[...])
    gb = _sigmoid(_dot(xn, wgb_ref[...]) + bgb_ref[...])
    ya = _dot(oa_ref[...], wpa_ref[...])
    yb = _dot(ob_ref[...], wpb_ref[...])
    o_ref[...] = (ga * ya + gb * yb).astype(o_ref.dtype)


def _merge(xn, oa, ob, w_gate, b_gate, w_pa, w_pb, bm=1024, bn=256, side=None):
    m, d = xn.shape
    bm = min(bm, m)
    nj = d // bn
    return _matmul_call(
        _merge_kernel, (m // bm, nj),
        [pl.BlockSpec((bm, d), lambda i, j: (i, 0)),
         pl.BlockSpec((bm, oa.shape[1]), lambda i, j: (i, 0)),
         pl.BlockSpec((bm, ob.shape[1]), lambda i, j: (i, 0)),
         pl.BlockSpec((d, bn), lambda i, j: (0, j)),
         pl.BlockSpec((d, bn), lambda i, j: (0, j + nj)),
         pl.BlockSpec((1, bn), lambda i, j: (0, j)),
         pl.BlockSpec((1, bn), lambda i, j: (0, j + nj)),
         pl.BlockSpec((w_pa.shape[0], bn), lambda i, j: (0, j)),
         pl.BlockSpec((w_pb.shape[0], bn), lambda i, j: (0, j))],
        pl.BlockSpec((bm, bn), lambda i, j: (i, j)),
        jax.ShapeDtypeStruct((m, d), BF16),
        (xn, oa, ob, w_gate, w_gate, b_gate, b_gate, w_pa, w_pb), "gated_merge", vmem_mib=56, side=side)


def _mm_res_kernel(x_ref, w_ref, res_ref, o_ref):
    o_ref[...] = res_ref[...] + _dot(x_ref[...], w_ref[...])


def _matmul_residual(x, w, res, bm=1024, bn=1024, name="matmul_res", side=None):
    m, k = x.shape
    n = w.shape[1]
    bm = min(bm, m)
    return _matmul_call(
        _mm_res_kernel, (m // bm, n // bn),
        [pl.BlockSpec((bm, k), lambda i, j: (i, 0)),
         pl.BlockSpec((k, bn), lambda i, j: (0, j)),
         pl.BlockSpec((bm, bn), lambda i, j: (i, j))],
        pl.BlockSpec((bm, bn), lambda i, j: (i, j)),
        jax.ShapeDtypeStruct((m, n), F32), (x, w, res), name, side=side)


def _ffn_up_kernel(x_ref, w1_ref, w3_ref, o_ref):
    x = x_ref[...]
    h1 = _dot(x, w1_ref[...])
    h3 = _dot(x, w3_ref[...])
    o_ref[...] = (h1 * _sigmoid(h1) * h3).astype(o_ref.dtype)


def _ffn_up(x, w1, w3, bm=1024, bn=512, side=None):
    m, k = x.shape
    n = w1.shape[1]
    bm = min(bm, m)
    return _matmul_call(
        _ffn_up_kernel, (m // bm, pl.cdiv(n, bn)),
        [pl.BlockSpec((bm, k), lambda i, j: (i, 0)),
         pl.BlockSpec((k, bn), lambda i, j: (0, j)),
         pl.BlockSpec((k, bn), lambda i, j: (0, j))],
        pl.BlockSpec((bm, bn), lambda i, j: (i, j)),
        jax.ShapeDtypeStruct((m, n), BF16), (x, w1, w3), "ffn_up", side=side)


def _ple_kernel(h_ref, wg_ref, p_ref, wp_ref, gain_ref, o_ref, hn_ref, *, bn, final_norm):
    j = pl.program_id(1)
    n_row_chunks = h_ref.shape[0] // NORM_ROW_CHUNK

    def row_chunk(r):
        return pl.ds(pl.multiple_of(r * NORM_ROW_CHUNK, NORM_ROW_CHUNK), NORM_ROW_CHUNK)

    @pl.when(j == 0)
    def _():
        def body(r, carry):
            hn_ref[row_chunk(r), :] = _rms_rows(h_ref[row_chunk(r), :]).astype(BF16)
            return carry
        lax.fori_loop(0, n_row_chunks, body, 0)

    col = pl.multiple_of(j * bn, bn)
    gate = _sigmoid(_dot(hn_ref[...], wg_ref[...]))
    o_ref[:, pl.ds(col, bn)] = h_ref[:, pl.ds(col, bn)] + gate * _dot(p_ref[...], wp_ref[...])

    if final_norm:
        @pl.when(j == pl.num_programs(1) - 1)
        def _():
            def body(r, carry):
                o_ref[row_chunk(r), :] = _rms_rows(o_ref[row_chunk(r), :]) * gain_ref[...]
                return carry
            lax.fori_loop(0, n_row_chunks, body, 0)


def _ple(h, wg, p, wp, final_gain, bm=512, bn=512):
    m, d = h.shape
    bm = min(bm, m)
    final_norm = final_gain is not None
    gain = (final_gain if final_norm else jnp.ones((d,), F32)).reshape(1, d)
    return pl.pallas_call(
        functools.partial(_ple_kernel, bn=bn, final_norm=final_norm),
        grid=(m // bm, d // bn),
        in_specs=[pl.BlockSpec((bm, d), lambda i, j: (i, 0)),
                  pl.BlockSpec((d, bn), lambda i, j: (0, j)),
                  pl.BlockSpec((bm, p.shape[1]), lambda i, j: (i, 0)),
                  pl.BlockSpec((p.shape[1], bn), lambda i, j: (0, j)),
                  pl.BlockSpec((1, d), lambda i, j: (0, 0))],
        out_specs=pl.BlockSpec((bm, d), lambda i, j: (i, 0)),
        out_shape=jax.ShapeDtypeStruct((m, d), F32),
        scratch_shapes=[pltpu.VMEM((bm, d), BF16)],
        compiler_params=_cparams(("arbitrary", "arbitrary"), 52),
        name="ple",
    )(h, wg, p, wp, gain)


def _rmsnorm_kernel(x_ref, g_ref, o_ref):
    o_ref[...] = (_rms_rows(x_ref[...]) * g_ref[...]).astype(o_ref.dtype)


def _rmsnorm(x, gain, out_dtype, bm=512):
    m, d = x.shape
    bm = min(bm, m)
    row_spec = pl.BlockSpec((bm, d), lambda i: (i, 0))
    return pl.pallas_call(
        _rmsnorm_kernel,
        grid=(m // bm,),
        in_specs=[row_spec, pl.BlockSpec((1, d), lambda i: (0, 0))],
        out_specs=row_spec,
        out_shape=jax.ShapeDtypeStruct((m, d), out_dtype),
        compiler_params=_cparams(("arbitrary",), 48),
        name="rmsnorm",
    )(x, gain.reshape(1, d))


def _pad_to(a, axis, size):
    pad = [(0, 0)] * a.ndim
    pad[axis] = (0, size - a.shape[axis])
    return jnp.pad(a, pad)


def _layer(h2, p2, cos, sin, batch, seq_len, norm_mix, w_in, mu_rkv, mu_wag, w0, w1, w2, a0, a1, a2,
           g1, g2, k_k, k_a, r_k, ln_w, ln_b, w_pa, w_pb, w_gate, b_gate, w_o, norm_ffn,
           w_ffn1, w_ffn3, w_ffn2, w_ple_gate, w_ple, final_gain):
    d = h2.shape[1]
    d_a = N_HEADS_A * HEAD_DIM
    kvd = N_KV_A * HEAD_DIM
    d_iq = N_HEADS_IDX * HEAD_DIM
    d_b = w_pb.shape[0]
    o_q, o_k, o_v = 0, d_a, d_a + kvd
    o_qi = o_v + kvd
    o_ki = o_qi + d_iq
    o_wi = o_ki + HEAD_DIM
    o_r = o_wi + N_HEADS_IDX

    bf = lambda a: a.astype(BF16)
    w_in_bf = bf(w_in)
    assert d_a == d_iq and o_wi % LANES == 0
    w_rkv = w_in_bf[:, o_r:o_r + 3 * d_b]
    lora = LANES
    w1p, a1p = bf(_pad_to(w1, 1, lora)), bf(_pad_to(a1, 1, lora))
    w2p, a2p = bf(_pad_to(w2, 0, lora)), bf(_pad_to(a2, 0, lora))

    xn, lw, a, g, wi = _prep(h2, norm_mix.reshape(1, d), mu_wag, w1p, w2p, w0.reshape(1, d_b),
                             a1p, a2p, a0.reshape(1, d_b), bf(g1), bf(g2), w_in_bf, o_wi // LANES,
                             seq_len)

    qh, w_o_bf = _proj_rope_headmajor(xn, w_in_bf, (o_q, o_qi), d_a, cos, sin, side=w_o)
    kkiv = _proj_kv(xn, w_in_bf, o_k, o_ki, o_v, cos, sin)
    rkv, w_gate_bf = _matmul(xn, w_rkv, F32, bn=512, name="proj_rkv", side=w_gate)

    kkiv = kkiv.reshape(batch, seq_len, -1)
    vt = jnp.swapaxes(kkiv[:, :, kvd + HEAD_DIM:], 1, 2)
    o_att = _attention(qh, wi, kkiv, vt, batch, seq_len)

    o_rwkv = _rwkv(rkv, lw, a, g, mu_rkv, k_k, k_a, r_k.reshape(-1), ln_w, ln_b, batch, seq_len)

    mixed, w_ffn1_bf = _merge(xn, o_att, o_rwkv, w_gate_bf, b_gate.reshape(1, -1), bf(w_pa), bf(w_pb),
                              side=w_ffn1)
    h2, w_ffn3_bf = _matmul_residual(mixed, w_o_bf, h2, bn=512, name="out_proj", side=w_ffn3)

    xf = _rmsnorm(h2, norm_ffn, BF16)
    u, w_ffn2_bf = _ffn_up(xf, w_ffn1_bf, w_ffn3_bf, side=w_ffn2)
    h2, w_ple_gate_bf = _matmul_residual(u, w_ffn2_bf, h2, bm=512, bn=512, name="ffn_down",
                                         side=w_ple_gate)

    return _ple(h2, w_ple_gate_bf, bf(p2), bf(w_ple), final_gain)


def kernel(x, p, positions, norm_mix, w_in, mu_rkv, mu_wag, w0, w1, w2, a0, a1, a2, g1, g2, k_k, k_a,
           r_k, ln_w, ln_b, w_pa, w_pb, w_gate, b_gate, w_o, norm_ffn, w_ffn1, w_ffn3, w_ffn2,
           w_ple_gate, w_ple, norm_final):
    batch, seq_len, d = x.shape
    depth = p.shape[0]
    h2 = x.reshape(batch * seq_len, d)
    cos, sin = _rope_tables(positions)
    for i in range(depth):
        h2 = _layer(h2, p[i].reshape(batch * seq_len, -1), cos, sin, batch, seq_len,
                    norm_mix[i], w_in[i], mu_rkv[i], mu_wag[i], w0[i], w1[i], w2[i], a0[i], a1[i], a2[i],
                    g1[i], g2[i], k_k[i], k_a[i], r_k[i], ln_w[i], ln_b[i], w_pa[i], w_pb[i], w_gate[i],
                    b_gate[i], w_o[i], norm_ffn[i], w_ffn1[i], w_ffn3[i], w_ffn2[i], w_ple_gate[i],
                    w_ple[i], norm_final if i == depth - 1 else None)
    return h2.reshape(batch, seq_len, d)
```

```python
import functools

import jax
import jax.numpy as jnp
from jax import lax
from jax.experimental import pallas as pl
from jax.experimental.pallas import tpu as pltpu

F32 = jnp.float32
BF16 = jnp.bfloat16
I32 = jnp.int32

N_HEADS_A = 16
HEAD_DIM = 128
N_KV_A = 4
N_HEADS_IDX = 16
TOPK_MAX = 256
Q_BLOCK = 128
ROPE_THETA = 10000.0
HEAD_DIM_B = 64
GN_EPS = 64e-5
RMS_EPS = 1e-6

LANES = 128
SUBLANES = 8
BF16_SUBLANES = 16
MXU_DIM = 256

INT_MIN = -2 ** 31
LOG2_E = 1.4426950408889634
NORM_ROW_CHUNK = 64
DECAY_SCALE = 0.6065306597126334

CHUNK = 64
HEADS_PER_TILE = MXU_DIM // HEAD_DIM_B
TILE_B = HEADS_PER_TILE * HEAD_DIM_B
RWKV_TILES_PER_STEP = 4


def _cparams(sem, vmem_mib):
    return pltpu.CompilerParams(dimension_semantics=sem, vmem_limit_bytes=vmem_mib << 20)


def _sidecar_cast(body, n_in):
    def kern(*refs):
        refs[n_in + 2][...] = refs[n_in][...].astype(BF16)
        body(*refs[:n_in], refs[n_in + 1])
    return kern


def _matmul_call(body, grid, in_specs, out_spec, out_shape, args, name, vmem_mib=52, side=None):
    params = _cparams(("arbitrary", "arbitrary"), vmem_mib)
    if side is None:
        out = pl.pallas_call(body, grid=grid, in_specs=in_specs, out_specs=out_spec,
                             out_shape=out_shape, compiler_params=params, name=name)(*args)
        return out, None
    steps, n_inner = grid[0] * grid[1], grid[1]
    rows = side.shape[0]
    rb = -(-(-(-rows // steps)) // BF16_SUBLANES) * BF16_SUBLANES
    last = -(-rows // rb) - 1
    side_spec = pl.BlockSpec((rb, side.shape[1]), lambda i, j: (jnp.minimum(i * n_inner + j, last), 0))
    return pl.pallas_call(
        _sidecar_cast(body, len(in_specs)), grid=grid, in_specs=[*in_specs, side_spec],
        out_specs=[out_spec, side_spec],
        out_shape=[out_shape, jax.ShapeDtypeStruct(side.shape, BF16)],
        compiler_params=params, name=name)(*args, side)


def _dot(a, b):
    return jnp.dot(a, b, preferred_element_type=F32)


def _dot_nt(a, b):
    return lax.dot_general(a, b, (((1,), (1,)), ((), ())), preferred_element_type=F32)


def _dot_tn(a, b):
    return lax.dot_general(a, b, (((0,), (0,)), ((), ())), preferred_element_type=F32)


def _sigmoid(x):
    return 1.0 / (1.0 + jnp.exp(-x))


def _rms_rows(x):
    return x * lax.rsqrt(jnp.mean(x * x, axis=-1, keepdims=True) + RMS_EPS)


def _rope_tab_kernel(pos_ref, freq_ref, sign_ref, cos_ref, sin_ref):
    ang = pos_ref[...].astype(F32) * freq_ref[...]
    cos_ref[...] = jnp.cos(ang)
    sin_ref[...] = jnp.sin(ang) * sign_ref[...]


def _rope_tables(positions):
    n = positions.size
    half = HEAD_DIM // 2
    inv_freq = ROPE_THETA ** (-jnp.arange(0, HEAD_DIM, 2, dtype=F32) / HEAD_DIM)
    freq2 = jnp.concatenate([inv_freq, inv_freq]).reshape(1, HEAD_DIM)
    sign = jnp.concatenate([-jnp.ones((half,), F32), jnp.ones((half,), F32)]).reshape(1, HEAD_DIM)
    bm = min(2048, n)
    return pl.pallas_call(
        _rope_tab_kernel,
        grid=(n // bm,),
        in_specs=[pl.BlockSpec((bm, 1), lambda i: (i, 0)),
                  pl.BlockSpec((1, HEAD_DIM), lambda i: (0, 0)),
                  pl.BlockSpec((1, HEAD_DIM), lambda i: (0, 0))],
        out_specs=[pl.BlockSpec((bm, HEAD_DIM), lambda i: (i, 0)),
                   pl.BlockSpec((bm, HEAD_DIM), lambda i: (i, 0))],
        out_shape=[jax.ShapeDtypeStruct((n, HEAD_DIM), F32)] * 2,
        compiler_params=_cparams(("arbitrary",), 32),
        name="rope_tables",
    )(positions.reshape(n, 1), freq2, sign)


def _rope(t, cos, sin):
    return t * cos + pltpu.roll(t, HEAD_DIM // 2, axis=1) * sin


def _prep_kernel(x_ref, xp_ref, gain_ref, mu_ref, w1_ref, w2_ref, w0_ref, a1_ref, a2_ref, a0_ref,
                 g1_ref, g2_ref, wwi_ref, win_ref,
                 xn_ref, lw_ref, a_ref, g_ref, wi_ref, winbf_ref, *, seq_len, bm):
    i = pl.program_id(0)
    winbf_ref[...] = win_ref[...].astype(BF16)
    gain = gain_ref[...]
    xn = _rms_rows(x_ref[...]) * gain
    prev = (_rms_rows(xp_ref[...]) * gain)[SUBLANES - 1:SUBLANES, :]
    prev = jnp.where((i * bm) % seq_len == 0, jnp.zeros_like(prev), prev)
    row = lax.broadcasted_iota(I32, xn.shape, 0)
    sh = jnp.where(row == 0, prev, pltpu.roll(xn, 1, axis=0))
    xx = sh - xn
    xn_bf = xn.astype(BF16)
    xn_ref[...] = xn_bf
    wi_ref[...] = _dot(xn_bf, wwi_ref[...].astype(BF16)) * (N_HEADS_IDX ** -0.5)

    xw = (xn + xx * mu_ref[0:1, :]).astype(BF16)
    hw = jnp.tanh(_dot(xw, w1_ref[...])).astype(BF16)
    wl = w0_ref[...] + _dot(hw, w2_ref[...])
    lw_ref[...] = -DECAY_SCALE * _sigmoid(wl)

    xa = (xn + xx * mu_ref[1:2, :]).astype(BF16)
    ha = _dot(xa, a1_ref[...]).astype(BF16)
    a_ref[...] = _sigmoid(a0_ref[...] + _dot(ha, a2_ref[...]))

    xg = (xn + xx * mu_ref[2:3, :]).astype(BF16)
    hg = _sigmoid(_dot(xg, g1_ref[...])).astype(BF16)
    g_ref[...] = _dot(hg, g2_ref[...])


def _prep(x2, gain, mu_wag, w1, w2, w0, a1, a2, a0, g1, g2, w_in, wi_block, seq_len):
    m, d = x2.shape
    db = w2.shape[1]
    bm = min(128, m)
    full = lambda a: pl.BlockSpec(a.shape, lambda i: (0,) * a.ndim)
    nsub = bm // SUBLANES
    steps = m // bm
    rb = -(-(-(-w_in.shape[0] // steps)) // BF16_SUBLANES) * BF16_SUBLANES
    last = -(-w_in.shape[0] // rb) - 1
    w_rows = pl.BlockSpec((rb, w_in.shape[1]), lambda i: (jnp.minimum(i, last), 0))
    return pl.pallas_call(
        functools.partial(_prep_kernel, seq_len=seq_len, bm=bm),
        grid=(m // bm,),
        in_specs=[pl.BlockSpec((bm, d), lambda i: (i, 0)),
                  pl.BlockSpec((SUBLANES, d), lambda i: (jnp.maximum(i * nsub - 1, 0), 0)),
                  full(gain), full(mu_wag), full(w1), full(w2), full(w0), full(a1), full(a2), full(a0),
                  full(g1), full(g2), pl.BlockSpec((d, LANES), lambda i: (0, wi_block)), w_rows],
        out_specs=[pl.BlockSpec((bm, d), lambda i: (i, 0)),
                   pl.BlockSpec((bm, db), lambda i: (i, 0)),
                   pl.BlockSpec((bm, db), lambda i: (i, 0)),
                   pl.BlockSpec((bm, db), lambda i: (i, 0)),
                   pl.BlockSpec((bm, LANES), lambda i: (i, 0)),
                   w_rows],
        out_shape=[jax.ShapeDtypeStruct((m, d), BF16),
                   jax.ShapeDtypeStruct((m, db), F32),
                   jax.ShapeDtypeStruct((m, db), F32),
                   jax.ShapeDtypeStruct((m, db), F32),
                   jax.ShapeDtypeStruct((m, LANES), F32),
                   jax.ShapeDtypeStruct(w_in.shape, BF16)],
        compiler_params=_cparams(("arbitrary",), 48),
        name="prep",
    )(x2, x2, gain, mu_wag, w1, w2, w0, a1, a2, a0, g1, g2, w_in, w_in)


def _proj_rope_hm_kernel(x_ref, w_ref, cos_ref, sin_ref, o_ref):
    acc = _dot(x_ref[...], w_ref[...])
    cos = cos_ref[...]
    sin = sin_ref[...]
    bm, bn = acc.shape
    for j in range(bn // HEAD_DIM):
        t = _rope(acc[:, j * HEAD_DIM:(j + 1) * HEAD_DIM], cos, sin).astype(o_ref.dtype)
        for r in range(bm // Q_BLOCK):
            o_ref[r, j] = t[r * Q_BLOCK:(r + 1) * Q_BLOCK, :]


def _proj_rope_headmajor(xn, w, col_starts, width, cos, sin, bm=1024, bn=1024, side=None):
    m, k = xn.shape
    n = width * len(col_starts)
    bm = min(bm, m)
    per = width // bn
    first, second = (c // bn for c in col_starts)

    def w_block(i, j):
        return 0, jnp.where(j < per, first + j, second + j - per)

    return _matmul_call(
        _proj_rope_hm_kernel, (m // bm, n // bn),
        [pl.BlockSpec((bm, k), lambda i, j: (i, 0)),
         pl.BlockSpec((k, bn), w_block),
         pl.BlockSpec((bm, HEAD_DIM), lambda i, j: (i, 0)),
         pl.BlockSpec((bm, HEAD_DIM), lambda i, j: (i, 0))],
        pl.BlockSpec((bm // Q_BLOCK, bn // HEAD_DIM, Q_BLOCK, HEAD_DIM), lambda i, j: (i, j, 0, 0)),
        jax.ShapeDtypeStruct((m // Q_BLOCK, n // HEAD_DIM, Q_BLOCK, HEAD_DIM), BF16),
        (xn, w, cos, sin), "proj_q", side=side)


def _proj_kv_kernel(x_ref, wk_ref, wki_ref, wv_ref, cos_ref, sin_ref, o_ref):
    x = x_ref[...]
    cos = cos_ref[...]
    sin = sin_ref[...]
    col = 0
    for w_ref, rotary in ((wk_ref, True), (wki_ref, True), (wv_ref, False)):
        acc = _dot(x, w_ref[...])
        for j in range(acc.shape[1] // HEAD_DIM):
            t = acc[:, j * HEAD_DIM:(j + 1) * HEAD_DIM]
            if rotary:
                t = _rope(t, cos, sin)
            o_ref[:, col:col + HEAD_DIM] = t.astype(o_ref.dtype)
            col += HEAD_DIM


def _proj_kv(xn, w, col_k, col_ki, col_v, cos, sin, bm=1024):
    m, k = xn.shape
    kvd = N_KV_A * HEAD_DIM
    n = 2 * kvd + HEAD_DIM
    bm = min(bm, m)
    return pl.pallas_call(
        _proj_kv_kernel,
        grid=(m // bm,),
        in_specs=[pl.BlockSpec((bm, k), lambda i: (i, 0)),
                  pl.BlockSpec((k, kvd), lambda i: (0, col_k // kvd)),
                  pl.BlockSpec((k, HEAD_DIM), lambda i: (0, col_ki // HEAD_DIM)),
                  pl.BlockSpec((k, kvd), lambda i: (0, col_v // kvd)),
                  pl.BlockSpec((bm, HEAD_DIM), lambda i: (i, 0)),
                  pl.BlockSpec((bm, HEAD_DIM), lambda i: (i, 0))],
        out_specs=pl.BlockSpec((bm, n), lambda i: (i, 0)),
        out_shape=jax.ShapeDtypeStruct((m, n), BF16),
        compiler_params=_cparams(("arbitrary",), 52),
        name="proj_kv",
    )(xn, w, w, w, cos, sin)


def _mm_kernel(x_ref, w_ref, o_ref):
    o_ref[...] = _dot(x_ref[...], w_ref[...]).astype(o_ref.dtype)


def _matmul(x, w, out_dtype, bm=1024, bn=1024, name="matmul", side=None):
    m, k = x.shape
    n = w.shape[1]
    bm = min(bm, m)
    return _matmul_call(
        _mm_kernel, (m // bm, n // bn),
        [pl.BlockSpec((bm, k), lambda i, j: (i, 0)),
         pl.BlockSpec((k, bn), lambda i, j: (0, j))],
        pl.BlockSpec((bm, bn), lambda i, j: (i, j)),
        jax.ShapeDtypeStruct((m, n), out_dtype), (x, w), name, side=side)


WORD_BITS = 32
SUM_ROWS = 16
KEYS_PER_WORD_GROUP = WORD_BITS * SUBLANES


def _bit_transpose32(words):
    a = list(words)
    j, m = 16, 0x0000FFFF
    while j:
        mask = jnp.int32(m - (1 << 32) if m >= (1 << 31) else m)
        k = 0
        while k < WORD_BITS:
            t = (a[k] ^ lax.shift_right_logical(a[k + j], jnp.full_like(a[k], j))) & mask
            a[k] = a[k] ^ t
            a[k + j] = a[k + j] ^ (t << j)
            k = (k + j + 1) & ~j
        j >>= 1
        m = (m ^ (m << j)) & 0xFFFFFFFF
    return a


def _popcount_rows(words):
    per_sublane = jnp.sum(lax.population_count(words).reshape(-1, SUBLANES, words.shape[1]), axis=0)
    return jnp.sum(per_sublane.astype(F32), axis=0, keepdims=True)


def _attn_kernel(q_ref, qi_ref, wi_ref, k_ref, ki_ref, vt_ref, o_ref,
                 plane_ref, sel_ref, bias_ref, m_ref, acc_ref, alpha_ref, s_ref, p_ref,
                 *, topk, tk, idx_bits):
    i = pl.program_id(1)
    nk = (i + 1) * Q_BLOCK
    nch = (nk + tk - 1) // tk
    n_words = plane_ref.shape[1]
    groups_per_chunk = tk // KEYS_PER_WORD_GROUP

    @pl.when(i == 0)
    def _():
        plane_ref[...] = jnp.zeros_like(plane_ref)

    qi = qi_ref[0].reshape(N_HEADS_IDX * Q_BLOCK, HEAD_DIM)
    wi_t = jnp.transpose(wi_ref[...]) * (HEAD_DIM ** -0.5)
    qpos = i * Q_BLOCK + lax.broadcasted_iota(I32, (tk, Q_BLOCK), 1)
    krow = lax.broadcasted_iota(I32, (tk, Q_BLOCK), 0)

    def score_body(c, carry):
        off = pl.multiple_of(c * tk, tk)
        lg = _dot_nt(ki_ref[0, pl.ds(off, tk), :], qi)
        sc = jnp.zeros((tk, Q_BLOCK), F32)
        for h in range(N_HEADS_IDX):
            sc = sc + wi_t[h:h + 1, :] * jnp.maximum(lg[:, h * Q_BLOCK:(h + 1) * Q_BLOCK], 0.0)
        bits = pltpu.bitcast(sc, I32)
        key = bits ^ ((bits >> 31) & 0x7FFFFFFF)
        key = jnp.where(key == -1, 0, key)
        ukey = jnp.where(off + krow <= qpos, key ^ INT_MIN, 0)
        for gi in range(groups_per_chunk):
            base = gi * KEYS_PER_WORD_GROUP
            planes = _bit_transpose32(
                [ukey[base + t * SUBLANES:base + (t + 1) * SUBLANES, :] for t in range(WORD_BITS)])
            row0 = pl.multiple_of((c * groups_per_chunk + gi) * SUBLANES, SUBLANES)
            for b in range(WORD_BITS):
                plane_ref[b, pl.ds(row0, SUBLANES), :] = planes[WORD_BITS - 1 - b]
        return carry

    lax.fori_loop(0, nch, score_body, 0)

    cand = jnp.full((n_words, Q_BLOCK), -1, I32)
    greater = jnp.zeros((n_words, Q_BLOCK), I32)
    cnt_gt = jnp.zeros((1, Q_BLOCK), F32)
    for b in range(WORD_BITS - 1, -1, -1):
        ones = cand & plane_ref[b]
        cnt = _popcount_rows(ones)
        take = cnt_gt + cnt >= topk
        greater = jnp.where(take, greater, greater | ones)
        cnt_gt = jnp.where(take, cnt_gt, cnt_gt + cnt)
        cand = jnp.where(take, ones, cand ^ ones)

    word_row = lax.broadcasted_iota(I32, (n_words, Q_BLOCK), 0)
    word_pos = (word_row >> 3) * KEYS_PER_WORD_GROUP + (word_row & (SUBLANES - 1))
    qcol = i * Q_BLOCK + lax.broadcasted_iota(I32, (n_words, Q_BLOCK), 1)

    def prefix(limit):
        nt = jnp.clip((limit - word_pos + (SUBLANES - 1)) >> 3, 0, WORD_BITS)
        top = lax.shift_right_arithmetic(jnp.full_like(nt, INT_MIN), jnp.maximum(nt, 1) - 1)
        return jnp.where(nt <= 0, 0, top)

    cand = cand & prefix(qcol + 1)
    need = topk - cnt_gt
    sel_ref[...] = greater | cand

    @pl.when(jnp.max(_popcount_rows(cand) - need) > 0)
    def _():
        def jbody(bi, jt):
            cj = jt | jnp.left_shift(jnp.int32(1), idx_bits - 1 - bi)
            return jnp.where(_popcount_rows(cand & prefix(cj)) <= need, cj, jt)
        jt = lax.fori_loop(0, idx_bits, jbody, jnp.zeros((1, Q_BLOCK), I32))
        sel_ref[...] = greater | (cand & prefix(jt))

    def bias_body(c, carry):
        off = pl.multiple_of(c * tk, tk)
        for gi in range(groups_per_chunk):
            row0 = pl.multiple_of((c * groups_per_chunk + gi) * SUBLANES, SUBLANES)
            w = sel_ref[pl.ds(row0, SUBLANES), :]
            for t in range(WORD_BITS):
                dst = pl.multiple_of(off + gi * KEYS_PER_WORD_GROUP + t * SUBLANES, SUBLANES)
                bias_ref[pl.ds(dst, SUBLANES), :] = jnp.where((w << t) < 0, 0.0, -jnp.inf)
        return carry

    lax.fori_loop(0, nch, bias_body, 0)

    n_rep = N_HEADS_A // N_KV_A
    cols = n_rep * Q_BLOCK
    q_all = q_ref[0].reshape(N_HEADS_A * Q_BLOCK, HEAD_DIM)
    scale2 = (HEAD_DIM ** -0.5) * LOG2_E
    m_ref[...] = jnp.full(m_ref.shape, -1e30, F32)
    acc_ref[...] = jnp.zeros(acc_ref.shape, F32)
    ones_rows = jnp.ones((SUM_ROWS, tk), BF16)

    def stage_qk(g, off):
        qg = q_all[g * cols:(g + 1) * cols, :]
        s_ref[g] = _dot_nt(k_ref[0, pl.ds(off, tk), g * HEAD_DIM:(g + 1) * HEAD_DIM], qg)

    def stage_max(g, off):
        for n in range(n_rep):
            csl = slice(n * Q_BLOCK, (n + 1) * Q_BLOCK)
            t = s_ref[g, :, csl] + bias_ref[pl.ds(off, tk), :]
            s_ref[g, :, csl] = t
            m_old = m_ref[g, :, csl]
            m_new = jnp.maximum(m_old, jnp.max(t, axis=0, keepdims=True))
            alpha_ref[g, :, csl] = jnp.exp2((m_old - m_new) * scale2)
            m_ref[g, :, csl] = m_new

    def stage_exp(g, off):
        for n in range(n_rep):
            csl = slice(n * Q_BLOCK, (n + 1) * Q_BLOCK)
            p_ref[g, :, csl] = jnp.exp2((s_ref[g, :, csl] - m_ref[g, 0:1, csl]) * scale2).astype(BF16)

    def stage_pv(g, off):
        v_ext = jnp.concatenate(
            [vt_ref[0, g * HEAD_DIM:(g + 1) * HEAD_DIM, pl.ds(off, tk)], ones_rows], axis=0)
        acc_ref[g] = alpha_ref[g, 0:1, :] * acc_ref[g] + _dot(v_ext, p_ref[g])

    stages = (stage_qk, stage_max, stage_exp, stage_pv)

    def att_body(c, carry):
        off = pl.multiple_of(c * tk, tk)
        for step in range(N_KV_A + len(stages) - 1):
            for g in range(N_KV_A):
                if 0 <= step - g < len(stages):
                    stages[step - g](g, off)
        return carry

    lax.fori_loop(0, nch, att_body, 0)
    for g in range(N_KV_A):
        acc = acc_ref[g]
        o_t = acc[0:HEAD_DIM, :] / acc[HEAD_DIM:HEAD_DIM + 1, :]
        for n in range(n_rep):
            h = g * n_rep + n
            o_ref[:, h * HEAD_DIM:(h + 1) * HEAD_DIM] = jnp.transpose(
                o_t[:, n * Q_BLOCK:(n + 1) * Q_BLOCK]).astype(o_ref.dtype)


def _attention(qh, wi, kkiv, vt, batch, seq_len):
    nb = seq_len // Q_BLOCK
    topk = min(TOPK_MAX, seq_len // 4)
    tk = min(512, seq_len)
    kvd = N_KV_A * HEAD_DIM
    cols = N_HEADS_A // N_KV_A * Q_BLOCK
    single = pl.Buffered(1)
    return pl.pallas_call(
        functools.partial(_attn_kernel, topk=topk, tk=tk, idx_bits=seq_len.bit_length()),
        grid=(batch, nb),
        in_specs=[pl.BlockSpec((1, N_HEADS_A, Q_BLOCK, HEAD_DIM), lambda b, i: (b * nb + i, 0, 0, 0)),
                  pl.BlockSpec((1, N_HEADS_IDX, Q_BLOCK, HEAD_DIM), lambda b, i: (b * nb + i, 1, 0, 0)),
                  pl.BlockSpec((Q_BLOCK, LANES), lambda b, i: (b * nb + i, 0)),
                  pl.BlockSpec((1, seq_len, kvd), lambda b, i: (b, 0, 0), pipeline_mode=single),
                  pl.BlockSpec((1, seq_len, HEAD_DIM), lambda b, i: (b, 0, kvd // HEAD_DIM),
                               pipeline_mode=single),
                  pl.BlockSpec((1, kvd, seq_len), lambda b, i: (b, 0, 0), pipeline_mode=single)],
        out_specs=pl.BlockSpec((Q_BLOCK, N_HEADS_A * HEAD_DIM), lambda b, i: (b * nb + i, 0)),
        out_shape=jax.ShapeDtypeStruct((batch * seq_len, N_HEADS_A * HEAD_DIM), BF16),
        scratch_shapes=[pltpu.VMEM((WORD_BITS, seq_len // WORD_BITS, Q_BLOCK), I32),
                        pltpu.VMEM((seq_len // WORD_BITS, Q_BLOCK), I32),
                        pltpu.VMEM((seq_len, Q_BLOCK), F32),
                        pltpu.VMEM((N_KV_A, SUBLANES, cols), F32),
                        pltpu.VMEM((N_KV_A, HEAD_DIM + SUM_ROWS, cols), F32),
                        pltpu.VMEM((N_KV_A, SUBLANES, cols), F32),
                        pltpu.VMEM((N_KV_A, tk, cols), F32),
                        pltpu.VMEM((N_KV_A, tk, cols), BF16)],
        compiler_params=_cparams(("arbitrary", "arbitrary"), 52),
        name="dsa_attention",
    )(qh, qh, wi, kkiv, kkiv, vt)


def _dot_hp(x, w_bf):
    hi = x.astype(BF16)
    r1 = x - hi.astype(F32)
    mid = r1.astype(BF16)
    lo = (r1 - mid.astype(F32)).astype(BF16)
    n = x.shape[0]
    parts = _dot(jnp.concatenate([hi, mid, lo], axis=0), w_bf)
    return parts[0:n] + parts[n:2 * n] + parts[2 * n:3 * n]


def _rwkv_kernel(r_ref, k_ref, v_ref, lw_ref, a_ref, g_ref, mu_ref, kk_ref, ka_ref, rk_ref,
                 lnw_ref, lnb_ref, bd_ref, ones_ref, ltri_ref, ts_ref, ti_ref, ic_ref, eye_ref,
                 o_ref, st_ref, prev_ref):
    @pl.when(pl.program_id(2) == 0)
    def _():
        st_ref[...] = jnp.zeros_like(st_ref)
        prev_ref[...] = jnp.zeros_like(prev_ref)

    tiles = [_rwkv_tile(tile, r_ref, k_ref, v_ref, lw_ref, a_ref, g_ref, mu_ref, kk_ref, ka_ref, rk_ref,
                        lnw_ref, lnb_ref, bd_ref, ones_ref, ltri_ref, ts_ref, ti_ref, ic_ref, eye_ref,
                        o_ref, st_ref, prev_ref)
             for tile in range(r_ref.shape[1] // TILE_B)]
    while tiles:
        tiles = [t for t in tiles if next(t, "done") != "done"]


def _rwkv_tile(tile, r_ref, k_ref, v_ref, lw_ref, a_ref, g_ref, mu_ref, kk_ref, ka_ref, rk_ref,
               lnw_ref, lnb_ref, bd_ref, ones_ref, ltri_ref, ts_ref, ti_ref, ic_ref, eye_ref,
               o_ref, st_ref, prev_ref):
    cols = slice(tile * TILE_B, (tile + 1) * TILE_B)
    bdm = bd_ref[...]
    row = lax.broadcasted_iota(I32, (CHUNK, TILE_B), 0)

    def shifted(x, slot):
        prev = prev_ref[slot:slot + 1, cols]
        prev_ref[slot:slot + 1, cols] = x[CHUNK - 1:CHUNK, :]
        return jnp.where(row == 0, prev, pltpu.roll(x, 1, axis=0))

    r0 = r_ref[:, cols]
    k0 = k_ref[:, cols]
    v0 = v_ref[:, cols]
    r = r0 + (shifted(r0, 0) - r0) * mu_ref[0:1, cols]
    k = k0 + (shifted(k0, 1) - k0) * mu_ref[1:2, cols]
    v = v0 + (shifted(v0, 2) - v0) * mu_ref[2:3, cols]
    a = a_ref[:, cols]
    ones_bd = ones_ref[...]

    kk = k * kk_ref[:, cols]
    k2 = k * (1.0 + (a - 1.0) * ka_ref[:, cols])
    head_sums = _dot_hp(jnp.concatenate([kk * kk, r * k2 * rk_ref[:, cols]], axis=0), ones_bd)
    ss = head_sums[0:CHUNK]
    bonus = head_sums[CHUNK:2 * CHUNK] * v
    yield
    kk = kk / jnp.maximum(jnp.sqrt(ss), 1e-12)
    aa = -kk
    bb = kk * a

    lw = lw_ref[:, cols]
    cs = _cumsum_rows(lw, ltri_ref[...])
    yield
    tot = cs[CHUNK - 1:CHUNK, :]
    e_in = jnp.exp(cs)
    e_out = jnp.exp(-cs)
    e_tail = jnp.exp(tot - cs)
    at = aa * jnp.exp(cs - lw)
    rt = r * e_in
    bt = (bb * e_out)
    kt = (k2 * e_out)
    bh = (bb * e_tail).astype(BF16)
    kh = (k2 * e_tail).astype(BF16)
    wc = jnp.exp(tot)

    def bd(x):
        return jnp.concatenate([x.astype(BF16)] * HEADS_PER_TILE, axis=0) * ones_bd

    lhs = jnp.concatenate([at, rt], axis=0).astype(BF16)
    mb = _dot_nt(lhs, bd(bt))
    mk = _dot_nt(lhs, bd(kt))
    yield
    ts = ts_ref[...]
    ti = ti_ref[...]
    m_ab = mb[0:CHUNK] * ts
    m_rb = (mb[CHUNK:2 * CHUNK] * ti).astype(BF16)
    m_ak = (mk[0:CHUNK] * ts).astype(BF16)
    m_rk = (mk[CHUNK:2 * CHUNK] * ti).astype(BF16)

    n_round = CHUNK.bit_length() - 1
    t_inv = ic_ref[...] + m_ab
    m_pow = _dot(m_ab.astype(BF16), bd(m_ab))
    yield
    for rnd in range(1, n_round):
        last = rnd == n_round - 1
        lhs_rows = [t_inv] if last else [t_inv, m_pow]
        prod = _dot(jnp.concatenate(lhs_rows, axis=0).astype(BF16), bd(m_pow))
        yield
        t_inv = t_inv + prod[0:CHUNK]
        if not last:
            m_pow = prod[CHUNK:2 * CHUNK]
    t_bf = t_inv.astype(BF16)

    p = _dot(t_bf, bd(at))
    uv = _dot(jnp.concatenate([m_ak, m_rk], axis=0), bd(v))
    u = uv[0:CHUNK]
    yield
    q = _dot(t_bf, bd(u))
    rp = rt + _dot(m_rb, bd(p))
    yield
    y0 = _dot(m_rb, bd(q)) + uv[CHUNK:2 * CHUNK]

    yield
    a_t = _dot_tn(bh, p.astype(BF16)) * bdm + eye_ref[...] * wc
    d_t = _dot_tn(jnp.concatenate([bh, kh], axis=0),
                  jnp.concatenate([q.astype(BF16), v.astype(BF16)], axis=0)) * bdm
    st = st_ref[tile]
    from_state = _dot(jnp.concatenate([rp.astype(BF16), a_t.astype(BF16)], axis=0), st.astype(BF16))
    y = from_state[0:CHUNK] + y0
    st_ref[tile] = from_state[CHUNK:CHUNK + TILE_B] + d_t

    yield
    inv_n = 1.0 / HEAD_DIM_B
    mu = _dot_hp(y, ones_bd) * inv_n
    yield
    yc = y - mu
    var = _dot_hp(yc * yc, ones_bd) * inv_n
    yield
    yn =(yc * lax.rsqrt(var + GN_EPS)) * lnw_ref[:, cols] + lnb_ref[:, cols]
    o_ref[:, cols] = ((yn + bonus) * g_ref[:, cols]).astype(o_ref.dtype)


def _cumsum_rows(x, ltri_bf):
    hi = x.astype(BF16)
    r1 = x - hi.astype(F32)
    mid = r1.astype(BF16)
    lo = (r1 - mid.astype(F32)).astype(BF16)
    return _dot(ltri_bf, hi) + _dot(ltri_bf, mid) + _dot(ltri_bf, lo)


def _rwkv_consts():
    idx = jnp.arange(TILE_B)
    head = idx // HEAD_DIM_B
    bdm = (head[:, None] == head[None, :])
    t = jnp.arange(CHUNK)[:, None]
    s = (idx % CHUNK)[None, :]
    return dict(
        bd=bdm.astype(F32),
        ones=bdm.astype(BF16),
        ltri=(jnp.arange(CHUNK)[None, :] <= t).astype(BF16),
        ts=(s < t).astype(F32),
        ti=(s <= t).astype(F32),
        ic=(s == t).astype(F32),
        eye=jnp.eye(TILE_B, dtype=F32),
    )


def _rwkv(rkv, lw, a, g, mu_rkv, k_k, k_a, r_k, ln_w, ln_b, batch, seq_len):
    m, db = lw.shape
    nc = seq_len // CHUNK
    width = RWKV_TILES_PER_STEP * TILE_B
    ng = db // width
    cst = _rwkv_consts()
    blk = lambda off: pl.BlockSpec((CHUNK, width), lambda b, h, c, off=off: (b * nc + c, off + h))
    par = lambda rows: pl.BlockSpec((rows, width), lambda b, h, c: (0, h))
    full = lambda arr: pl.BlockSpec(arr.shape, lambda b, h, c: (0, 0))
    row = lambda p: p.reshape(1, db)
    return pl.pallas_call(
        _rwkv_kernel,
        grid=(batch, ng, nc),
        in_specs=[blk(0), blk(ng), blk(2 * ng), blk(0), blk(0), blk(0),
                  par(3), par(1), par(1), par(1), par(1), par(1),
                  full(cst["bd"]), full(cst["ones"]), full(cst["ltri"]), full(cst["ts"]),
                  full(cst["ti"]), full(cst["ic"]), full(cst["eye"])],
        out_specs=blk(0),
        out_shape=jax.ShapeDtypeStruct((m, db), BF16),
        scratch_shapes=[pltpu.VMEM((RWKV_TILES_PER_STEP, TILE_B, TILE_B), F32),
                        pltpu.VMEM((SUBLANES, width), F32)],
        compiler_params=_cparams(("arbitrary", "arbitrary", "arbitrary"), 32),
        name="rwkv7",
    )(rkv, rkv, rkv, lw, a, g, mu_rkv, row(k_k), row(k_a), row(r_k), row(ln_w), row(ln_b),
      cst["bd"], cst["ones"], cst["ltri"], cst["ts"], cst["ti"], cst["ic"], cst["eye"])


def _merge_kernel(xn_ref, oa_ref, ob_ref, wga_ref, wgb_ref, bga_ref, bgb_ref, wpa_ref, wpb_ref, o_ref):
    xn = xn_ref[...]
    ga = _sigmoid(_dot(xn, wga_ref[...]) + bga_ref[...])
    gb = _sigmoid(_dot(xn, wgb_ref[...]) + bgb_ref[...])
    ya = _dot(oa_ref[...], wpa_ref[...])
    yb = _dot(ob_ref[...], wpb_ref[...])
    o_ref[...] = (ga * ya + gb * yb).astype(o_ref.dtype)


def _merge(xn, oa, ob, w_gate, b_gate, w_pa, w_pb, bm=1024, bn=256, side=None):
    m, d = xn.shape
    bm = min(bm, m)
    nj = d // bn
    return _matmul_call(
        _merge_kernel, (m // bm, nj),
        [pl.BlockSpec((bm, d), lambda i, j: (i, 0)),
         pl.BlockSpec((bm, oa.shape[1]), lambda i, j: (i, 0)),
         pl.BlockSpec((bm, ob.shape[1]), lambda i, j: (i, 0)),
         pl.BlockSpec((d, bn), lambda i, j: (0, j)),
         pl.BlockSpec((d, bn), lambda i, j: (0, j + nj)),
         pl.BlockSpec((1, bn), lambda i, j: (0, j)),
         pl.BlockSpec((1, bn), lambda i, j: (0, j + nj)),
         pl.BlockSpec((w_pa.shape[0], bn), lambda i, j: (0, j)),
         pl.BlockSpec((w_pb.shape[0], bn), lambda i, j: (0, j))],
        pl.BlockSpec((bm, bn), lambda i, j: (i, j)),
        jax.ShapeDtypeStruct((m, d), BF16),
        (xn, oa, ob, w_gate, w_gate, b_gate, b_gate, w_pa, w_pb), "gated_merge", vmem_mib=56, side=side)


def _mm_res_kernel(x_ref, w_ref, res_ref, o_ref):
    o_ref[...] = res_ref[...] + _dot(x_ref[...], w_ref[...])


def _matmul_residual(x, w, res, bm=1024, bn=1024, name="matmul_res", side=None):
    m, k = x.shape
    n = w.shape[1]
    bm = min(bm, m)
    return _matmul_call(
        _mm_res_kernel, (m // bm, n // bn),
        [pl.BlockSpec((bm, k), lambda i, j: (i, 0)),
         pl.BlockSpec((k, bn), lambda i, j: (0, j)),
         pl.BlockSpec((bm, bn), lambda i, j: (i, j))],
        pl.BlockSpec((bm, bn), lambda i, j: (i, j)),
        jax.ShapeDtypeStruct((m, n), F32), (x, w, res), name, side=side)


def _ffn_up_kernel(x_ref, w1_ref, w3_ref, o_ref):
    x = x_ref[...]
    h1 = _dot(x, w1_ref[...])
    h3 = _dot(x, w3_ref[...])
    o_ref[...] = (h1 * _sigmoid(h1) * h3).astype(o_ref.dtype)


def _ffn_up(x, w1, w3, bm=1024, bn=512, side=None):
    m, k = x.shape
    n = w1.shape[1]
    bm = min(bm, m)
    return _matmul_call(
        _ffn_up_kernel, (m // bm, pl.cdiv(n, bn)),
        [pl.BlockSpec((bm, k), lambda i, j: (i, 0)),
         pl.BlockSpec((k, bn), lambda i, j: (0, j)),
         pl.BlockSpec((k, bn), lambda i, j: (0, j))],
        pl.BlockSpec((bm, bn), lambda i, j: (i, j)),
        jax.ShapeDtypeStruct((m, n), BF16), (x, w1, w3), "ffn_up", side=side)


def _ple_kernel(h_ref, wg_ref, p_ref, wp_ref, gain_ref, o_ref, hn_ref, *, bn, final_norm):
    j = pl.program_id(1)
    n_row_chunks = h_ref.shape[0] // NORM_ROW_CHUNK

    def row_chunk(r):
        return pl.ds(pl.multiple_of(r * NORM_ROW_CHUNK, NORM_ROW_CHUNK), NORM_ROW_CHUNK)

    @pl.when(j == 0)
    def _():
        def body(r, carry):
            hn_ref[row_chunk(r), :] = _rms_rows(h_ref[row_chunk(r), :]).astype(BF16)
            return carry
        lax.fori_loop(0, n_row_chunks, body, 0)

    col = pl.multiple_of(j * bn, bn)
    gate = _sigmoid(_dot(hn_ref[...], wg_ref[...]))
    o_ref[:, pl.ds(col, bn)] = h_ref[:, pl.ds(col, bn)] + gate * _dot(p_ref[...], wp_ref[...])

    if final_norm:
        @pl.when(j == pl.num_programs(1) - 1)
        def _():
            def body(r, carry):
                o_ref[row_chunk(r), :] = _rms_rows(o_ref[row_chunk(r), :]) * gain_ref[...]
                return carry
            lax.fori_loop(0, n_row_chunks, body, 0)


def _ple(h, wg, p, wp, final_gain, bm=512, bn=512):
    m, d = h.shape
    bm = min(bm, m)
    final_norm = final_gain is not None
    gain = (final_gain if final_norm else jnp.ones((d,), F32)).reshape(1, d)
    return pl.pallas_call(
        functools.partial(_ple_kernel, bn=bn, final_norm=final_norm),
        grid=(m // bm, d // bn),
        in_specs=[pl.BlockSpec((bm, d), lambda i, j: (i, 0)),
                  pl.BlockSpec((d, bn), lambda i, j: (0, j)),
                  pl.BlockSpec((bm, p.shape[1]), lambda i, j: (i, 0)),
                  pl.BlockSpec((p.shape[1], bn), lambda i, j: (0, j)),
                  pl.BlockSpec((1, d), lambda i, j: (0, 0))],
        out_specs=pl.BlockSpec((bm, d), lambda i, j: (i, 0)),
        out_shape=jax.ShapeDtypeStruct((m, d), F32),
        scratch_shapes=[pltpu.VMEM((bm, d), BF16)],
        compiler_params=_cparams(("arbitrary", "arbitrary"), 52),
        name="ple",
    )(h, wg, p, wp, gain)


def _rmsnorm_kernel(x_ref, g_ref, o_ref):
    o_ref[...] = (_rms_rows(x_ref[...]) * g_ref[...]).astype(o_ref.dtype)


def _rmsnorm(x, gain, out_dtype, bm=512):
    m, d = x.shape
    bm = min(bm, m)
    row_spec = pl.BlockSpec((bm, d), lambda i: (i, 0))
    return pl.pallas_call(
        _rmsnorm_kernel,
        grid=(m // bm,),
        in_specs=[row_spec, pl.BlockSpec((1, d), lambda i: (0, 0))],
        out_specs=row_spec,
        out_shape=jax.ShapeDtypeStruct((m, d), out_dtype),
        compiler_params=_cparams(("arbitrary",), 48),
        name="rmsnorm",
    )(x, gain.reshape(1, d))


def _pad_to(a, axis, size):
    pad = [(0, 0)] * a.ndim
    pad[axis] = (0, size - a.shape[axis])
    return jnp.pad(a, pad)


def _layer(h2, p2, cos, sin, batch, seq_len, norm_mix, w_in, mu_rkv, mu_wag, w0, w1, w2, a0, a1, a2,
           g1, g2, k_k, k_a, r_k, ln_w, ln_b, w_pa, w_pb, w_gate, b_gate, w_o, norm_ffn,
           w_ffn1, w_ffn3, w_ffn2, w_ple_gate, w_ple, final_gain):
    d = h2.shape[1]
    d_a = N_HEADS_A * HEAD_DIM
    kvd = N_KV_A * HEAD_DIM
    d_iq = N_HEADS_IDX * HEAD_DIM
    d_b = w_pb.shape[0]
    o_q, o_k, o_v = 0, d_a, d_a + kvd
    o_qi = o_v + kvd
    o_ki = o_qi + d_iq
    o_wi = o_ki + HEAD_DIM
    o_r = o_wi + N_HEADS_IDX

    bf = lambda a: a.astype(BF16)
    assert d_a == d_iq and o_wi % LANES == 0
    lora = LANES
    w1p, a1p = bf(_pad_to(w1, 1, lora)), bf(_pad_to(a1, 1, lora))
    w2p, a2p = bf(_pad_to(w2, 0, lora)), bf(_pad_to(a2, 0, lora))

    xn, lw, a, g, wi, w_in_bf = _prep(h2, norm_mix.reshape(1, d), mu_wag, w1p, w2p, w0.reshape(1, d_b),
                                      a1p, a2p, a0.reshape(1, d_b), bf(g1), bf(g2), w_in,
                                      o_wi // LANES, seq_len)
    w_rkv = w_in_bf[:, o_r:o_r + 3 * d_b]

    qh, w_o_bf = _proj_rope_headmajor(xn, w_in_bf, (o_q, o_qi), d_a, cos, sin, side=w_o)
    kkiv = _proj_kv(xn, w_in_bf, o_k, o_ki, o_v, cos, sin)
    rkv, w_gate_bf = _matmul(xn, w_rkv, F32, bn=512, name="proj_rkv", side=w_gate)

    kkiv = kkiv.reshape(batch, seq_len, -1)
    vt = jnp.swapaxes(kkiv[:, :, kvd + HEAD_DIM:], 1, 2)
    o_att = _attention(qh, wi, kkiv, vt, batch, seq_len)

    o_rwkv = _rwkv(rkv, lw, a, g, mu_rkv, k_k, k_a, r_k.reshape(-1), ln_w, ln_b, batch, seq_len)

    mixed, w_ffn1_bf = _merge(xn, o_att, o_rwkv, w_gate_bf, b_gate.reshape(1, -1), bf(w_pa), bf(w_pb),
                              side=w_ffn1)
    h2, w_ffn3_bf = _matmul_residual(mixed, w_o_bf, h2, bn=512, name="out_proj", side=w_ffn3)

    xf = _rmsnorm(h2, norm_ffn, BF16)
    u, w_ffn2_bf = _ffn_up(xf, w_ffn1_bf, w_ffn3_bf, side=w_ffn2)
    h2, w_ple_gate_bf = _matmul_residual(u, w_ffn2_bf, h2, bm=512, bn=512, name="ffn_down",
                                         side=w_ple_gate)

    return _ple(h2, w_ple_gate_bf, bf(p2), bf(w_ple), final_gain)


def kernel(x, p, positions, norm_mix, w_in, mu_rkv, mu_wag, w0, w1, w2, a0, a1, a2, g1, g2, k_k, k_a,
           r_k, ln_w, ln_b, w_pa, w_pb, w_gate, b_gate, w_o, norm_ffn, w_ffn1, w_ffn3, w_ffn2,
           w_ple_gate, w_ple, norm_final):
    batch, seq_len, d = x.shape
    depth = p.shape[0]
    h2 = x.reshape(batch * seq_len, d)
    cos, sin = _rope_tables(positions)
    for i in range(depth):
        h2 = _layer(h2, p[i].reshape(batch * seq_len, -1), cos, sin, batch, seq_len,
                    norm_mix[i], w_in[i], mu_rkv[i], mu_wag[i], w0[i], w1[i], w2[i], a0[i], a1[i], a2[i],
                    g1[i], g2[i], k_k[i], k_a[i], r_k[i], ln_w[i], ln_b[i], w_pa[i], w_pb[i], w_gate[i],
                    b_gate[i], w_o[i], norm_ffn[i], w_ffn1[i], w_ffn3[i], w_ffn2[i], w_ple_gate[i],
                    w_ple[i], norm_final if i == depth - 1 else None)
    return h2.reshape(batch, seq_len, d)
```

```python
import functools

import jax
import jax.numpy as jnp
from jax import lax
from jax.experimental import pallas as pl
from jax.experimental.pallas import tpu as pltpu

F32 = jnp.float32
BF16 = jnp.bfloat16
I32 = jnp.int32

N_HEADS_A = 16
HEAD_DIM = 128
N_KV_A = 4
N_HEADS_IDX = 16
TOPK_MAX = 256
Q_BLOCK = 128
ROPE_THETA = 10000.0
HEAD_DIM_B = 64
GN_EPS = 64e-5
RMS_EPS = 1e-6

LANES = 128
SUBLANES = 8
BF16_SUBLANES = 16
MXU_DIM = 256

INT_MIN = -2 ** 31
LOG2_E = 1.4426950408889634
NORM_ROW_CHUNK = 64
DECAY_SCALE = 0.6065306597126334

CHUNK = 64
HEADS_PER_TILE = MXU_DIM // HEAD_DIM_B
TILE_B = HEADS_PER_TILE * HEAD_DIM_B
RWKV_CHUNKS_PER_STEP = 4
RWKV_TILES_PER_STEP = 4


def _cparams(sem, vmem_mib):
    return pltpu.CompilerParams(dimension_semantics=sem, vmem_limit_bytes=vmem_mib << 20)


def _sidecar_cast(body, n_in):
    def kern(*refs):
        refs[n_in + 2][...] = refs[n_in][...].astype(BF16)
        body(*refs[:n_in], refs[n_in + 1])
    return kern


def _matmul_call(body, grid, in_specs, out_spec, out_shape, args, name, vmem_mib=52, side=None):
    params = _cparams(("arbitrary", "arbitrary"), vmem_mib)
    if side is None:
        out = pl.pallas_call(body, grid=grid, in_specs=in_specs, out_specs=out_spec,
                             out_shape=out_shape, compiler_params=params, name=name)(*args)
        return out, None
    steps, n_inner = grid[0] * grid[1], grid[1]
    rows = side.shape[0]
    rb = -(-(-(-rows // steps)) // BF16_SUBLANES) * BF16_SUBLANES
    last = -(-rows // rb) - 1
    side_spec = pl.BlockSpec((rb, side.shape[1]), lambda i, j: (jnp.minimum(i * n_inner + j, last), 0))
    return pl.pallas_call(
        _sidecar_cast(body, len(in_specs)), grid=grid, in_specs=[*in_specs, side_spec],
        out_specs=[out_spec, side_spec],
        out_shape=[out_shape, jax.ShapeDtypeStruct(side.shape, BF16)],
        compiler_params=params, name=name)(*args, side)


def _dot(a, b):
    return jnp.dot(a, b, preferred_element_type=F32)


def _dot_nt(a, b):
    return lax.dot_general(a, b, (((1,), (1,)), ((), ())), preferred_element_type=F32)


def _dot_tn(a, b):
    return lax.dot_general(a, b, (((0,), (0,)), ((), ())), preferred_element_type=F32)


def _sigmoid(x):
    return 1.0 / (1.0 + jnp.exp(-x))


def _rms_rows(x):
    return x * lax.rsqrt(jnp.mean(x * x, axis=-1, keepdims=True) + RMS_EPS)


def _rope_tab_kernel(pos_ref, freq_ref, sign_ref, cos_ref, sin_ref):
    ang = pos_ref[...].astype(F32) * freq_ref[...]
    cos_ref[...] = jnp.cos(ang)
    sin_ref[...] = jnp.sin(ang) * sign_ref[...]


def _rope_tables(positions):
    n = positions.size
    half = HEAD_DIM // 2
    inv_freq = ROPE_THETA ** (-jnp.arange(0, HEAD_DIM, 2, dtype=F32) / HEAD_DIM)
    freq2 = jnp.concatenate([inv_freq, inv_freq]).reshape(1, HEAD_DIM)
    sign = jnp.concatenate([-jnp.ones((half,), F32), jnp.ones((half,), F32)]).reshape(1, HEAD_DIM)
    bm = min(2048, n)
    return pl.pallas_call(
        _rope_tab_kernel,
        grid=(n // bm,),
        in_specs=[pl.BlockSpec((bm, 1), lambda i: (i, 0)),
                  pl.BlockSpec((1, HEAD_DIM), lambda i: (0, 0)),
                  pl.BlockSpec((1, HEAD_DIM), lambda i: (0, 0))],
        out_specs=[pl.BlockSpec((bm, HEAD_DIM), lambda i: (i, 0)),
                   pl.BlockSpec((bm, HEAD_DIM), lambda i: (i, 0))],
        out_shape=[jax.ShapeDtypeStruct((n, HEAD_DIM), F32)] * 2,
        compiler_params=_cparams(("arbitrary",), 32),
        name="rope_tables",
    )(positions.reshape(n, 1), freq2, sign)


def _rope(t, cos, sin):
    return t * cos + pltpu.roll(t, HEAD_DIM // 2, axis=1) * sin


def _prep_kernel(x_ref, xp_ref, gain_ref, mu_ref, w1_ref, w2_ref, w0_ref, a1_ref, a2_ref, a0_ref,
                 g1_ref, g2_ref, wwi_ref,
                 xn_ref, lw_ref, a_ref, g_ref, wi_ref, *, seq_len, bm):
    i = pl.program_id(0)
    gain = gain_ref[...]
    xn = _rms_rows(x_ref[...]) * gain
    prev = (_rms_rows(xp_ref[...]) * gain)[SUBLANES - 1:SUBLANES, :]
    prev = jnp.where((i * bm) % seq_len == 0, jnp.zeros_like(prev), prev)
    row = lax.broadcasted_iota(I32, xn.shape, 0)
    sh = jnp.where(row == 0, prev, pltpu.roll(xn, 1, axis=0))
    xx = sh - xn
    xn_bf = xn.astype(BF16)
    xn_ref[...] = xn_bf
    wi_ref[...] = _dot(xn_bf, wwi_ref[...]) * (N_HEADS_IDX ** -0.5)

    xw = (xn + xx * mu_ref[0:1, :]).astype(BF16)
    hw = jnp.tanh(_dot(xw, w1_ref[...])).astype(BF16)
    wl = w0_ref[...] + _dot(hw, w2_ref[...])
    lw_ref[...] = -DECAY_SCALE * _sigmoid(wl)

    xa = (xn + xx * mu_ref[1:2, :]).astype(BF16)
    ha = _dot(xa, a1_ref[...]).astype(BF16)
    a_ref[...] = _sigmoid(a0_ref[...] + _dot(ha, a2_ref[...]))

    xg = (xn + xx * mu_ref[2:3, :]).astype(BF16)
    hg = _sigmoid(_dot(xg, g1_ref[...])).astype(BF16)
    g_ref[...] = _dot(hg, g2_ref[...])


def _prep(x2, gain, mu_wag, w1, w2, w0, a1, a2, a0, g1, g2, w_in, wi_block, seq_len):
    m, d = x2.shape
    db = w2.shape[1]
    bm = min(128, m)
    full = lambda a: pl.BlockSpec(a.shape, lambda i: (0,) * a.ndim)
    nsub = bm // SUBLANES
    return pl.pallas_call(
        functools.partial(_prep_kernel, seq_len=seq_len, bm=bm),
        grid=(m // bm,),
        in_specs=[pl.BlockSpec((bm, d), lambda i: (i, 0)),
                  pl.BlockSpec((SUBLANES, d), lambda i: (jnp.maximum(i * nsub - 1, 0), 0)),
                  full(gain), full(mu_wag), full(w1), full(w2), full(w0), full(a1), full(a2), full(a0),
                  full(g1), full(g2), pl.BlockSpec((d, LANES), lambda i: (0, wi_block))],
        out_specs=[pl.BlockSpec((bm, d), lambda i: (i, 0)),
                   pl.BlockSpec((bm, db), lambda i: (i, 0)),
                   pl.BlockSpec((bm, db), lambda i: (i, 0)),
                   pl.BlockSpec((bm, db), lambda i: (i, 0)),
                   pl.BlockSpec((bm, LANES), lambda i: (i, 0))],
        out_shape=[jax.ShapeDtypeStruct((m, d), BF16),
                   jax.ShapeDtypeStruct((m, db), F32),
                   jax.ShapeDtypeStruct((m, db), F32),
                   jax.ShapeDtypeStruct((m, db), F32),
                   jax.ShapeDtypeStruct((m, LANES), F32)],
        compiler_params=_cparams(("arbitrary",), 48),
        name="prep",
    )(x2, x2, gain, mu_wag, w1, w2, w0, a1, a2, a0, g1, g2, w_in)


def _proj_rope_hm_kernel(x_ref, w_ref, cos_ref, sin_ref, o_ref):
    acc = _dot(x_ref[...], w_ref[...])
    cos = cos_ref[...]
    sin = sin_ref[...]
    bm, bn = acc.shape
    for j in range(bn // HEAD_DIM):
        t = _rope(acc[:, j * HEAD_DIM:(j + 1) * HEAD_DIM], cos, sin).astype(o_ref.dtype)
        for r in range(bm // Q_BLOCK):
            o_ref[r, j] = t[r * Q_BLOCK:(r + 1) * Q_BLOCK, :]


def _proj_rope_headmajor(xn, w, col_starts, width, cos, sin, bm=1024, bn=1024, side=None):
    m, k = xn.shape
    n = width * len(col_starts)
    bm = min(bm, m)
    per = width // bn
    first, second = (c // bn for c in col_starts)

    def w_block(i, j):
        return 0, jnp.where(j < per, first + j, second + j - per)

    return _matmul_call(
        _proj_rope_hm_kernel, (m // bm, n // bn),
        [pl.BlockSpec((bm, k), lambda i, j: (i, 0)),
         pl.BlockSpec((k, bn), w_block),
         pl.BlockSpec((bm, HEAD_DIM), lambda i, j: (i, 0)),
         pl.BlockSpec((bm, HEAD_DIM), lambda i, j: (i, 0))],
        pl.BlockSpec((bm // Q_BLOCK, bn // HEAD_DIM, Q_BLOCK, HEAD_DIM), lambda i, j: (i, j, 0, 0)),
        jax.ShapeDtypeStruct((m // Q_BLOCK, n // HEAD_DIM, Q_BLOCK, HEAD_DIM), BF16),
        (xn, w, cos, sin), "proj_q", side=side)


def _proj_kv_kernel(x_ref, wk_ref, wki_ref, wv_ref, cos_ref, sin_ref, o_ref):
    x = x_ref[...]
    cos = cos_ref[...]
    sin = sin_ref[...]
    col = 0
    for w_ref, rotary in ((wk_ref, True), (wki_ref, True), (wv_ref, False)):
        acc = _dot(x, w_ref[...])
        for j in range(acc.shape[1] // HEAD_DIM):
            t = acc[:, j * HEAD_DIM:(j + 1) * HEAD_DIM]
            if rotary:
                t = _rope(t, cos, sin)
            o_ref[:, col:col + HEAD_DIM] = t.astype(o_ref.dtype)
            col += HEAD_DIM


def _proj_kv(xn, w, col_k, col_ki, col_v, cos, sin, bm=1024):
    m, k = xn.shape
    kvd = N_KV_A * HEAD_DIM
    n = 2 * kvd + HEAD_DIM
    bm = min(bm, m)
    return pl.pallas_call(
        _proj_kv_kernel,
        grid=(m // bm,),
        in_specs=[pl.BlockSpec((bm, k), lambda i: (i, 0)),
                  pl.BlockSpec((k, kvd), lambda i: (0, col_k // kvd)),
                  pl.BlockSpec((k, HEAD_DIM), lambda i: (0, col_ki // HEAD_DIM)),
                  pl.BlockSpec((k, kvd), lambda i: (0, col_v // kvd)),
                  pl.BlockSpec((bm, HEAD_DIM), lambda i: (i, 0)),
                  pl.BlockSpec((bm, HEAD_DIM), lambda i: (i, 0))],
        out_specs=pl.BlockSpec((bm, n), lambda i: (i, 0)),
        out_shape=jax.ShapeDtypeStruct((m, n), BF16),
        compiler_params=_cparams(("arbitrary",), 52),
        name="proj_kv",
    )(xn, w, w, w, cos, sin)


def _mm_kernel(x_ref, w_ref, o_ref):
    o_ref[...] = _dot(x_ref[...], w_ref[...]).astype(o_ref.dtype)


def _matmul(x, w, out_dtype, bm=1024, bn=1024, name="matmul", side=None):
    m, k = x.shape
    n = w.shape[1]
    bm = min(bm, m)
    return _matmul_call(
        _mm_kernel, (m // bm, n // bn),
        [pl.BlockSpec((bm, k), lambda i, j: (i, 0)),
         pl.BlockSpec((k, bn), lambda i, j: (0, j))],
        pl.BlockSpec((bm, bn), lambda i, j: (i, j)),
        jax.ShapeDtypeStruct((m, n), out_dtype), (x, w), name, side=side)


WORD_BITS = 32
SUM_ROWS = 16
KEYS_PER_WORD_GROUP = WORD_BITS * SUBLANES


def _bit_transpose32(words):
    a = list(words)
    j, m = 16, 0x0000FFFF
    while j:
        mask = jnp.int32(m - (1 << 32) if m >= (1 << 31) else m)
        k = 0
        while k < WORD_BITS:
            t = (a[k] ^ lax.shift_right_logical(a[k + j], jnp.full_like(a[k], j))) & mask
            a[k] = a[k] ^ t
            a[k + j] = a[k + j] ^ (t << j)
            k = (k + j + 1) & ~j
        j >>= 1
        m = (m ^ (m << j)) & 0xFFFFFFFF
    return a


def _popcount_rows(words):
    per_sublane = jnp.sum(lax.population_count(words).reshape(-1, SUBLANES, words.shape[1]), axis=0)
    return jnp.sum(per_sublane.astype(F32), axis=0, keepdims=True)


def _attn_kernel(q_ref, qi_ref, wi_ref, k_ref, ki_ref, vt_ref, o_ref,
                 plane_ref, sel_ref, bias_ref, m_ref, acc_ref, alpha_ref, s_ref, p_ref,
                 *, topk, tk, idx_bits):
    i = pl.program_id(1)
    nk = (i + 1) * Q_BLOCK
    nch = (nk + tk - 1) // tk
    n_words = plane_ref.shape[1]
    groups_per_chunk = tk // KEYS_PER_WORD_GROUP

    @pl.when(i == 0)
    def _():
        plane_ref[...] = jnp.zeros_like(plane_ref)

    qi = qi_ref[0].reshape(N_HEADS_IDX * Q_BLOCK, HEAD_DIM)
    wi_t = jnp.transpose(wi_ref[...]) * (HEAD_DIM ** -0.5)
    qpos = i * Q_BLOCK + lax.broadcasted_iota(I32, (tk, Q_BLOCK), 1)
    krow = lax.broadcasted_iota(I32, (tk, Q_BLOCK), 0)

    def score_body(c, carry):
        off = pl.multiple_of(c * tk, tk)
        lg = _dot_nt(ki_ref[0, pl.ds(off, tk), :], qi)
        sc = jnp.zeros((tk, Q_BLOCK), F32)
        for h in range(N_HEADS_IDX):
            sc = sc + wi_t[h:h + 1, :] * jnp.maximum(lg[:, h * Q_BLOCK:(h + 1) * Q_BLOCK], 0.0)
        bits = pltpu.bitcast(sc, I32)
        key = bits ^ ((bits >> 31) & 0x7FFFFFFF)
        key = jnp.where(key == -1, 0, key)
        ukey = jnp.where(off + krow <= qpos, key ^ INT_MIN, 0)
        for gi in range(groups_per_chunk):
            base = gi * KEYS_PER_WORD_GROUP
            planes = _bit_transpose32(
                [ukey[base + t * SUBLANES:base + (t + 1) * SUBLANES, :] for t in range(WORD_BITS)])
            row0 = pl.multiple_of((c * groups_per_chunk + gi) * SUBLANES, SUBLANES)
            for b in range(WORD_BITS):
                plane_ref[b, pl.ds(row0, SUBLANES), :] = planes[WORD_BITS - 1 - b]
        return carry

    lax.fori_loop(0, nch, score_body, 0)

    cand = jnp.full((n_words, Q_BLOCK), -1, I32)
    greater = jnp.zeros((n_words, Q_BLOCK), I32)
    cnt_gt = jnp.zeros((1, Q_BLOCK), F32)
    for b in range(WORD_BITS - 1, -1, -1):
        ones = cand & plane_ref[b]
        cnt = _popcount_rows(ones)
        take = cnt_gt + cnt >= topk
        greater = jnp.where(take, greater, greater | ones)
        cnt_gt = jnp.where(take, cnt_gt, cnt_gt + cnt)
        cand = jnp.where(take, ones, cand ^ ones)

    word_row = lax.broadcasted_iota(I32, (n_words, Q_BLOCK), 0)
    word_pos = (word_row >> 3) * KEYS_PER_WORD_GROUP + (word_row & (SUBLANES - 1))
    qcol = i * Q_BLOCK + lax.broadcasted_iota(I32, (n_words, Q_BLOCK), 1)

    def prefix(limit):
        nt = jnp.clip((limit - word_pos + (SUBLANES - 1)) >> 3, 0, WORD_BITS)
        top = lax.shift_right_arithmetic(jnp.full_like(nt, INT_MIN), jnp.maximum(nt, 1) - 1)
        return jnp.where(nt <= 0, 0, top)

    cand = cand & prefix(qcol + 1)
    need = topk - cnt_gt
    sel_ref[...] = greater | cand

    @pl.when(jnp.max(_popcount_rows(cand) - need) > 0)
    def _():
        def jbody(bi, jt):
            cj = jt | jnp.left_shift(jnp.int32(1), idx_bits - 1 - bi)
            return jnp.where(_popcount_rows(cand & prefix(cj)) <= need, cj, jt)
        jt = lax.fori_loop(0, idx_bits, jbody, jnp.zeros((1, Q_BLOCK), I32))
        sel_ref[...] = greater | (cand & prefix(jt))

    def bias_body(c, carry):
        off = pl.multiple_of(c * tk, tk)
        for gi in range(groups_per_chunk):
            row0 = pl.multiple_of((c * groups_per_chunk + gi) * SUBLANES, SUBLANES)
            w = sel_ref[pl.ds(row0, SUBLANES), :]
            for t in range(WORD_BITS):
                dst = pl.multiple_of(off + gi * KEYS_PER_WORD_GROUP + t * SUBLANES, SUBLANES)
                bias_ref[pl.ds(dst, SUBLANES), :] = jnp.where((w << t) < 0, 0.0, -jnp.inf)
        return carry

    lax.fori_loop(0, nch, bias_body, 0)

    n_rep = N_HEADS_A // N_KV_A
    cols = n_rep * Q_BLOCK
    q_all = q_ref[0].reshape(N_HEADS_A * Q_BLOCK, HEAD_DIM)
    scale2 = (HEAD_DIM ** -0.5) * LOG2_E
    m_ref[...] = jnp.full(m_ref.shape, -1e30, F32)
    acc_ref[...] = jnp.zeros(acc_ref.shape, F32)
    ones_rows = jnp.ones((SUM_ROWS, tk), BF16)

    def stage_qk(g, off):
        qg = q_all[g * cols:(g + 1) * cols, :]
        s_ref[g] = _dot_nt(k_ref[0, pl.ds(off, tk), g * HEAD_DIM:(g + 1) * HEAD_DIM], qg)

    def stage_max(g, off):
        for n in range(n_rep):
            csl = slice(n * Q_BLOCK, (n + 1) * Q_BLOCK)
            t = s_ref[g, :, csl] + bias_ref[pl.ds(off, tk), :]
            s_ref[g, :, csl] = t
            m_old = m_ref[g, :, csl]
            m_new = jnp.maximum(m_old, jnp.max(t, axis=0, keepdims=True))
            alpha_ref[g, :, csl] = jnp.exp2((m_old - m_new) * scale2)
            m_ref[g, :, csl] = m_new

    def stage_exp(g, off):
        for n in range(n_rep):
            csl = slice(n * Q_BLOCK, (n + 1) * Q_BLOCK)
            p_ref[g, :, csl] = jnp.exp2((s_ref[g, :, csl] - m_ref[g, 0:1, csl]) * scale2).astype(BF16)

    def stage_pv(g, off):
        v_ext = jnp.concatenate(
            [vt_ref[0, g * HEAD_DIM:(g + 1) * HEAD_DIM, pl.ds(off, tk)], ones_rows], axis=0)
        acc_ref[g] = alpha_ref[g, 0:1, :] * acc_ref[g] + _dot(v_ext, p_ref[g])

    stages = (stage_qk, stage_max, stage_exp, stage_pv)

    def att_body(c, carry):
        off = pl.multiple_of(c * tk, tk)
        for step in range(N_KV_A + len(stages) - 1):
            for g in range(N_KV_A):
                if 0 <= step - g < len(stages):
                    stages[step - g](g, off)
        return carry

    lax.fori_loop(0, nch, att_body, 0)
    for g in range(N_KV_A):
        acc = acc_ref[g]
        o_t = acc[0:HEAD_DIM, :] / acc[HEAD_DIM:HEAD_DIM + 1, :]
        for n in range(n_rep):
            h = g * n_rep + n
            o_ref[:, h * HEAD_DIM:(h + 1) * HEAD_DIM] = jnp.transpose(
                o_t[:, n * Q_BLOCK:(n + 1) * Q_BLOCK]).astype(o_ref.dtype)


def _attention(qh, wi, kkiv, vt, batch, seq_len):
    nb = seq_len // Q_BLOCK
    topk = min(TOPK_MAX, seq_len // 4)
    tk = min(512, seq_len)
    kvd = N_KV_A * HEAD_DIM
    cols = N_HEADS_A // N_KV_A * Q_BLOCK
    single = pl.Buffered(1)
    return pl.pallas_call(
        functools.partial(_attn_kernel, topk=topk, tk=tk, idx_bits=seq_len.bit_length()),
        grid=(batch, nb),
        in_specs=[pl.BlockSpec((1, N_HEADS_A, Q_BLOCK, HEAD_DIM), lambda b, i: (b * nb + i, 0, 0, 0)),
                  pl.BlockSpec((1, N_HEADS_IDX, Q_BLOCK, HEAD_DIM), lambda b, i: (b * nb + i, 1, 0, 0)),
                  pl.BlockSpec((Q_BLOCK, LANES), lambda b, i: (b * nb + i, 0)),
                  pl.BlockSpec((1, seq_len, kvd), lambda b, i: (b, 0, 0), pipeline_mode=single),
                  pl.BlockSpec((1, seq_len, HEAD_DIM), lambda b, i: (b, 0, kvd // HEAD_DIM),
                               pipeline_mode=single),
                  pl.BlockSpec((1, kvd, seq_len), lambda b, i: (b, 0, 0), pipeline_mode=single)],
        out_specs=pl.BlockSpec((Q_BLOCK, N_HEADS_A * HEAD_DIM), lambda b, i: (b * nb + i, 0)),
        out_shape=jax.ShapeDtypeStruct((batch * seq_len, N_HEADS_A * HEAD_DIM), BF16),
        scratch_shapes=[pltpu.VMEM((WORD_BITS, seq_len // WORD_BITS, Q_BLOCK), I32),
                        pltpu.VMEM((seq_len // WORD_BITS, Q_BLOCK), I32),
                        pltpu.VMEM((seq_len, Q_BLOCK), F32),
                        pltpu.VMEM((N_KV_A, SUBLANES, cols), F32),
                        pltpu.VMEM((N_KV_A, HEAD_DIM + SUM_ROWS, cols), F32),
                        pltpu.VMEM((N_KV_A, SUBLANES, cols), F32),
                        pltpu.VMEM((N_KV_A, tk, cols), F32),
                        pltpu.VMEM((N_KV_A, tk, cols), BF16)],
        compiler_params=_cparams(("arbitrary", "arbitrary"), 52),
        name="dsa_attention",
    )(qh, qh, wi, kkiv, kkiv, vt)


def _dot_hp(x, w_bf):
    hi = x.astype(BF16)
    r1 = x - hi.astype(F32)
    mid = r1.astype(BF16)
    lo = (r1 - mid.astype(F32)).astype(BF16)
    n = x.shape[0]
    parts = _dot(jnp.concatenate([hi, mid, lo], axis=0), w_bf)
    return parts[0:n] + parts[n:2 * n] + parts[2 * n:3 * n]


def _rwkv_kernel(r_ref, k_ref, v_ref, lw_ref, a_ref, g_ref, mu_ref, kk_ref, ka_ref, rk_ref,
                 lnw_ref, lnb_ref, bd_ref, ones_ref, ltri_ref, ts_ref, ti_ref, ic_ref, eye_ref,
                 o_ref, st_ref, prev_ref):
    @pl.when(pl.program_id(2) == 0)
    def _():
        st_ref[...] = jnp.zeros_like(st_ref)
        prev_ref[...] = jnp.zeros_like(prev_ref)

    pieces = [_rwkv_tile(tile, row0, r_ref, k_ref, v_ref, lw_ref, a_ref, g_ref, mu_ref, kk_ref, ka_ref,
                         rk_ref, lnw_ref, lnb_ref, bd_ref, ones_ref, ltri_ref, ts_ref, ti_ref, ic_ref,
                         eye_ref, o_ref, st_ref, prev_ref)
              for row0 in range(0, r_ref.shape[0], CHUNK)
              for tile in range(r_ref.shape[1] // TILE_B)]
    while pieces:
        pieces = [t for t in pieces if next(t, "done") != "done"]


def _rwkv_tile(tile, row0, r_ref, k_ref, v_ref, lw_ref, a_ref, g_ref, mu_ref, kk_ref, ka_ref, rk_ref,
               lnw_ref, lnb_ref, bd_ref, ones_ref, ltri_ref, ts_ref, ti_ref, ic_ref, eye_ref,
               o_ref, st_ref, prev_ref):
    cols = slice(tile * TILE_B, (tile + 1) * TILE_B)
    rows = slice(row0, row0 + CHUNK)
    bdm = bd_ref[...]
    row = lax.broadcasted_iota(I32, (CHUNK, TILE_B), 0)

    def shifted(x, slot):
        prev = prev_ref[slot:slot + 1, cols]
        prev_ref[slot:slot + 1, cols] = x[CHUNK - 1:CHUNK, :]
        return jnp.where(row == 0, prev, pltpu.roll(x, 1, axis=0))

    r0 = r_ref[rows, cols]
    k0 = k_ref[rows, cols]
    v0 = v_ref[rows, cols]
    r = r0 + (shifted(r0, 0) - r0) * mu_ref[0:1, cols]
    k = k0 + (shifted(k0, 1) - k0) * mu_ref[1:2, cols]
    v = v0 + (shifted(v0, 2) - v0) * mu_ref[2:3, cols]
    a = a_ref[rows, cols]
    ones_bd = ones_ref[...]

    kk = k * kk_ref[:, cols]
    k2 = k * (1.0 + (a - 1.0) * ka_ref[:, cols])
    head_sums = _dot_hp(jnp.concatenate([kk * kk, r * k2 * rk_ref[:, cols]], axis=0), ones_bd)
    ss = head_sums[0:CHUNK]
    bonus = head_sums[CHUNK:2 * CHUNK] * v
    yield
    kk = kk / jnp.maximum(jnp.sqrt(ss), 1e-12)
    aa = -kk
    bb = kk * a

    lw = lw_ref[rows, cols]
    cs = _cumsum_rows(lw, ltri_ref[...])
    yield
    tot = cs[CHUNK - 1:CHUNK, :]
    e_in = jnp.exp(cs)
    e_out = jnp.exp(-cs)
    e_tail = jnp.exp(tot - cs)
    at = aa * jnp.exp(cs - lw)
    rt = r * e_in
    bt = (bb * e_out)
    kt = (k2 * e_out)
    bh = (bb * e_tail).astype(BF16)
    kh = (k2 * e_tail).astype(BF16)
    wc = jnp.exp(tot)

    def bd(x):
        return jnp.concatenate([x.astype(BF16)] * HEADS_PER_TILE, axis=0) * ones_bd

    lhs = jnp.concatenate([at, rt], axis=0).astype(BF16)
    mb = _dot_nt(lhs, bd(bt))
    mk = _dot_nt(lhs, bd(kt))
    yield
    ts = ts_ref[...]
    ti = ti_ref[...]
    m_ab = mb[0:CHUNK] * ts
    m_rb = (mb[CHUNK:2 * CHUNK] * ti).astype(BF16)
    m_ak = (mk[0:CHUNK] * ts).astype(BF16)
    m_rk = (mk[CHUNK:2 * CHUNK] * ti).astype(BF16)

    n_round = CHUNK.bit_length() - 1
    t_inv = ic_ref[...] + m_ab
    m_pow = _dot(m_ab.astype(BF16), bd(m_ab))
    yield
    for rnd in range(1, n_round):
        last = rnd == n_round - 1
        lhs_rows = [t_inv] if last else [t_inv, m_pow]
        prod = _dot(jnp.concatenate(lhs_rows, axis=0).astype(BF16), bd(m_pow))
        yield
        t_inv = t_inv + prod[0:CHUNK]
        if not last:
            m_pow = prod[CHUNK:2 * CHUNK]
    t_bf = t_inv.astype(BF16)

    p = _dot(t_bf, bd(at))
    uv = _dot(jnp.concatenate([m_ak, m_rk], axis=0), bd(v))
    u = uv[0:CHUNK]
    yield
    q = _dot(t_bf, bd(u))
    rp = rt + _dot(m_rb, bd(p))
    yield
    y0 = _dot(m_rb, bd(q)) + uv[CHUNK:2 * CHUNK]

    yield
    a_t = _dot_tn(bh, p.astype(BF16)) * bdm + eye_ref[...] * wc
    d_t = _dot_tn(jnp.concatenate([bh, kh], axis=0),
                  jnp.concatenate([q.astype(BF16), v.astype(BF16)], axis=0)) * bdm
    st = st_ref[tile]
    from_state = _dot(jnp.concatenate([rp.astype(BF16), a_t.astype(BF16)], axis=0), st.astype(BF16))
    y = from_state[0:CHUNK] + y0
    st_ref[tile] = from_state[CHUNK:CHUNK + TILE_B] + d_t

    yield
    inv_n = 1.0 / HEAD_DIM_B
    mu = _dot_hp(y, ones_bd) * inv_n
    yield
    yc = y - mu
    var = _dot_hp(yc * yc, ones_bd) * inv_n
    yield
    yn =(yc * lax.rsqrt(var + GN_EPS)) * lnw_ref[:, cols] + lnb_ref[:, cols]
    o_ref[rows, cols] = ((yn + bonus) * g_ref[rows, cols]).astype(o_ref.dtype)


def _cumsum_rows(x, ltri_bf):
    hi = x.astype(BF16)
    r1 = x - hi.astype(F32)
    mid = r1.astype(BF16)
    lo = (r1 - mid.astype(F32)).astype(BF16)
    return _dot(ltri_bf, hi) + _dot(ltri_bf, mid) + _dot(ltri_bf, lo)


def _rwkv_consts():
    idx = jnp.arange(TILE_B)
    head = idx // HEAD_DIM_B
    bdm = (head[:, None] == head[None, :])
    t = jnp.arange(CHUNK)[:, None]
    s = (idx % CHUNK)[None, :]
    return dict(
        bd=bdm.astype(F32),
        ones=bdm.astype(BF16),
        ltri=(jnp.arange(CHUNK)[None, :] <= t).astype(BF16),
        ts=(s < t).astype(F32),
        ti=(s <= t).astype(F32),
        ic=(s == t).astype(F32),
        eye=jnp.eye(TILE_B, dtype=F32),
    )


def _rwkv(rkv, lw, a, g, mu_rkv, k_k, k_a, r_k, ln_w, ln_b, batch, seq_len):
    m, db = lw.shape
    block_rows = RWKV_CHUNKS_PER_STEP * CHUNK
    nc = seq_len // block_rows
    width = RWKV_TILES_PER_STEP * TILE_B
    ng = db // width
    cst = _rwkv_consts()
    blk = lambda off: pl.BlockSpec((block_rows, width), lambda b, h, c, off=off: (b * nc + c, off + h))
    par = lambda rows: pl.BlockSpec((rows, width), lambda b, h, c: (0, h))
    full = lambda arr: pl.BlockSpec(arr.shape, lambda b, h, c: (0, 0))
    row = lambda p: p.reshape(1, db)
    return pl.pallas_call(
        _rwkv_kernel,
        grid=(batch, ng, nc),
        in_specs=[blk(0), blk(ng), blk(2 * ng), blk(0), blk(0), blk(0),
                  par(3), par(1), par(1), par(1), par(1), par(1),
                  full(cst["bd"]), full(cst["ones"]), full(cst["ltri"]), full(cst["ts"]),
                  full(cst["ti"]), full(cst["ic"]), full(cst["eye"])],
        out_specs=blk(0),
        out_shape=jax.ShapeDtypeStruct((m, db), BF16),
        scratch_shapes=[pltpu.VMEM((RWKV_TILES_PER_STEP, TILE_B, TILE_B), F32),
                        pltpu.VMEM((SUBLANES, width), F32)],
        compiler_params=_cparams(("arbitrary", "arbitrary", "arbitrary"), 32),
        name="rwkv7",
    )(rkv, rkv, rkv, lw, a, g, mu_rkv, row(k_k), row(k_a), row(r_k), row(ln_w), row(ln_b),
      cst["bd"], cst["ones"], cst["ltri"], cst["ts"], cst["ti"], cst["ic"], cst["eye"])


def _merge_kernel(xn_ref, oa_ref, ob_ref, wga_ref, wgb_ref, bga_ref, bgb_ref, wpa_ref, wpb_ref, o_ref):
    xn = xn_ref[...]
    ga = _sigmoid(_dot(xn, wga_ref[...]) + bga_ref[...])
    gb = _sigmoid(_dot(xn, wgb_ref[...]) + bgb_ref[...])
    ya = _dot(oa_ref[...], wpa_ref[...])
    yb = _dot(ob_ref[...], wpb_ref[...])
    o_ref[...] = (ga * ya + gb * yb).astype(o_ref.dtype)


def _merge(xn, oa, ob, w_gate, b_gate, w_pa, w_pb, bm=1024, bn=256, side=None):
    m, d = xn.shape
    bm = min(bm, m)
    nj = d // bn
    return _matmul_call(
        _merge_kernel, (m // bm, nj),
        [pl.BlockSpec((bm, d), lambda i, j: (i, 0)),
         pl.BlockSpec((bm, oa.shape[1]), lambda i, j: (i, 0)),
         pl.BlockSpec((bm, ob.shape[1]), lambda i, j: (i, 0)),
         pl.BlockSpec((d, bn), lambda i, j: (0, j)),
         pl.BlockSpec((d, bn), lambda i, j: (0, j + nj)),
         pl.BlockSpec((1, bn), lambda i, j: (0, j)),
         pl.BlockSpec((1, bn), lambda i, j: (0, j + nj)),
         pl.BlockSpec((w_pa.shape[0], bn), lambda i, j: (0, j)),
         pl.BlockSpec((w_pb.shape[0], bn), lambda i, j: (0, j))],
        pl.BlockSpec((bm, bn), lambda i, j: (i, j)),
        jax.ShapeDtypeStruct((m, d), BF16),
        (xn, oa, ob, w_gate, w_gate, b_gate, b_gate, w_pa, w_pb), "gated_merge", vmem_mib=56, side=side)


def _mm_res_kernel(x_ref, w_ref, res_ref, o_ref):
    o_ref[...] = res_ref[...] + _dot(x_ref[...], w_ref[...])


def _matmul_residual(x, w, res, bm=1024, bn=1024, name="matmul_res", side=None):
    m, k = x.shape
    n = w.shape[1]
    bm = min(bm, m)
    return _matmul_call(
        _mm_res_kernel, (m // bm, n // bn),
        [pl.BlockSpec((bm, k), lambda i, j: (i, 0)),
         pl.BlockSpec((k, bn), lambda i, j: (0, j)),
         pl.BlockSpec((bm, bn), lambda i, j: (i, j))],
        pl.BlockSpec((bm, bn), lambda i, j: (i, j)),
        jax.ShapeDtypeStruct((m, n), F32), (x, w, res), name, side=side)


def _ffn_up_kernel(x_ref, w1_ref, w3_ref, o_ref):
    x = x_ref[...]
    h1 = _dot(x, w1_ref[...])
    h3 = _dot(x, w3_ref[...])
    o_ref[...] = (h1 * _sigmoid(h1) * h3).astype(o_ref.dtype)


def _ffn_up(x, w1, w3, bm=1024, bn=512, side=None):
    m, k = x.shape
    n = w1.shape[1]
    bm = min(bm, m)
    return _matmul_call(
        _ffn_up_kernel, (m // bm, pl.cdiv(n, bn)),
        [pl.BlockSpec((bm, k), lambda i, j: (i, 0)),
         pl.BlockSpec((k, bn), lambda i, j: (0, j)),
         pl.BlockSpec((k, bn), lambda i, j: (0, j))],
        pl.BlockSpec((bm, bn), lambda i, j: (i, j)),
        jax.ShapeDtypeStruct((m, n), BF16), (x, w1, w3), "ffn_up", side=side)


def _ple_kernel(h_ref, wg_ref, p_ref, wp_ref, gain_ref, o_ref, hn_ref, *, bn, final_norm):
    j = pl.program_id(1)
    n_row_chunks = h_ref.shape[0] // NORM_ROW_CHUNK

    def row_chunk(r):
        return pl.ds(pl.multiple_of(r * NORM_ROW_CHUNK, NORM_ROW_CHUNK), NORM_ROW_CHUNK)

    @pl.when(j == 0)
    def _():
        def body(r, carry):
            hn_ref[row_chunk(r), :] = _rms_rows(h_ref[row_chunk(r), :]).astype(BF16)
            return carry
        lax.fori_loop(0, n_row_chunks, body, 0)

    col = pl.multiple_of(j * bn, bn)
    gate = _sigmoid(_dot(hn_ref[...], wg_ref[...]))
    o_ref[:, pl.ds(col, bn)] = h_ref[:, pl.ds(col, bn)] + gate * _dot(p_ref[...], wp_ref[...])

    if final_norm:
        @pl.when(j == pl.num_programs(1) - 1)
        def _():
            def body(r, carry):
                o_ref[row_chunk(r), :] = _rms_rows(o_ref[row_chunk(r), :]) * gain_ref[...]
                return carry
            lax.fori_loop(0, n_row_chunks, body, 0)


def _ple(h, wg, p, wp, final_gain, bm=512, bn=512):
    m, d = h.shape
    bm = min(bm, m)
    final_norm = final_gain is not None
    gain = (final_gain if final_norm else jnp.ones((d,), F32)).reshape(1, d)
    return pl.pallas_call(
        functools.partial(_ple_kernel, bn=bn, final_norm=final_norm),
        grid=(m // bm, d // bn),
        in_specs=[pl.BlockSpec((bm, d), lambda i, j: (i, 0)),
                  pl.BlockSpec((d, bn), lambda i, j: (0, j)),
                  pl.BlockSpec((bm, p.shape[1]), lambda i, j: (i, 0)),
                  pl.BlockSpec((p.shape[1], bn), lambda i, j: (0, j)),
                  pl.BlockSpec((1, d), lambda i, j: (0, 0))],
        out_specs=pl.BlockSpec((bm, d), lambda i, j: (i, 0)),
        out_shape=jax.ShapeDtypeStruct((m, d), F32),
        scratch_shapes=[pltpu.VMEM((bm, d), BF16)],
        compiler_params=_cparams(("arbitrary", "arbitrary"), 52),
        name="ple",
    )(h, wg, p, wp, gain)


def _rmsnorm_kernel(x_ref, g_ref, o_ref):
    o_ref[...] = (_rms_rows(x_ref[...]) * g_ref[...]).astype(o_ref.dtype)


def _rmsnorm(x, gain, out_dtype, bm=512):
    m, d = x.shape
    bm = min(bm, m)
    row_spec = pl.BlockSpec((bm, d), lambda i: (i, 0))
    return pl.pallas_call(
        _rmsnorm_kernel,
        grid=(m // bm,),
        in_specs=[row_spec, pl.BlockSpec((1, d), lambda i: (0, 0))],
        out_specs=row_spec,
        out_shape=jax.ShapeDtypeStruct((m, d), out_dtype),
        compiler_params=_cparams(("arbitrary",), 48),
        name="rmsnorm",
    )(x, gain.reshape(1, d))


def _pad_to(a, axis, size):
    pad = [(0, 0)] * a.ndim
    pad[axis] = (0, size - a.shape[axis])
    return jnp.pad(a, pad)


def _layer(h2, p2, cos, sin, batch, seq_len, norm_mix, w_in, mu_rkv, mu_wag, w0, w1, w2, a0, a1, a2,
           g1, g2, k_k, k_a, r_k, ln_w, ln_b, w_pa, w_pb, w_gate, b_gate, w_o, norm_ffn,
           w_ffn1, w_ffn3, w_ffn2, w_ple_gate, w_ple, final_gain):
    d = h2.shape[1]
    d_a = N_HEADS_A * HEAD_DIM
    kvd = N_KV_A * HEAD_DIM
    d_iq = N_HEADS_IDX * HEAD_DIM
    d_b = w_pb.shape[0]
    o_q, o_k, o_v = 0, d_a, d_a + kvd
    o_qi = o_v + kvd
    o_ki = o_qi + d_iq
    o_wi = o_ki + HEAD_DIM
    o_r = o_wi + N_HEADS_IDX

    bf = lambda a: a.astype(BF16)
    w_in_bf = bf(w_in)
    assert d_a == d_iq and o_wi % LANES == 0
    w_rkv = w_in_bf[:, o_r:o_r + 3 * d_b]
    lora = LANES
    w1p, a1p = bf(_pad_to(w1, 1, lora)), bf(_pad_to(a1, 1, lora))
    w2p, a2p = bf(_pad_to(w2, 0, lora)), bf(_pad_to(a2, 0, lora))

    xn, lw, a, g, wi = _prep(h2, norm_mix.reshape(1, d), mu_wag, w1p, w2p, w0.reshape(1, d_b),
                             a1p, a2p, a0.reshape(1, d_b), bf(g1), bf(g2), w_in_bf, o_wi // LANES,
                             seq_len)

    qh, w_o_bf = _proj_rope_headmajor(xn, w_in_bf, (o_q, o_qi), d_a, cos, sin, side=w_o)
    kkiv = _proj_kv(xn, w_in_bf, o_k, o_ki, o_v, cos, sin)
    rkv, w_gate_bf = _matmul(xn, w_rkv, F32, bn=512, name="proj_rkv", side=w_gate)

    kkiv = kkiv.reshape(batch, seq_len, -1)
    vt = jnp.swapaxes(kkiv[:, :, kvd + HEAD_DIM:], 1, 2)
    o_att = _attention(qh, wi, kkiv, vt, batch, seq_len)

    o_rwkv = _rwkv(rkv, lw, a, g, mu_rkv, k_k, k_a, r_k.reshape(-1), ln_w, ln_b, batch, seq_len)

    mixed, w_ffn1_bf = _merge(xn, o_att, o_rwkv, w_gate_bf, b_gate.reshape(1, -1), bf(w_pa), bf(w_pb),
                              side=w_ffn1)
    h2, w_ffn3_bf = _matmul_residual(mixed, w_o_bf, h2, bn=512, name="out_proj", side=w_ffn3)

    xf = _rmsnorm(h2, norm_ffn, BF16)
    u, w_ffn2_bf = _ffn_up(xf, w_ffn1_bf, w_ffn3_bf, side=w_ffn2)
    h2, w_ple_gate_bf = _matmul_residual(u, w_ffn2_bf, h2, bm=512, bn=512, name="ffn_down",
                                         side=w_ple_gate)

    return _ple(h2, w_ple_gate_bf, bf(p2), bf(w_ple), final_gain)


def kernel(x, p, positions, norm_mix, w_in, mu_rkv, mu_wag, w0, w1, w2, a0, a1, a2, g1, g2, k_k, k_a,
           r_k, ln_w, ln_b, w_pa, w_pb, w_gate, b_gate, w_o, norm_ffn, w_ffn1, w_ffn3, w_ffn2,
           w_ple_gate, w_ple, norm_final):
    batch, seq_len, d = x.shape
    depth = p.shape[0]
    h2 = x.reshape(batch * seq_len, d)
    cos, sin = _rope_tables(positions)
    for i in range(depth):
        h2 = _layer(h2, p[i].reshape(batch * seq_len, -1), cos, sin, batch, seq_len,
                    norm_mix[i], w_in[i], mu_rkv[i], mu_wag[i], w0[i], w1[i], w2[i], a0[i], a1[i], a2[i],
                    g1[i], g2[i], k_k[i], k_a[i], r_k[i], ln_w[i], ln_b[i], w_pa[i], w_pb[i], w_gate[i],
                    b_gate[i], w_o[i], norm_ffn[i], w_ffn1[i], w_ffn3[i], w_ffn2[i], w_ple_gate[i],
                    w_ple[i], norm_final if i == depth - 1 else None)
    return h2.reshape(batch, seq_len, d)
```

```python
import functools

import jax
import jax.numpy as jnp
from jax import lax
from jax.experimental import pallas as pl
from jax.experimental.pallas import tpu as pltpu

F32 = jnp.float32
BF16 = jnp.bfloat16
I32 = jnp.int32

N_HEADS_A = 16
HEAD_DIM = 128
N_KV_A = 4
N_HEADS_IDX = 16
TOPK_MAX = 256
Q_BLOCK = 128
ROPE_THETA = 10000.0
HEAD_DIM_B = 64
GN_EPS = 64e-5
RMS_EPS = 1e-6

LANES = 128
SUBLANES = 8
BF16_SUBLANES = 16
MXU_DIM = 256

INT_MIN = -2 ** 31
LOG2_E = 1.4426950408889634
NORM_ROW_CHUNK = 64
DECAY_SCALE = 0.6065306597126334

CHUNK = 64
HEADS_PER_TILE = MXU_DIM // HEAD_DIM_B
TILE_B = HEADS_PER_TILE * HEAD_DIM_B
RWKV_CHUNKS_PER_STEP = 4
RWKV_TILES_PER_STEP = 4


def _cparams(sem, vmem_mib):
    return pltpu.CompilerParams(dimension_semantics=sem, vmem_limit_bytes=vmem_mib << 20)


def _sidecar_cast(body, n_in):
    def kern(*refs):
        refs[n_in + 2][...] = refs[n_in][...].astype(BF16)
        body(*refs[:n_in], refs[n_in + 1])
    return kern


def _matmul_call(body, grid, in_specs, out_spec, out_shape, args, name, vmem_mib=52, side=None):
    params = _cparams(("arbitrary", "arbitrary"), vmem_mib)
    if side is None:
        out = pl.pallas_call(body, grid=grid, in_specs=in_specs, out_specs=out_spec,
                             out_shape=out_shape, compiler_params=params, name=name)(*args)
        return out, None
    steps, n_inner = grid[0] * grid[1], grid[1]
    rows = side.shape[0]
    rb = -(-(-(-rows // steps)) // BF16_SUBLANES) * BF16_SUBLANES
    last = -(-rows // rb) - 1
    side_spec = pl.BlockSpec((rb, side.shape[1]), lambda i, j: (jnp.minimum(i * n_inner + j, last), 0))
    return pl.pallas_call(
        _sidecar_cast(body, len(in_specs)), grid=grid, in_specs=[*in_specs, side_spec],
        out_specs=[out_spec, side_spec],
        out_shape=[out_shape, jax.ShapeDtypeStruct(side.shape, BF16)],
        compiler_params=params, name=name)(*args, side)


def _dot(a, b):
    return jnp.dot(a, b, preferred_element_type=F32)


def _dot_nt(a, b):
    return lax.dot_general(a, b, (((1,), (1,)), ((), ())), preferred_element_type=F32)


def _dot_tn(a, b):
    return lax.dot_general(a, b, (((0,), (0,)), ((), ())), preferred_element_type=F32)


def _sigmoid(x):
    return 1.0 / (1.0 + jnp.exp(-x))


def _rms_rows(x):
    return x * lax.rsqrt(jnp.mean(x * x, axis=-1, keepdims=True) + RMS_EPS)


def _rope_tab_kernel(pos_ref, freq_ref, sign_ref, cos_ref, sin_ref):
    ang = pos_ref[...].astype(F32) * freq_ref[...]
    cos_ref[...] = jnp.cos(ang)
    sin_ref[...] = jnp.sin(ang) * sign_ref[...]


def _rope_tables(positions):
    n = positions.size
    half = HEAD_DIM // 2
    inv_freq = ROPE_THETA ** (-jnp.arange(0, HEAD_DIM, 2, dtype=F32) / HEAD_DIM)
    freq2 = jnp.concatenate([inv_freq, inv_freq]).reshape(1, HEAD_DIM)
    sign = jnp.concatenate([-jnp.ones((half,), F32), jnp.ones((half,), F32)]).reshape(1, HEAD_DIM)
    bm = min(2048, n)
    return pl.pallas_call(
        _rope_tab_kernel,
        grid=(n // bm,),
        in_specs=[pl.BlockSpec((bm, 1), lambda i: (i, 0)),
                  pl.BlockSpec((1, HEAD_DIM), lambda i: (0, 0)),
                  pl.BlockSpec((1, HEAD_DIM), lambda i: (0, 0))],
        out_specs=[pl.BlockSpec((bm, HEAD_DIM), lambda i: (i, 0)),
                   pl.BlockSpec((bm, HEAD_DIM), lambda i: (i, 0))],
        out_shape=[jax.ShapeDtypeStruct((n, HEAD_DIM), F32)] * 2,
        compiler_params=_cparams(("arbitrary",), 32),
        name="rope_tables",
    )(positions.reshape(n, 1), freq2, sign)


def _rope(t, cos, sin):
    return t * cos + pltpu.roll(t, HEAD_DIM // 2, axis=1) * sin


def _prep_kernel(x_ref, xp_ref, gain_ref, mu_ref, w1_ref, w2_ref, w0_ref, a1_ref, a2_ref, a0_ref,
                 g1_ref, g2_ref, wwi_ref,
                 xn_ref, lw_ref, a_ref, g_ref, wi_ref, *, seq_len, bm):
    i = pl.program_id(0)
    gain = gain_ref[...]
    xn = _rms_rows(x_ref[...]) * gain
    prev = (_rms_rows(xp_ref[...]) * gain)[SUBLANES - 1:SUBLANES, :]
    prev = jnp.where((i * bm) % seq_len == 0, jnp.zeros_like(prev), prev)
    row = lax.broadcasted_iota(I32, xn.shape, 0)
    sh = jnp.where(row == 0, prev, pltpu.roll(xn, 1, axis=0))
    xx = sh - xn
    xn_bf = xn.astype(BF16)
    xn_ref[...] = xn_bf
    wi_ref[...] = _dot(xn_bf, wwi_ref[...]) * (N_HEADS_IDX ** -0.5)

    xw = (xn + xx * mu_ref[0:1, :]).astype(BF16)
    hw = jnp.tanh(_dot(xw, w1_ref[...])).astype(BF16)
    wl = w0_ref[...] + _dot(hw, w2_ref[...])
    lw_ref[...] = -DECAY_SCALE * _sigmoid(wl)

    xa = (xn + xx * mu_ref[1:2, :]).astype(BF16)
    ha = _dot(xa, a1_ref[...]).astype(BF16)
    a_ref[...] = _sigmoid(a0_ref[...] + _dot(ha, a2_ref[...]))

    xg = (xn + xx * mu_ref[2:3, :]).astype(BF16)
    hg = _sigmoid(_dot(xg, g1_ref[...])).astype(BF16)
    g_ref[...] = _dot(hg, g2_ref[...])


def _prep(x2, gain, mu_wag, w1, w2, w0, a1, a2, a0, g1, g2, w_in, wi_block, seq_len):
    m, d = x2.shape
    db = w2.shape[1]
    bm = min(128, m)
    full = lambda a: pl.BlockSpec(a.shape, lambda i: (0,) * a.ndim)
    nsub = bm // SUBLANES
    return pl.pallas_call(
        functools.partial(_prep_kernel, seq_len=seq_len, bm=bm),
        grid=(m // bm,),
        in_specs=[pl.BlockSpec((bm, d), lambda i: (i, 0)),
                  pl.BlockSpec((SUBLANES, d), lambda i: (jnp.maximum(i * nsub - 1, 0), 0)),
                  full(gain), full(mu_wag), full(w1), full(w2), full(w0), full(a1), full(a2), full(a0),
                  full(g1), full(g2), pl.BlockSpec((d, LANES), lambda i: (0, wi_block))],
        out_specs=[pl.BlockSpec((bm, d), lambda i: (i, 0)),
                   pl.BlockSpec((bm, db), lambda i: (i, 0)),
                   pl.BlockSpec((bm, db), lambda i: (i, 0)),
                   pl.BlockSpec((bm, db), lambda i: (i, 0)),
                   pl.BlockSpec((bm, LANES), lambda i: (i, 0))],
        out_shape=[jax.ShapeDtypeStruct((m, d), BF16),
                   jax.ShapeDtypeStruct((m, db), F32),
                   jax.ShapeDtypeStruct((m, db), F32),
                   jax.ShapeDtypeStruct((m, db), F32),
                   jax.ShapeDtypeStruct((m, LANES), F32)],
        compiler_params=_cparams(("arbitrary",), 48),
        name="prep",
    )(x2, x2, gain, mu_wag, w1, w2, w0, a1, a2, a0, g1, g2, w_in)


def _proj_rope_hm_kernel(x_ref, w_ref, cos_ref, sin_ref, o_ref):
    acc = _dot(x_ref[...], w_ref[...])
    cos = cos_ref[...]
    sin = sin_ref[...]
    bm, bn = acc.shape
    for j in range(bn // HEAD_DIM):
        t = _rope(acc[:, j * HEAD_DIM:(j + 1) * HEAD_DIM], cos, sin).astype(o_ref.dtype)
        for r in range(bm // Q_BLOCK):
            o_ref[r, j] = t[r * Q_BLOCK:(r + 1) * Q_BLOCK, :]


def _proj_rope_headmajor(xn, w, col_starts, width, cos, sin, bm=1024, bn=1024, side=None):
    m, k = xn.shape
    n = width * len(col_starts)
    bm = min(bm, m)
    per = width // bn
    first, second = (c // bn for c in col_starts)

    def w_block(i, j):
        return 0, jnp.where(j < per, first + j, second + j - per)

    return _matmul_call(
        _proj_rope_hm_kernel, (m // bm, n // bn),
        [pl.BlockSpec((bm, k), lambda i, j: (i, 0)),
         pl.BlockSpec((k, bn), w_block),
         pl.BlockSpec((bm, HEAD_DIM), lambda i, j: (i, 0)),
         pl.BlockSpec((bm, HEAD_DIM), lambda i, j: (i, 0))],
        pl.BlockSpec((bm // Q_BLOCK, bn // HEAD_DIM, Q_BLOCK, HEAD_DIM), lambda i, j: (i, j, 0, 0)),
        jax.ShapeDtypeStruct((m // Q_BLOCK, n // HEAD_DIM, Q_BLOCK, HEAD_DIM), BF16),
        (xn, w, cos, sin), "proj_q", side=side)


def _proj_kv_kernel(x_ref, wk_ref, wki_ref, wv_ref, cos_ref, sin_ref, o_ref):
    x = x_ref[...]
    cos = cos_ref[...]
    sin = sin_ref[...]
    col = 0
    for w_ref, rotary in ((wk_ref, True), (wki_ref, True), (wv_ref, False)):
        acc = _dot(x, w_ref[...])
        for j in range(acc.shape[1] // HEAD_DIM):
            t = acc[:, j * HEAD_DIM:(j + 1) * HEAD_DIM]
            if rotary:
                t = _rope(t, cos, sin)
            o_ref[:, col:col + HEAD_DIM] = t.astype(o_ref.dtype)
            col += HEAD_DIM


def _proj_kv(xn, w, col_k, col_ki, col_v, cos, sin, bm=1024):
    m, k = xn.shape
    kvd = N_KV_A * HEAD_DIM
    n = 2 * kvd + HEAD_DIM
    bm = min(bm, m)
    return pl.pallas_call(
        _proj_kv_kernel,
        grid=(m // bm,),
        in_specs=[pl.BlockSpec((bm, k), lambda i: (i, 0)),
                  pl.BlockSpec((k, kvd), lambda i: (0, col_k // kvd)),
                  pl.BlockSpec((k, HEAD_DIM), lambda i: (0, col_ki // HEAD_DIM)),
                  pl.BlockSpec((k, kvd), lambda i: (0, col_v // kvd)),
                  pl.BlockSpec((bm, HEAD_DIM), lambda i: (i, 0)),
                  pl.BlockSpec((bm, HEAD_DIM), lambda i: (i, 0))],
        out_specs=pl.BlockSpec((bm, n), lambda i: (i, 0)),
        out_shape=jax.ShapeDtypeStruct((m, n), BF16),
        compiler_params=_cparams(("arbitrary",), 52),
        name="proj_kv",
    )(xn, w, w, w, cos, sin)


def _mm_kernel(x_ref, w_ref, o_ref):
    o_ref[...] = _dot(x_ref[...], w_ref[...]).astype(o_ref.dtype)


def _matmul(x, w, out_dtype, bm=1024, bn=1024, name="matmul", side=None):
    m, k = x.shape
    n = w.shape[1]
    bm = min(bm, m)
    return _matmul_call(
        _mm_kernel, (m // bm, n // bn),
        [pl.BlockSpec((bm, k), lambda i, j: (i, 0)),
         pl.BlockSpec((k, bn), lambda i, j: (0, j))],
        pl.BlockSpec((bm, bn), lambda i, j: (i, j)),
        jax.ShapeDtypeStruct((m, n), out_dtype), (x, w), name, side=side)


WORD_BITS = 32
SUM_ROWS = 16
KEYS_PER_WORD_GROUP = WORD_BITS * SUBLANES


def _bit_transpose32(words):
    a = list(words)
    j, m = 16, 0x0000FFFF
    while j:
        mask = jnp.int32(m - (1 << 32) if m >= (1 << 31) else m)
        k = 0
        while k < WORD_BITS:
            t = (a[k] ^ lax.shift_right_logical(a[k + j], jnp.full_like(a[k], j))) & mask
            a[k] = a[k] ^ t
            a[k + j] = a[k + j] ^ (t << j)
            k = (k + j + 1) & ~j
        j >>= 1
        m = (m ^ (m << j)) & 0xFFFFFFFF
    return a


def _popcount_rows(words):
    per_sublane = jnp.sum(lax.population_count(words).reshape(-1, SUBLANES, words.shape[1]), axis=0)
    return jnp.sum(per_sublane.astype(F32), axis=0, keepdims=True)


def _attn_kernel(q_ref, qi_ref, wi_ref, k_ref, ki_ref, vt_ref, o_ref,
                 plane_ref, sel_ref, bias_ref, m_ref, acc_ref, alpha_ref, s_ref, p_ref,
                 *, topk, tk, idx_bits):
    i = pl.program_id(1)
    nk = (i + 1) * Q_BLOCK
    nch = (nk + tk - 1) // tk
    n_words = plane_ref.shape[1]
    groups_per_chunk = tk // KEYS_PER_WORD_GROUP

    @pl.when(i == 0)
    def _():
        plane_ref[...] = jnp.zeros_like(plane_ref)

    qi = qi_ref[0].reshape(N_HEADS_IDX * Q_BLOCK, HEAD_DIM)
    wi_t = jnp.transpose(wi_ref[...]) * (HEAD_DIM ** -0.5)
    qpos = i * Q_BLOCK + lax.broadcasted_iota(I32, (tk, Q_BLOCK), 1)
    krow = lax.broadcasted_iota(I32, (tk, Q_BLOCK), 0)

    def score_chunk(c):
        off = pl.multiple_of(c * tk, tk)
        lg = _dot_nt(ki_ref[0, pl.ds(off, tk), :], qi)
        yield
        sc = jnp.zeros((tk, Q_BLOCK), F32)
        for h in range(N_HEADS_IDX):
            sc = sc + wi_t[h:h + 1, :] * jnp.maximum(lg[:, h * Q_BLOCK:(h + 1) * Q_BLOCK], 0.0)
        yield
        bits = pltpu.bitcast(sc, I32)
        key = bits ^ ((bits >> 31) & 0x7FFFFFFF)
        key = jnp.where(key == -1, 0, key)
        ukey = jnp.where(off + krow <= qpos, key ^ INT_MIN, 0)
        for gi in range(groups_per_chunk):
            base = gi * KEYS_PER_WORD_GROUP
            planes = _bit_transpose32(
                [ukey[base + t * SUBLANES:base + (t + 1) * SUBLANES, :] for t in range(WORD_BITS)])
            row0 = pl.multiple_of((c * groups_per_chunk + gi) * SUBLANES, SUBLANES)
            for b in range(WORD_BITS):
                plane_ref[b, pl.ds(row0, SUBLANES), :] = planes[WORD_BITS - 1 - b]
            yield

    def score_chunks(chunks):
        pieces = [score_chunk(c) for c in chunks]
        while pieces:
            pieces = [t for t in pieces if next(t, "done") != "done"]

    def score_pair(j, carry):
        score_chunks((2 * j, 2 * j + 1))
        return carry

    lax.fori_loop(0, nch // 2, score_pair, 0)

    @pl.when(nch % 2 == 1)
    def _():
        score_chunks((nch - 1,))

    cand = jnp.full((n_words, Q_BLOCK), -1, I32)
    greater = jnp.zeros((n_words, Q_BLOCK), I32)
    cnt_gt = jnp.zeros((1, Q_BLOCK), F32)
    for b in range(WORD_BITS - 1, -1, -1):
        ones = cand & plane_ref[b]
        cnt = _popcount_rows(ones)
        take = cnt_gt + cnt >= topk
        greater = jnp.where(take, greater, greater | ones)
        cnt_gt = jnp.where(take, cnt_gt, cnt_gt + cnt)
        cand = jnp.where(take, ones, cand ^ ones)

    word_row = lax.broadcasted_iota(I32, (n_words, Q_BLOCK), 0)
    word_pos = (word_row >> 3) * KEYS_PER_WORD_GROUP + (word_row & (SUBLANES - 1))
    qcol = i * Q_BLOCK + lax.broadcasted_iota(I32, (n_words, Q_BLOCK), 1)

    def prefix(limit):
        nt = jnp.clip((limit - word_pos + (SUBLANES - 1)) >> 3, 0, WORD_BITS)
        top = lax.shift_right_arithmetic(jnp.full_like(nt, INT_MIN), jnp.maximum(nt, 1) - 1)
        return jnp.where(nt <= 0, 0, top)

    cand = cand & prefix(qcol + 1)
    need = topk - cnt_gt
    sel_ref[...] = greater | cand

    @pl.when(jnp.max(_popcount_rows(cand) - need) > 0)
    def _():
        def jbody(bi, jt):
            cj = jt | jnp.left_shift(jnp.int32(1), idx_bits - 1 - bi)
            return jnp.where(_popcount_rows(cand & prefix(cj)) <= need, cj, jt)
        jt = lax.fori_loop(0, idx_bits, jbody, jnp.zeros((1, Q_BLOCK), I32))
        sel_ref[...] = greater | (cand & prefix(jt))

    def bias_body(c, carry):
        off = pl.multiple_of(c * tk, tk)
        for gi in range(groups_per_chunk):
            row0 = pl.multiple_of((c * groups_per_chunk + gi) * SUBLANES, SUBLANES)
            w = sel_ref[pl.ds(row0, SUBLANES), :]
            for t in range(WORD_BITS):
                dst = pl.multiple_of(off + gi * KEYS_PER_WORD_GROUP + t * SUBLANES, SUBLANES)
                bias_ref[pl.ds(dst, SUBLANES), :] = jnp.where((w << t) < 0, 0.0, -jnp.inf)
        return carry

    lax.fori_loop(0, nch, bias_body, 0)

    n_rep = N_HEADS_A // N_KV_A
    cols = n_rep * Q_BLOCK
    q_all = q_ref[0].reshape(N_HEADS_A * Q_BLOCK, HEAD_DIM)
    scale2 = (HEAD_DIM ** -0.5) * LOG2_E
    m_ref[...] = jnp.full(m_ref.shape, -1e30, F32)
    acc_ref[...] = jnp.zeros(acc_ref.shape, F32)
    ones_rows = jnp.ones((SUM_ROWS, tk), BF16)

    def stage_qk(g, off):
        qg = q_all[g * cols:(g + 1) * cols, :]
        s_ref[g] = _dot_nt(k_ref[0, pl.ds(off, tk), g * HEAD_DIM:(g + 1) * HEAD_DIM], qg)

    def stage_max(g, off):
        for n in range(n_rep):
            csl = slice(n * Q_BLOCK, (n + 1) * Q_BLOCK)
            t = s_ref[g, :, csl] + bias_ref[pl.ds(off, tk), :]
            s_ref[g, :, csl] = t
            m_old = m_ref[g, :, csl]
            m_new = jnp.maximum(m_old, jnp.max(t, axis=0, keepdims=True))
            alpha_ref[g, :, csl] = jnp.exp2((m_old - m_new) * scale2)
            m_ref[g, :, csl] = m_new

    def stage_exp(g, off):
        for n in range(n_rep):
            csl = slice(n * Q_BLOCK, (n + 1) * Q_BLOCK)
            p_ref[g, :, csl] = jnp.exp2((s_ref[g, :, csl] - m_ref[g, 0:1, csl]) * scale2).astype(BF16)

    def stage_pv(g, off):
        v_ext = jnp.concatenate(
            [vt_ref[0, g * HEAD_DIM:(g + 1) * HEAD_DIM, pl.ds(off, tk)], ones_rows], axis=0)
        acc_ref[g] = alpha_ref[g, 0:1, :] * acc_ref[g] + _dot(v_ext, p_ref[g])

    stages = (stage_qk, stage_max, stage_exp, stage_pv)

    def attend_chunks(chunks):
        pieces = [(g, pl.multiple_of(c * tk, tk)) for c in chunks for g in range(N_KV_A)]
        for step in range(len(pieces) + len(stages) - 1):
            for k, (g, off) in enumerate(pieces):
                if 0 <= step - k < len(stages):
                    stages[step - k](g, off)

    def att_pair(j, carry):
        attend_chunks((2 * j, 2 * j + 1))
        return carry

    lax.fori_loop(0, nch // 2, att_pair, 0)

    @pl.when(nch % 2 == 1)
    def _():
        attend_chunks((nch - 1,))
    for g in range(N_KV_A):
        acc = acc_ref[g]
        o_t = acc[0:HEAD_DIM, :] / acc[HEAD_DIM:HEAD_DIM + 1, :]
        for n in range(n_rep):
            h = g * n_rep + n
            o_ref[:, h * HEAD_DIM:(h + 1) * HEAD_DIM] = jnp.transpose(
                o_t[:, n * Q_BLOCK:(n + 1) * Q_BLOCK]).astype(o_ref.dtype)


def _attention(qh, wi, kkiv, vt, batch, seq_len):
    nb = seq_len // Q_BLOCK
    topk = min(TOPK_MAX, seq_len // 4)
    tk = min(512, seq_len)
    kvd = N_KV_A * HEAD_DIM
    cols = N_HEADS_A // N_KV_A * Q_BLOCK
    single = pl.Buffered(1)
    return pl.pallas_call(
        functools.partial(_attn_kernel, topk=topk, tk=tk, idx_bits=seq_len.bit_length()),
        grid=(batch, nb),
        in_specs=[pl.BlockSpec((1, N_HEADS_A, Q_BLOCK, HEAD_DIM), lambda b, i: (b * nb + i, 0, 0, 0)),
                  pl.BlockSpec((1, N_HEADS_IDX, Q_BLOCK, HEAD_DIM), lambda b, i: (b * nb + i, 1, 0, 0)),
                  pl.BlockSpec((Q_BLOCK, LANES), lambda b, i: (b * nb + i, 0)),
                  pl.BlockSpec((1, seq_len, kvd), lambda b, i: (b, 0, 0), pipeline_mode=single),
                  pl.BlockSpec((1, seq_len, HEAD_DIM), lambda b, i: (b, 0, kvd // HEAD_DIM),
                               pipeline_mode=single),
                  pl.BlockSpec((1, kvd, seq_len), lambda b, i: (b, 0, 0), pipeline_mode=single)],
        out_specs=pl.BlockSpec((Q_BLOCK, N_HEADS_A * HEAD_DIM), lambda b, i: (b * nb + i, 0)),
        out_shape=jax.ShapeDtypeStruct((batch * seq_len, N_HEADS_A * HEAD_DIM), BF16),
        scratch_shapes=[pltpu.VMEM((WORD_BITS, seq_len // WORD_BITS, Q_BLOCK), I32),
                        pltpu.VMEM((seq_len // WORD_BITS, Q_BLOCK), I32),
                        pltpu.VMEM((seq_len, Q_BLOCK), F32),
                        pltpu.VMEM((N_KV_A, SUBLANES, cols), F32),
                        pltpu.VMEM((N_KV_A, HEAD_DIM + SUM_ROWS, cols), F32),
                        pltpu.VMEM((N_KV_A, SUBLANES, cols), F32),
                        pltpu.VMEM((N_KV_A, tk, cols), F32),
                        pltpu.VMEM((N_KV_A, tk, cols), BF16)],
        compiler_params=_cparams(("arbitrary", "arbitrary"), 52),
        name="dsa_attention",
    )(qh, qh, wi, kkiv, kkiv, vt)


def _dot_hp(x, w_bf):
    hi = x.astype(BF16)
    r1 = x - hi.astype(F32)
    mid = r1.astype(BF16)
    lo = (r1 - mid.astype(F32)).astype(BF16)
    n = x.shape[0]
    parts = _dot(jnp.concatenate([hi, mid, lo], axis=0), w_bf)
    return parts[0:n] + parts[n:2 * n] + parts[2 * n:3 * n]


def _rwkv_kernel(r_ref, k_ref, v_ref, lw_ref, a_ref, g_ref, mu_ref, kk_ref, ka_ref, rk_ref,
                 lnw_ref, lnb_ref, bd_ref, ones_ref, ltri_ref, ts_ref, ti_ref, ic_ref, eye_ref,
                 o_ref, st_ref, prev_ref):
    @pl.when(pl.program_id(2) == 0)
    def _():
        st_ref[...] = jnp.zeros_like(st_ref)
        prev_ref[...] = jnp.zeros_like(prev_ref)

    pieces = [_rwkv_tile(tile, row0, r_ref, k_ref, v_ref, lw_ref, a_ref, g_ref, mu_ref, kk_ref, ka_ref,
                         rk_ref, lnw_ref, lnb_ref, bd_ref, ones_ref, ltri_ref, ts_ref, ti_ref, ic_ref,
                         eye_ref, o_ref, st_ref, prev_ref)
              for row0 in range(0, r_ref.shape[0], CHUNK)
              for tile in range(r_ref.shape[1] // TILE_B)]
    while pieces:
        pieces = [t for t in pieces if next(t, "done") != "done"]


def _rwkv_tile(tile, row0, r_ref, k_ref, v_ref, lw_ref, a_ref, g_ref, mu_ref, kk_ref, ka_ref, rk_ref,
               lnw_ref, lnb_ref, bd_ref, ones_ref, ltri_ref, ts_ref, ti_ref, ic_ref, eye_ref,
               o_ref, st_ref, prev_ref):
    cols = slice(tile * TILE_B, (tile + 1) * TILE_B)
    rows = slice(row0, row0 + CHUNK)
    bdm = bd_ref[...]
    row = lax.broadcasted_iota(I32, (CHUNK, TILE_B), 0)

    def shifted(x, slot):
        prev = prev_ref[slot:slot + 1, cols]
        prev_ref[slot:slot + 1, cols] = x[CHUNK - 1:CHUNK, :]
        return jnp.where(row == 0, prev, pltpu.roll(x, 1, axis=0))

    r0 = r_ref[rows, cols]
    k0 = k_ref[rows, cols]
    v0 = v_ref[rows, cols]
    r = r0 + (shifted(r0, 0) - r0) * mu_ref[0:1, cols]
    k = k0 + (shifted(k0, 1) - k0) * mu_ref[1:2, cols]
    v = v0 + (shifted(v0, 2) - v0) * mu_ref[2:3, cols]
    a = a_ref[rows, cols]
    ones_bd = ones_ref[...]

    kk = k * kk_ref[:, cols]
    k2 = k * (1.0 + (a - 1.0) * ka_ref[:, cols])
    head_sums = _dot_hp(jnp.concatenate([kk * kk, r * k2 * rk_ref[:, cols]], axis=0), ones_bd)
    ss = head_sums[0:CHUNK]
    bonus = head_sums[CHUNK:2 * CHUNK] * v
    yield
    kk = kk / jnp.maximum(jnp.sqrt(ss), 1e-12)
    aa = -kk
    bb = kk * a

    lw = lw_ref[rows, cols]
    cs = _cumsum_rows(lw, ltri_ref[...])
    yield
    tot = cs[CHUNK - 1:CHUNK, :]
    e_in = jnp.exp(cs)
    e_out = jnp.exp(-cs)
    e_tail = jnp.exp(tot - cs)
    at = aa * jnp.exp(cs - lw)
    rt = r * e_in
    bt = (bb * e_out)
    kt = (k2 * e_out)
    bh = (bb * e_tail).astype(BF16)
    kh = (k2 * e_tail).astype(BF16)
    wc = jnp.exp(tot)

    def bd(x):
        return jnp.concatenate([x.astype(BF16)] * HEADS_PER_TILE, axis=0) * ones_bd

    lhs = jnp.concatenate([at, rt], axis=0).astype(BF16)
    mb = _dot_nt(lhs, bd(bt))
    mk = _dot_nt(lhs, bd(kt))
    yield
    ts = ts_ref[...]
    ti = ti_ref[...]
    m_ab = mb[0:CHUNK] * ts
    m_rb = (mb[CHUNK:2 * CHUNK] * ti).astype(BF16)
    m_ak = (mk[0:CHUNK] * ts).astype(BF16)
    m_rk = (mk[CHUNK:2 * CHUNK] * ti).astype(BF16)

    n_round = CHUNK.bit_length() - 1
    t_inv = ic_ref[...] + m_ab
    m_pow = _dot(m_ab.astype(BF16), bd(m_ab))
    yield
    for rnd in range(1, n_round):
        last = rnd == n_round - 1
        lhs_rows = [t_inv] if last else [t_inv, m_pow]
        prod = _dot(jnp.concatenate(lhs_rows, axis=0).astype(BF16), bd(m_pow))
        yield
        t_inv = t_inv + prod[0:CHUNK]
        if not last:
            m_pow = prod[CHUNK:2 * CHUNK]
    t_bf = t_inv.astype(BF16)

    p = _dot(t_bf, bd(at))
    uv = _dot(jnp.concatenate([m_ak, m_rk], axis=0), bd(v))
    u = uv[0:CHUNK]
    yield
    q = _dot(t_bf, bd(u))
    rp = rt + _dot(m_rb, bd(p))
    yield
    y0 = _dot(m_rb, bd(q)) + uv[CHUNK:2 * CHUNK]

    yield
    a_t = _dot_tn(bh, p.astype(BF16)) * bdm + eye_ref[...] * wc
    d_t = _dot_tn(jnp.concatenate([bh, kh], axis=0),
                  jnp.concatenate([q.astype(BF16), v.astype(BF16)], axis=0)) * bdm
    st = st_ref[tile]
    from_state = _dot(jnp.concatenate([rp.astype(BF16), a_t.astype(BF16)], axis=0), st.astype(BF16))
    y = from_state[0:CHUNK] + y0
    st_ref[tile] = from_state[CHUNK:CHUNK + TILE_B] + d_t

    yield
    inv_n = 1.0 / HEAD_DIM_B
    mu = _dot_hp(y, ones_bd) * inv_n
    yield
    yc = y - mu
    var = _dot_hp(yc * yc, ones_bd) * inv_n
    yield
    yn =(yc * lax.rsqrt(var + GN_EPS)) * lnw_ref[:, cols] + lnb_ref[:, cols]
    o_ref[rows, cols] = ((yn + bonus) * g_ref[rows, cols]).astype(o_ref.dtype)


def _cumsum_rows(x, ltri_bf):
    hi = x.astype(BF16)
    r1 = x - hi.astype(F32)
    mid = r1.astype(BF16)
    lo = (r1 - mid.astype(F32)).astype(BF16)
    return _dot(ltri_bf, hi) + _dot(ltri_bf, mid) + _dot(ltri_bf, lo)


def _rwkv_consts():
    idx = jnp.arange(TILE_B)
    head = idx // HEAD_DIM_B
    bdm = (head[:, None] == head[None, :])
    t = jnp.arange(CHUNK)[:, None]
    s = (idx % CHUNK)[None, :]
    return dict(
        bd=bdm.astype(F32),
        ones=bdm.astype(BF16),
        ltri=(jnp.arange(CHUNK)[None, :] <= t).astype(BF16),
        ts=(s < t).astype(F32),
        ti=(s <= t).astype(F32),
        ic=(s == t).astype(F32),
        eye=jnp.eye(TILE_B, dtype=F32),
    )


def _rwkv(rkv, lw, a, g, mu_rkv, k_k, k_a, r_k, ln_w, ln_b, batch, seq_len):
    m, db = lw.shape
    block_rows = RWKV_CHUNKS_PER_STEP * CHUNK
    nc = seq_len // block_rows
    width = RWKV_TILES_PER_STEP * TILE_B
    ng = db // width
    cst = _rwkv_consts()
    blk = lambda off: pl.BlockSpec((block_rows, width), lambda b, h, c, off=off: (b * nc + c, off + h))
    par = lambda rows: pl.BlockSpec((rows, width), lambda b, h, c: (0, h))
    full = lambda arr: pl.BlockSpec(arr.shape, lambda b, h, c: (0, 0))
    row = lambda p: p.reshape(1, db)
    return pl.pallas_call(
        _rwkv_kernel,
        grid=(batch, ng, nc),
        in_specs=[blk(0), blk(ng), blk(2 * ng), blk(0), blk(0), blk(0),
                  par(3), par(1), par(1), par(1), par(1), par(1),
                  full(cst["bd"]), full(cst["ones"]), full(cst["ltri"]), full(cst["ts"]),
                  full(cst["ti"]), full(cst["ic"]), full(cst["eye"])],
        out_specs=blk(0),
        out_shape=jax.ShapeDtypeStruct((m, db), BF16),
        scratch_shapes=[pltpu.VMEM((RWKV_TILES_PER_STEP, TILE_B, TILE_B), F32),
                        pltpu.VMEM((SUBLANES, width), F32)],
        compiler_params=_cparams(("arbitrary", "arbitrary", "arbitrary"), 32),
        name="rwkv7",
    )(rkv, rkv, rkv, lw, a, g, mu_rkv, row(k_k), row(k_a), row(r_k), row(ln_w), row(ln_b),
      cst["bd"], cst["ones"], cst["ltri"], cst["ts"], cst["ti"], cst["ic"], cst["eye"])


def _merge_kernel(xn_ref, oa_ref, ob_ref, wga_ref, wgb_ref, bga_ref, bgb_ref, wpa_ref, wpb_ref, o_ref):
    xn = xn_ref[...]
    ga = _sigmoid(_dot(xn, wga_ref[...]) + bga_ref[...])
    gb = _sigmoid(_dot(xn, wgb_ref[...]) + bgb_ref[...])
    ya = _dot(oa_ref[...], wpa_ref[...])
    yb = _dot(ob_ref[...], wpb_ref[...])
    o_ref[...] = (ga * ya + gb * yb).astype(o_ref.dtype)


def _merge(xn, oa, ob, w_gate, b_gate, w_pa, w_pb, bm=1024, bn=256, side=None):
    m, d = xn.shape
    bm = min(bm, m)
    nj = d // bn
    return _matmul_call(
        _merge_kernel, (m // bm, nj),
        [pl.BlockSpec((bm, d), lambda i, j: (i, 0)),
         pl.BlockSpec((bm, oa.shape[1]), lambda i, j: (i, 0)),
         pl.BlockSpec((bm, ob.shape[1]), lambda i, j: (i, 0)),
         pl.BlockSpec((d, bn), lambda i, j: (0, j)),
         pl.BlockSpec((d, bn), lambda i, j: (0, j + nj)),
         pl.BlockSpec((1, bn), lambda i, j: (0, j)),
         pl.BlockSpec((1, bn), lambda i, j: (0, j + nj)),
         pl.BlockSpec((w_pa.shape[0], bn), lambda i, j: (0, j)),
         pl.BlockSpec((w_pb.shape[0], bn), lambda i, j: (0, j))],
        pl.BlockSpec((bm, bn), lambda i, j: (i, j)),
        jax.ShapeDtypeStruct((m, d), BF16),
        (xn, oa, ob, w_gate, w_gate, b_gate, b_gate, w_pa, w_pb), "gated_merge", vmem_mib=56, side=side)


def _mm_res_kernel(x_ref, w_ref, res_ref, o_ref):
    o_ref[...] = res_ref[...] + _dot(x_ref[...], w_ref[...])


def _matmul_residual(x, w, res, bm=1024, bn=1024, name="matmul_res", side=None):
    m, k = x.shape
    n = w.shape[1]
    bm = min(bm, m)
    return _matmul_call(
        _mm_res_kernel, (m // bm, n // bn),
        [pl.BlockSpec((bm, k), lambda i, j: (i, 0)),
         pl.BlockSpec((k, bn), lambda i, j: (0, j)),
         pl.BlockSpec((bm, bn), lambda i, j: (i, j))],
        pl.BlockSpec((bm, bn), lambda i, j: (i, j)),
        jax.ShapeDtypeStruct((m, n), F32), (x, w, res), name, side=side)


def _ffn_up_kernel(x_ref, w1_ref, w3_ref, o_ref):
    x = x_ref[...]
    h1 = _dot(x, w1_ref[...])
    h3 = _dot(x, w3_ref[...])
    o_ref[...] = (h1 * _sigmoid(h1) * h3).astype(o_ref.dtype)


def _ffn_up(x, w1, w3, bm=1024, bn=512, side=None):
    m, k = x.shape
    n = w1.shape[1]
    bm = min(bm, m)
    return _matmul_call(
        _ffn_up_kernel, (m // bm, pl.cdiv(n, bn)),
        [pl.BlockSpec((bm, k), lambda i, j: (i, 0)),
         pl.BlockSpec((k, bn), lambda i, j: (0, j)),
         pl.BlockSpec((k, bn), lambda i, j: (0, j))],
        pl.BlockSpec((bm, bn), lambda i, j: (i, j)),
        jax.ShapeDtypeStruct((m, n), BF16), (x, w1, w3), "ffn_up", side=side)


def _ple_kernel(h_ref, wg_ref, p_ref, wp_ref, gain_ref, o_ref, hn_ref, *, bn, final_norm):
    j = pl.program_id(1)
    n_row_chunks = h_ref.shape[0] // NORM_ROW_CHUNK

    def row_chunk(r):
        return pl.ds(pl.multiple_of(r * NORM_ROW_CHUNK, NORM_ROW_CHUNK), NORM_ROW_CHUNK)

    @pl.when(j == 0)
    def _():
        def body(r, carry):
            hn_ref[row_chunk(r), :] = _rms_rows(h_ref[row_chunk(r), :]).astype(BF16)
            return carry
        lax.fori_loop(0, n_row_chunks, body, 0)

    col = pl.multiple_of(j * bn, bn)
    gate = _sigmoid(_dot(hn_ref[...], wg_ref[...]))
    o_ref[:, pl.ds(col, bn)] = h_ref[:, pl.ds(col, bn)] + gate * _dot(p_ref[...], wp_ref[...])

    if final_norm:
        @pl.when(j == pl.num_programs(1) - 1)
        def _():
            def body(r, carry):
                o_ref[row_chunk(r), :] = _rms_rows(o_ref[row_chunk(r), :]) * gain_ref[...]
                return carry
            lax.fori_loop(0, n_row_chunks, body, 0)


def _ple(h, wg, p, wp, final_gain, bm=512, bn=512):
    m, d = h.shape
    bm = min(bm, m)
    final_norm = final_gain is not None
    gain = (final_gain if final_norm else jnp.ones((d,), F32)).reshape(1, d)
    return pl.pallas_call(
        functools.partial(_ple_kernel, bn=bn, final_norm=final_norm),
        grid=(m // bm, d // bn),
        in_specs=[pl.BlockSpec((bm, d), lambda i, j: (i, 0)),
                  pl.BlockSpec((d, bn), lambda i, j: (0, j)),
                  pl.BlockSpec((bm, p.shape[1]), lambda i, j: (i, 0)),
                  pl.BlockSpec((p.shape[1], bn), lambda i, j: (0, j)),
                  pl.BlockSpec((1, d), lambda i, j: (0, 0))],
        out_specs=pl.BlockSpec((bm, d), lambda i, j: (i, 0)),
        out_shape=jax.ShapeDtypeStruct((m, d), F32),
        scratch_shapes=[pltpu.VMEM((bm, d), BF16)],
        compiler_params=_cparams(("arbitrary", "arbitrary"), 52),
        name="ple",
    )(h, wg, p, wp, gain)


def _rmsnorm_kernel(x_ref, g_ref, o_ref):
    o_ref[...] = (_rms_rows(x_ref[...]) * g_ref[...]).astype(o_ref.dtype)


def _rmsnorm(x, gain, out_dtype, bm=512):
    m, d = x.shape
    bm = min(bm, m)
    row_spec = pl.BlockSpec((bm, d), lambda i: (i, 0))
    return pl.pallas_call(
        _rmsnorm_kernel,
        grid=(m // bm,),
        in_specs=[row_spec, pl.BlockSpec((1, d), lambda i: (0, 0))],
        out_specs=row_spec,
        out_shape=jax.ShapeDtypeStruct((m, d), out_dtype),
        compiler_params=_cparams(("arbitrary",), 48),
        name="rmsnorm",
    )(x, gain.reshape(1, d))


def _pad_to(a, axis, size):
    pad = [(0, 0)] * a.ndim
    pad[axis] = (0, size - a.shape[axis])
    return jnp.pad(a, pad)


def _layer(h2, p2, cos, sin, batch, seq_len, norm_mix, w_in, mu_rkv, mu_wag, w0, w1, w2, a0, a1, a2,
           g1, g2, k_k, k_a, r_k, ln_w, ln_b, w_pa, w_pb, w_gate, b_gate, w_o, norm_ffn,
           w_ffn1, w_ffn3, w_ffn2, w_ple_gate, w_ple, final_gain):
    d = h2.shape[1]
    d_a = N_HEADS_A * HEAD_DIM
    kvd = N_KV_A * HEAD_DIM
    d_iq = N_HEADS_IDX * HEAD_DIM
    d_b = w_pb.shape[0]
    o_q, o_k, o_v = 0, d_a, d_a + kvd
    o_qi = o_v + kvd
    o_ki = o_qi + d_iq
    o_wi = o_ki + HEAD_DIM
    o_r = o_wi + N_HEADS_IDX

    bf = lambda a: a.astype(BF16)
    w_in_bf = bf(w_in)
    assert d_a == d_iq and o_wi % LANES == 0
    w_rkv = w_in_bf[:, o_r:o_r + 3 * d_b]
    lora = LANES
    w1p, a1p = bf(_pad_to(w1, 1, lora)), bf(_pad_to(a1, 1, lora))
    w2p, a2p = bf(_pad_to(w2, 0, lora)), bf(_pad_to(a2, 0, lora))

    xn, lw, a, g, wi = _prep(h2, norm_mix.reshape(1, d), mu_wag, w1p, w2p, w0.reshape(1, d_b),
                             a1p, a2p, a0.reshape(1, d_b), bf(g1), bf(g2), w_in_bf, o_wi // LANES,
                             seq_len)

    qh, w_o_bf = _proj_rope_headmajor(xn, w_in_bf, (o_q, o_qi), d_a, cos, sin, side=w_o)
    kkiv = _proj_kv(xn, w_in_bf, o_k, o_ki, o_v, cos, sin)
    rkv, w_gate_bf = _matmul(xn, w_rkv, F32, bn=512, name="proj_rkv", side=w_gate)

    kkiv = kkiv.reshape(batch, seq_len, -1)
    vt = jnp.swapaxes(kkiv[:, :, kvd + HEAD_DIM:], 1, 2)
    o_att = _attention(qh, wi, kkiv, vt, batch, seq_len)

    o_rwkv = _rwkv(rkv, lw, a, g, mu_rkv, k_k, k_a, r_k.reshape(-1), ln_w, ln_b, batch, seq_len)

    mixed, w_ffn1_bf = _merge(xn, o_att, o_rwkv, w_gate_bf, b_gate.reshape(1, -1), bf(w_pa), bf(w_pb),
                              side=w_ffn1)
    h2, w_ffn3_bf = _matmul_residual(mixed, w_o_bf, h2, bn=512, name="out_proj", side=w_ffn3)

    xf = _rmsnorm(h2, norm_ffn, BF16)
    u, w_ffn2_bf = _ffn_up(xf, w_ffn1_bf, w_ffn3_bf, side=w_ffn2)
    h2, w_ple_gate_bf = _matmul_residual(u, w_ffn2_bf, h2, bm=512, bn=512, name="ffn_down",
                                         side=w_ple_gate)

    return _ple(h2, w_ple_gate_bf, bf(p2), bf(w_ple), final_gain)


def kernel(x, p, positions, norm_mix, w_in, mu_rkv, mu_wag, w0, w1, w2, a0, a1, a2, g1, g2, k_k, k_a,
           r_k, ln_w, ln_b, w_pa, w_pb, w_gate, b_gate, w_o, norm_ffn, w_ffn1, w_ffn3, w_ffn2,
           w_ple_gate, w_ple, norm_final):
    batch, seq_len, d = x.shape
    depth = p.shape[0]
    h2 = x.reshape(batch * seq_len, d)
    cos, sin = _rope_tables(positions)
    for i in range(depth):
        h2 = _layer(h2, p[i].reshape(batch * seq_len, -1), cos, sin, batch, seq_len,
                    norm_mix[i], w_in[i], mu_rkv[i], mu_wag[i], w0[i], w1[i], w2[i], a0[i], a1[i], a2[i],
                    g1[i], g2[i], k_k[i], k_a[i], r_k[i], ln_w[i], ln_b[i], w_pa[i], w_pb[i], w_gate[i],
                    b_gate[i], w_o[i], norm_ffn[i], w_ffn1[i], w_ffn3[i], w_ffn2[i], w_ple_gate[i],
                    w_ple[i], norm_final if i == depth - 1 else None)
    return h2.reshape(batch, seq_len, d)
```

```python
import functools

import jax
import jax.numpy as jnp
from jax import lax
from jax.experimental import pallas as pl
from jax.experimental.pallas import tpu as pltpu

F32 = jnp.float32
BF16 = jnp.bfloat16
I32 = jnp.int32

N_HEADS_A = 16
HEAD_DIM = 128
N_KV_A = 4
N_HEADS_IDX = 16
TOPK_MAX = 256
Q_BLOCK = 128
ROPE_THETA = 10000.0
HEAD_DIM_B = 64
GN_EPS = 64e-5
RMS_EPS = 1e-6

LANES = 128
SUBLANES = 8
BF16_SUBLANES = 16
MXU_DIM = 256

INT_MIN = -2 ** 31
LOG2_E = 1.4426950408889634
NORM_ROW_CHUNK = 64
DECAY_SCALE = 0.6065306597126334

CHUNK = 64
HEADS_PER_TILE = MXU_DIM // HEAD_DIM_B
TILE_B = HEADS_PER_TILE * HEAD_DIM_B
RWKV_CHUNKS_PER_STEP = 4
RWKV_TILES_PER_STEP = 4


def _cparams(sem, vmem_mib):
    return pltpu.CompilerParams(dimension_semantics=sem, vmem_limit_bytes=vmem_mib << 20)


def _sidecar_cast(body, n_in):
    def kern(*refs):
        refs[n_in + 2][...] = refs[n_in][...].astype(BF16)
        body(*refs[:n_in], refs[n_in + 1])
    return kern


def _matmul_call(body, grid, in_specs, out_spec, out_shape, args, name, vmem_mib=52, side=None):
    params = _cparams(("arbitrary", "arbitrary"), vmem_mib)
    if side is None:
        out = pl.pallas_call(body, grid=grid, in_specs=in_specs, out_specs=out_spec,
                             out_shape=out_shape, compiler_params=params, name=name)(*args)
        return out, None
    steps, n_inner = grid[0] * grid[1], grid[1]
    rows = side.shape[0]
    rb = -(-(-(-rows // steps)) // BF16_SUBLANES) * BF16_SUBLANES
    last = -(-rows // rb) - 1
    side_spec = pl.BlockSpec((rb, side.shape[1]), lambda i, j: (jnp.minimum(i * n_inner + j, last), 0))
    return pl.pallas_call(
        _sidecar_cast(body, len(in_specs)), grid=grid, in_specs=[*in_specs, side_spec],
        out_specs=[out_spec, side_spec],
        out_shape=[out_shape, jax.ShapeDtypeStruct(side.shape, BF16)],
        compiler_params=params, name=name)(*args, side)


def _dot(a, b):
    return jnp.dot(a, b, preferred_element_type=F32)


def _dot_nt(a, b):
    return lax.dot_general(a, b, (((1,), (1,)), ((), ())), preferred_element_type=F32)


def _dot_tn(a, b):
    return lax.dot_general(a, b, (((0,), (0,)), ((), ())), preferred_element_type=F32)


def _sigmoid(x):
    return 1.0 / (1.0 + jnp.exp(-x))


def _rms_rows(x):
    return x * lax.rsqrt(jnp.mean(x * x, axis=-1, keepdims=True) + RMS_EPS)


def _rope_tab_kernel(pos_ref, freq_ref, sign_ref, cos_ref, sin_ref):
    ang = pos_ref[...].astype(F32) * freq_ref[...]
    cos_ref[...] = jnp.cos(ang)
    sin_ref[...] = jnp.sin(ang) * sign_ref[...]


def _rope_tables(positions):
    n = positions.size
    half = HEAD_DIM // 2
    inv_freq = ROPE_THETA ** (-jnp.arange(0, HEAD_DIM, 2, dtype=F32) / HEAD_DIM)
    freq2 = jnp.concatenate([inv_freq, inv_freq]).reshape(1, HEAD_DIM)
    sign = jnp.concatenate([-jnp.ones((half,), F32), jnp.ones((half,), F32)]).reshape(1, HEAD_DIM)
    bm = min(2048, n)
    return pl.pallas_call(
        _rope_tab_kernel,
        grid=(n // bm,),
        in_specs=[pl.BlockSpec((bm, 1), lambda i: (i, 0)),
                  pl.BlockSpec((1, HEAD_DIM), lambda i: (0, 0)),
                  pl.BlockSpec((1, HEAD_DIM), lambda i: (0, 0))],
        out_specs=[pl.BlockSpec((bm, HEAD_DIM), lambda i: (i, 0)),
                   pl.BlockSpec((bm, HEAD_DIM), lambda i: (i, 0))],
        out_shape=[jax.ShapeDtypeStruct((n, HEAD_DIM), F32)] * 2,
        compiler_params=_cparams(("arbitrary",), 32),
        name="rope_tables",
    )(positions.reshape(n, 1), freq2, sign)


def _rope(t, cos, sin):
    return t * cos + pltpu.roll(t, HEAD_DIM // 2, axis=1) * sin


def _prep_kernel(x_ref, xp_ref, gain_ref, mu_ref, w1_ref, w2_ref, w0_ref, a1_ref, a2_ref, a0_ref,
                 g1_ref, g2_ref, wwi_ref,
                 xn_ref, lw_ref, a_ref, g_ref, wi_ref, *, seq_len, bm):
    i = pl.program_id(0)
    gain = gain_ref[...]
    xn = _rms_rows(x_ref[...]) * gain
    prev = (_rms_rows(xp_ref[...]) * gain)[SUBLANES - 1:SUBLANES, :]
    prev = jnp.where((i * bm) % seq_len == 0, jnp.zeros_like(prev), prev)
    row = lax.broadcasted_iota(I32, xn.shape, 0)
    sh = jnp.where(row == 0, prev, pltpu.roll(xn, 1, axis=0))
    xx = sh - xn
    xn_bf = xn.astype(BF16)
    xn_ref[...] = xn_bf
    wi_ref[...] = _dot(xn_bf, wwi_ref[...]) * (N_HEADS_IDX ** -0.5)

    xw = (xn + xx * mu_ref[0:1, :]).astype(BF16)
    hw = jnp.tanh(_dot(xw, w1_ref[...])).astype(BF16)
    wl = w0_ref[...] + _dot(hw, w2_ref[...])
    lw_ref[...] = -DECAY_SCALE * _sigmoid(wl)

    xa = (xn + xx * mu_ref[1:2, :]).astype(BF16)
    ha = _dot(xa, a1_ref[...]).astype(BF16)
    a_ref[...] = _sigmoid(a0_ref[...] + _dot(ha, a2_ref[...]))

    xg = (xn + xx * mu_ref[2:3, :]).astype(BF16)
    hg = _sigmoid(_dot(xg, g1_ref[...])).astype(BF16)
    g_ref[...] = _dot(hg, g2_ref[...])


def _prep(x2, gain, mu_wag, w1, w2, w0, a1, a2, a0, g1, g2, w_in, wi_block, seq_len):
    m, d = x2.shape
    db = w2.shape[1]
    bm = min(128, m)
    full = lambda a: pl.BlockSpec(a.shape, lambda i: (0,) * a.ndim)
    nsub = bm // SUBLANES
    return pl.pallas_call(
        functools.partial(_prep_kernel, seq_len=seq_len, bm=bm),
        grid=(m // bm,),
        in_specs=[pl.BlockSpec((bm, d), lambda i: (i, 0)),
                  pl.BlockSpec((SUBLANES, d), lambda i: (jnp.maximum(i * nsub - 1, 0), 0)),
                  full(gain), full(mu_wag), full(w1), full(w2), full(w0), full(a1), full(a2), full(a0),
                  full(g1), full(g2), pl.BlockSpec((d, LANES), lambda i: (0, wi_block))],
        out_specs=[pl.BlockSpec((bm, d), lambda i: (i, 0)),
                   pl.BlockSpec((bm, db), lambda i: (i, 0)),
                   pl.BlockSpec((bm, db), lambda i: (i, 0)),
                   pl.BlockSpec((bm, db), lambda i: (i, 0)),
                   pl.BlockSpec((bm, LANES), lambda i: (i, 0))],
        out_shape=[jax.ShapeDtypeStruct((m, d), BF16),
                   jax.ShapeDtypeStruct((m, db), F32),
                   jax.ShapeDtypeStruct((m, db), F32),
                   jax.ShapeDtypeStruct((m, db), F32),
                   jax.ShapeDtypeStruct((m, LANES), F32)],
        compiler_params=_cparams(("arbitrary",), 48),
        name="prep",
    )(x2, x2, gain, mu_wag, w1, w2, w0, a1, a2, a0, g1, g2, w_in)


def _proj_rope_hm_kernel(x_ref, w_ref, cos_ref, sin_ref, o_ref):
    acc = _dot(x_ref[...], w_ref[...])
    cos = cos_ref[...]
    sin = sin_ref[...]
    bm, bn = acc.shape
    for j in range(bn // HEAD_DIM):
        t = _rope(acc[:, j * HEAD_DIM:(j + 1) * HEAD_DIM], cos, sin).astype(o_ref.dtype)
        for r in range(bm // Q_BLOCK):
            o_ref[r, j] = t[r * Q_BLOCK:(r + 1) * Q_BLOCK, :]


def _proj_rope_headmajor(xn, w, col_starts, width, cos, sin, bm=1024, bn=1024, side=None):
    m, k = xn.shape
    n = width * len(col_starts)
    bm = min(bm, m)
    per = width // bn
    first, second = (c // bn for c in col_starts)

    def w_block(i, j):
        return 0, jnp.where(j < per, first + j, second + j - per)

    return _matmul_call(
        _proj_rope_hm_kernel, (m // bm, n // bn),
        [pl.BlockSpec((bm, k), lambda i, j: (i, 0)),
         pl.BlockSpec((k, bn), w_block),
         pl.BlockSpec((bm, HEAD_DIM), lambda i, j: (i, 0)),
         pl.BlockSpec((bm, HEAD_DIM), lambda i, j: (i, 0))],
        pl.BlockSpec((bm // Q_BLOCK, bn // HEAD_DIM, Q_BLOCK, HEAD_DIM), lambda i, j: (i, j, 0, 0)),
        jax.ShapeDtypeStruct((m // Q_BLOCK, n // HEAD_DIM, Q_BLOCK, HEAD_DIM), BF16),
        (xn, w, cos, sin), "proj_q", side=side)


def _proj_kv_kernel(x_ref, wk_ref, wki_ref, wv_ref, cos_ref, sin_ref, o_ref):
    x = x_ref[...]
    cos = cos_ref[...]
    sin = sin_ref[...]
    col = 0
    for w_ref, rotary in ((wk_ref, True), (wki_ref, True), (wv_ref, False)):
        acc = _dot(x, w_ref[...])
        for j in range(acc.shape[1] // HEAD_DIM):
            t = acc[:, j * HEAD_DIM:(j + 1) * HEAD_DIM]
            if rotary:
                t = _rope(t, cos, sin)
            o_ref[:, col:col + HEAD_DIM] = t.astype(o_ref.dtype)
            col += HEAD_DIM


def _proj_kv(xn, w, col_k, col_ki, col_v, cos, sin, bm=1024):
    m, k = xn.shape
    kvd = N_KV_A * HEAD_DIM
    n = 2 * kvd + HEAD_DIM
    bm = min(bm, m)
    return pl.pallas_call(
        _proj_kv_kernel,
        grid=(m // bm,),
        in_specs=[pl.BlockSpec((bm, k), lambda i: (i, 0)),
                  pl.BlockSpec((k, kvd), lambda i: (0, col_k // kvd)),
                  pl.BlockSpec((k, HEAD_DIM), lambda i: (0, col_ki // HEAD_DIM)),
                  pl.BlockSpec((k, kvd), lambda i: (0, col_v // kvd)),
                  pl.BlockSpec((bm, HEAD_DIM), lambda i: (i, 0)),
                  pl.BlockSpec((bm, HEAD_DIM), lambda i: (i, 0))],
        out_specs=pl.BlockSpec((bm, n), lambda i: (i, 0)),
        out_shape=jax.ShapeDtypeStruct((m, n), BF16),
        compiler_params=_cparams(("arbitrary",), 52),
        name="proj_kv",
    )(xn, w, w, w, cos, sin)


def _mm_kernel(x_ref, w_ref, o_ref):
    o_ref[...] = _dot(x_ref[...], w_ref[...]).astype(o_ref.dtype)


def _matmul(x, w, out_dtype, bm=1024, bn=1024, name="matmul", side=None):
    m, k = x.shape
    n = w.shape[1]
    bm = min(bm, m)
    return _matmul_call(
        _mm_kernel, (m // bm, n // bn),
        [pl.BlockSpec((bm, k), lambda i, j: (i, 0)),
         pl.BlockSpec((k, bn), lambda i, j: (0, j))],
        pl.BlockSpec((bm, bn), lambda i, j: (i, j)),
        jax.ShapeDtypeStruct((m, n), out_dtype), (x, w), name, side=side)


WORD_BITS = 32
SUM_ROWS = 16
KEYS_PER_WORD_GROUP = WORD_BITS * SUBLANES


def _bit_transpose32(words):
    a = list(words)
    j, m = 16, 0x0000FFFF
    while j:
        mask = jnp.int32(m - (1 << 32) if m >= (1 << 31) else m)
        k = 0
        while k < WORD_BITS:
            t = (a[k] ^ lax.shift_right_logical(a[k + j], jnp.full_like(a[k], j))) & mask
            a[k] = a[k] ^ t
            a[k + j] = a[k + j] ^ (t << j)
            k = (k + j + 1) & ~j
        j >>= 1
        m = (m ^ (m << j)) & 0xFFFFFFFF
    return a


def _popcount_rows(words):
    per_sublane = jnp.sum(lax.population_count(words).reshape(-1, SUBLANES, words.shape[1]), axis=0)
    return jnp.sum(per_sublane.astype(F32), axis=0, keepdims=True)


def _attn_kernel(q_ref, qi_ref, wi_ref, k_ref, ki_ref, vt_ref, o_ref,
                 plane_ref, sel_ref, bias_ref, m_ref, acc_ref, alpha_ref, s_ref, p_ref,
                 *, topk, tk, idx_bits):
    i = pl.program_id(1)
    nk = (i + 1) * Q_BLOCK
    nch = (nk + tk - 1) // tk
    n_words = plane_ref.shape[1]
    groups_per_chunk = tk // KEYS_PER_WORD_GROUP

    @pl.when(i == 0)
    def _():
        plane_ref[...] = jnp.zeros_like(plane_ref)

    qi = qi_ref[0].reshape(N_HEADS_IDX * Q_BLOCK, HEAD_DIM)
    wi_t = jnp.transpose(wi_ref[...]) * (HEAD_DIM ** -0.5)
    qpos = i * Q_BLOCK + lax.broadcasted_iota(I32, (tk, Q_BLOCK), 1)
    krow = lax.broadcasted_iota(I32, (tk, Q_BLOCK), 0)

    def score_chunk(c):
        off = pl.multiple_of(c * tk, tk)
        lg = _dot_nt(ki_ref[0, pl.ds(off, tk), :], qi)
        yield
        sc = jnp.zeros((tk, Q_BLOCK), F32)
        for h in range(N_HEADS_IDX):
            sc = sc + wi_t[h:h + 1, :] * jnp.maximum(lg[:, h * Q_BLOCK:(h + 1) * Q_BLOCK], 0.0)
        yield
        bits = pltpu.bitcast(sc, I32)
        key = bits ^ ((bits >> 31) & 0x7FFFFFFF)
        key = jnp.where(key == -1, 0, key)
        ukey = jnp.where(off + krow <= qpos, key ^ INT_MIN, 0)
        for gi in range(groups_per_chunk):
            base = gi * KEYS_PER_WORD_GROUP
            planes = _bit_transpose32(
                [ukey[base + t * SUBLANES:base + (t + 1) * SUBLANES, :] for t in range(WORD_BITS)])
            row0 = pl.multiple_of((c * groups_per_chunk + gi) * SUBLANES, SUBLANES)
            for b in range(WORD_BITS):
                plane_ref[b, pl.ds(row0, SUBLANES), :] = planes[WORD_BITS - 1 - b]
            yield

    def score_chunks(chunks):
        pieces = [score_chunk(c) for c in chunks]
        while pieces:
            pieces = [t for t in pieces if next(t, "done") != "done"]

    def score_quad(j, carry):
        score_chunks((4 * j, 4 * j + 1, 4 * j + 2, 4 * j + 3))
        return carry

    lax.fori_loop(0, nch // 4, score_quad, 0)

    @pl.when(nch % 4 >= 2)
    def _():
        base = (nch // 4) * 4
        score_chunks((base, base + 1))

    @pl.when(nch % 2 == 1)
    def _():
        score_chunks((nch - 1,))

    cand = jnp.full((n_words, Q_BLOCK), -1, I32)
    greater = jnp.zeros((n_words, Q_BLOCK), I32)
    cnt_gt = jnp.zeros((1, Q_BLOCK), F32)
    for b in range(WORD_BITS - 1, -1, -1):
        ones = cand & plane_ref[b]
        cnt = _popcount_rows(ones)
        take = cnt_gt + cnt >= topk
        greater = jnp.where(take, greater, greater | ones)
        cnt_gt = jnp.where(take, cnt_gt, cnt_gt + cnt)
        cand = jnp.where(take, ones, cand ^ ones)

    word_row = lax.broadcasted_iota(I32, (n_words, Q_BLOCK), 0)
    word_pos = (word_row >> 3) * KEYS_PER_WORD_GROUP + (word_row & (SUBLANES - 1))
    qcol = i * Q_BLOCK + lax.broadcasted_iota(I32, (n_words, Q_BLOCK), 1)

    def prefix(limit):
        nt = jnp.clip((limit - word_pos + (SUBLANES - 1)) >> 3, 0, WORD_BITS)
        top = lax.shift_right_arithmetic(jnp.full_like(nt, INT_MIN), jnp.maximum(nt, 1) - 1)
        return jnp.where(nt <= 0, 0, top)

    cand = cand & prefix(qcol + 1)
    need = topk - cnt_gt
    sel_ref[...] = greater | cand

    @pl.when(jnp.max(_popcount_rows(cand) - need) > 0)
    def _():
        def jbody(bi, jt):
            cj = jt | jnp.left_shift(jnp.int32(1), idx_bits - 1 - bi)
            return jnp.where(_popcount_rows(cand & prefix(cj)) <= need, cj, jt)
        jt = lax.fori_loop(0, idx_bits, jbody, jnp.zeros((1, Q_BLOCK), I32))
        sel_ref[...] = greater | (cand & prefix(jt))

    def bias_body(c, carry):
        off = pl.multiple_of(c * tk, tk)
        for gi in range(groups_per_chunk):
            row0 = pl.multiple_of((c * groups_per_chunk + gi) * SUBLANES, SUBLANES)
            w = sel_ref[pl.ds(row0, SUBLANES), :]
            for t in range(WORD_BITS):
                dst = pl.multiple_of(off + gi * KEYS_PER_WORD_GROUP + t * SUBLANES, SUBLANES)
                bias_ref[pl.ds(dst, SUBLANES), :] = jnp.where((w << t) < 0, 0.0, -jnp.inf)
        return carry

    lax.fori_loop(0, nch, bias_body, 0)

    n_rep = N_HEADS_A // N_KV_A
    cols = n_rep * Q_BLOCK
    q_all = q_ref[0].reshape(N_HEADS_A * Q_BLOCK, HEAD_DIM)
    scale2 = (HEAD_DIM ** -0.5) * LOG2_E
    m_ref[...] = jnp.full(m_ref.shape, -1e30, F32)
    acc_ref[...] = jnp.zeros(acc_ref.shape, F32)
    ones_rows = jnp.ones((SUM_ROWS, tk), BF16)

    def stage_qk(g, off):
        qg = q_all[g * cols:(g + 1) * cols, :]
        s_ref[g] = _dot_nt(k_ref[0, pl.ds(off, tk), g * HEAD_DIM:(g + 1) * HEAD_DIM], qg)

    def stage_max(g, off):
        for n in range(n_rep):
            csl = slice(n * Q_BLOCK, (n + 1) * Q_BLOCK)
            t = s_ref[g, :, csl] + bias_ref[pl.ds(off, tk), :]
            s_ref[g, :, csl] = t
            m_old = m_ref[g, :, csl]
            m_new = jnp.maximum(m_old, jnp.max(t, axis=0, keepdims=True))
            alpha_ref[g, :, csl] = jnp.exp2((m_old - m_new) * scale2)
            m_ref[g, :, csl] = m_new

    def stage_exp(g, off):
        for n in range(n_rep):
            csl = slice(n * Q_BLOCK, (n + 1) * Q_BLOCK)
            p_ref[g, :, csl] = jnp.exp2((s_ref[g, :, csl] - m_ref[g, 0:1, csl]) * scale2).astype(BF16)

    def stage_pv(g, off):
        v_ext = jnp.concatenate(
            [vt_ref[0, g * HEAD_DIM:(g + 1) * HEAD_DIM, pl.ds(off, tk)], ones_rows], axis=0)
        acc_ref[g] = alpha_ref[g, 0:1, :] * acc_ref[g] + _dot(v_ext, p_ref[g])

    stages = (stage_qk, stage_max, stage_exp, stage_pv)

    def attend_chunks(chunks):
        pieces = [(g, pl.multiple_of(c * tk, tk)) for c in chunks for g in range(N_KV_A)]
        for step in range(len(pieces) + len(stages) - 1):
            for k, (g, off) in enumerate(pieces):
                if 0 <= step - k < len(stages):
                    stages[step - k](g, off)

    def att_quad(j, carry):
        attend_chunks((4 * j, 4 * j + 1, 4 * j + 2, 4 * j + 3))
        return carry

    lax.fori_loop(0, nch // 4, att_quad, 0)

    @pl.when(nch % 4 >= 2)
    def _():
        base = (nch // 4) * 4
        attend_chunks((base, base + 1))

    @pl.when(nch % 2 == 1)
    def _():
        attend_chunks((nch - 1,))
    for g in range(N_KV_A):
        acc = acc_ref[g]
        o_t = acc[0:HEAD_DIM, :] / acc[HEAD_DIM:HEAD_DIM + 1, :]
        for n in range(n_rep):
            h = g * n_rep + n
            o_ref[:, h * HEAD_DIM:(h + 1) * HEAD_DIM] = jnp.transpose(
                o_t[:, n * Q_BLOCK:(n + 1) * Q_BLOCK]).astype(o_ref.dtype)


def _attention(qh, wi, kkiv, vt, batch, seq_len):
    nb = seq_len // Q_BLOCK
    topk = min(TOPK_MAX, seq_len // 4)
    tk = min(512, seq_len)
    kvd = N_KV_A * HEAD_DIM
    cols = N_HEADS_A // N_KV_A * Q_BLOCK
    single = pl.Buffered(1)
    return pl.pallas_call(
        functools.partial(_attn_kernel, topk=topk, tk=tk, idx_bits=seq_len.bit_length()),
        grid=(batch, nb),
        in_specs=[pl.BlockSpec((1, N_HEADS_A, Q_BLOCK, HEAD_DIM), lambda b, i: (b * nb + i, 0, 0, 0)),
                  pl.BlockSpec((1, N_HEADS_IDX, Q_BLOCK, HEAD_DIM), lambda b, i: (b * nb + i, 1, 0, 0)),
                  pl.BlockSpec((Q_BLOCK, LANES), lambda b, i: (b * nb + i, 0)),
                  pl.BlockSpec((1, seq_len, kvd), lambda b, i: (b, 0, 0), pipeline_mode=single),
                  pl.BlockSpec((1, seq_len, HEAD_DIM), lambda b, i: (b, 0, kvd // HEAD_DIM),
                               pipeline_mode=single),
                  pl.BlockSpec((1, kvd, seq_len), lambda b, i: (b, 0, 0), pipeline_mode=single)],
        out_specs=pl.BlockSpec((Q_BLOCK, N_HEADS_A * HEAD_DIM), lambda b, i: (b * nb + i, 0)),
        out_shape=jax.ShapeDtypeStruct((batch * seq_len, N_HEADS_A * HEAD_DIM), BF16),
        scratch_shapes=[pltpu.VMEM((WORD_BITS, seq_len // WORD_BITS, Q_BLOCK), I32),
                        pltpu.VMEM((seq_len // WORD_BITS, Q_BLOCK), I32),
                        pltpu.VMEM((seq_len, Q_BLOCK), F32),
                        pltpu.VMEM((N_KV_A, SUBLANES, cols), F32),
                        pltpu.VMEM((N_KV_A, HEAD_DIM + SUM_ROWS, cols), F32),
                        pltpu.VMEM((N_KV_A, SUBLANES, cols), F32),
                        pltpu.VMEM((N_KV_A, tk, cols), F32),
                        pltpu.VMEM((N_KV_A, tk, cols), BF16)],
        compiler_params=_cparams(("arbitrary", "arbitrary"), 52),
        name="dsa_attention",
    )(qh, qh, wi, kkiv, kkiv, vt)


def _dot_hp(x, w_bf):
    hi = x.astype(BF16)
    r1 = x - hi.astype(F32)
    mid = r1.astype(BF16)
    lo = (r1 - mid.astype(F32)).astype(BF16)
    n = x.shape[0]
    parts = _dot(jnp.concatenate([hi, mid, lo], axis=0), w_bf)
    return parts[0:n] + parts[n:2 * n] + parts[2 * n:3 * n]


def _rwkv_kernel(r_ref, k_ref, v_ref, lw_ref, a_ref, g_ref, mu_ref, kk_ref, ka_ref, rk_ref,
                 lnw_ref, lnb_ref, bd_ref, ones_ref, ltri_ref, ts_ref, ti_ref, ic_ref, eye_ref,
                 o_ref, st_ref, prev_ref):
    @pl.when(pl.program_id(2) == 0)
    def _():
        st_ref[...] = jnp.zeros_like(st_ref)
        prev_ref[...] = jnp.zeros_like(prev_ref)

    pieces = [_rwkv_tile(tile, row0, r_ref, k_ref, v_ref, lw_ref, a_ref, g_ref, mu_ref, kk_ref, ka_ref,
                         rk_ref, lnw_ref, lnb_ref, bd_ref, ones_ref, ltri_ref, ts_ref, ti_ref, ic_ref,
                         eye_ref, o_ref, st_ref, prev_ref)
              for row0 in range(0, r_ref.shape[0], CHUNK)
              for tile in range(r_ref.shape[1] // TILE_B)]
    while pieces:
        pieces = [t for t in pieces if next(t, "done") != "done"]


def _rwkv_tile(tile, row0, r_ref, k_ref, v_ref, lw_ref, a_ref, g_ref, mu_ref, kk_ref, ka_ref, rk_ref,
               lnw_ref, lnb_ref, bd_ref, ones_ref, ltri_ref, ts_ref, ti_ref, ic_ref, eye_ref,
               o_ref, st_ref, prev_ref):
    cols = slice(tile * TILE_B, (tile + 1) * TILE_B)
    rows = slice(row0, row0 + CHUNK)
    bdm = bd_ref[...]
    row = lax.broadcasted_iota(I32, (CHUNK, TILE_B), 0)

    def shifted(x, slot):
        prev = prev_ref[slot:slot + 1, cols]
        prev_ref[slot:slot + 1, cols] = x[CHUNK - 1:CHUNK, :]
        return jnp.where(row == 0, prev, pltpu.roll(x, 1, axis=0))

    r0 = r_ref[rows, cols]
    k0 = k_ref[rows, cols]
    v0 = v_ref[rows, cols]
    r = r0 + (shifted(r0, 0) - r0) * mu_ref[0:1, cols]
    k = k0 + (shifted(k0, 1) - k0) * mu_ref[1:2, cols]
    v = v0 + (shifted(v0, 2) - v0) * mu_ref[2:3, cols]
    a = a_ref[rows, cols]
    ones_bd = ones_ref[...]

    kk = k * kk_ref[:, cols]
    k2 = k * (1.0 + (a - 1.0) * ka_ref[:, cols])
    head_sums = _dot_hp(jnp.concatenate([kk * kk, r * k2 * rk_ref[:, cols]], axis=0), ones_bd)
    ss = head_sums[0:CHUNK]
    bonus = head_sums[CHUNK:2 * CHUNK] * v
    yield
    kk = kk / jnp.maximum(jnp.sqrt(ss), 1e-12)
    aa = -kk
    bb = kk * a

    lw = lw_ref[rows, cols]
    cs = _cumsum_rows(lw, ltri_ref[...])
    yield
    tot = cs[CHUNK - 1:CHUNK, :]
    e_in = jnp.exp(cs)
    e_out = jnp.exp(-cs)
    e_tail = jnp.exp(tot - cs)
    at = aa * jnp.exp(cs - lw)
    rt = r * e_in
    bt = (bb * e_out)
    kt = (k2 * e_out)
    bh = (bb * e_tail).astype(BF16)
    kh = (k2 * e_tail).astype(BF16)
    wc = jnp.exp(tot)

    def bd(x):
        return jnp.concatenate([x.astype(BF16)] * HEADS_PER_TILE, axis=0) * ones_bd

    lhs = jnp.concatenate([at, rt], axis=0).astype(BF16)
    mb = _dot_nt(lhs, bd(bt))
    mk = _dot_nt(lhs, bd(kt))
    yield
    ts = ts_ref[...]
    ti = ti_ref[...]
    m_ab = mb[0:CHUNK] * ts
    m_rb = (mb[CHUNK:2 * CHUNK] * ti).astype(BF16)
    m_ak = (mk[0:CHUNK] * ts).astype(BF16)
    m_rk = (mk[CHUNK:2 * CHUNK] * ti).astype(BF16)

    n_round = CHUNK.bit_length() - 1
    t_inv = ic_ref[...] + m_ab
    m_pow = _dot(m_ab.astype(BF16), bd(m_ab))
    yield
    for rnd in range(1, n_round):
        last = rnd == n_round - 1
        lhs_rows = [t_inv] if last else [t_inv, m_pow]
        prod = _dot(jnp.concatenate(lhs_rows, axis=0).astype(BF16), bd(m_pow))
        yield
        t_inv = t_inv + prod[0:CHUNK]
        if not last:
            m_pow = prod[CHUNK:2 * CHUNK]
    t_bf = t_inv.astype(BF16)

    p = _dot(t_bf, bd(at))
    uv = _dot(jnp.concatenate([m_ak, m_rk], axis=0), bd(v))
    u = uv[0:CHUNK]
    yield
    q = _dot(t_bf, bd(u))
    rp = rt + _dot(m_rb, bd(p))
    yield
    y0 = _dot(m_rb, bd(q)) + uv[CHUNK:2 * CHUNK]

    yield
    a_t = _dot_tn(bh, p.astype(BF16)) * bdm + eye_ref[...] * wc
    d_t = _dot_tn(jnp.concatenate([bh, kh], axis=0),
                  jnp.concatenate([q.astype(BF16), v.astype(BF16)], axis=0)) * bdm
    st = st_ref[tile]
    from_state = _dot(jnp.concatenate([rp.astype(BF16), a_t.astype(BF16)], axis=0), st.astype(BF16))
    y = from_state[0:CHUNK] + y0
    st_ref[tile] = from_state[CHUNK:CHUNK + TILE_B] + d_t

    yield
    inv_n = 1.0 / HEAD_DIM_B
    mu = _dot_hp(y, ones_bd) * inv_n
    yield
    yc = y - mu
    var = _dot_hp(yc * yc, ones_bd) * inv_n
    yield
    yn =(yc * lax.rsqrt(var + GN_EPS)) * lnw_ref[:, cols] + lnb_ref[:, cols]
    o_ref[rows, cols] = ((yn + bonus) * g_ref[rows, cols]).astype(o_ref.dtype)


def _cumsum_rows(x, ltri_bf):
    hi = x.astype(BF16)
    r1 = x - hi.astype(F32)
    mid = r1.astype(BF16)
    lo = (r1 - mid.astype(F32)).astype(BF16)
    return _dot(ltri_bf, hi) + _dot(ltri_bf, mid) + _dot(ltri_bf, lo)


def _rwkv_consts():
    idx = jnp.arange(TILE_B)
    head = idx // HEAD_DIM_B
    bdm = (head[:, None] == head[None, :])
    t = jnp.arange(CHUNK)[:, None]
    s = (idx % CHUNK)[None, :]
    return dict(
        bd=bdm.astype(F32),
        ones=bdm.astype(BF16),
        ltri=(jnp.arange(CHUNK)[None, :] <= t).astype(BF16),
        ts=(s < t).astype(F32),
        ti=(s <= t).astype(F32),
        ic=(s == t).astype(F32),
        eye=jnp.eye(TILE_B, dtype=F32),
    )


def _rwkv(rkv, lw, a, g, mu_rkv, k_k, k_a, r_k, ln_w, ln_b, batch, seq_len):
    m, db = lw.shape
    block_rows = RWKV_CHUNKS_PER_STEP * CHUNK
    nc = seq_len // block_rows
    width = RWKV_TILES_PER_STEP * TILE_B
    ng = db // width
    cst = _rwkv_consts()
    blk = lambda off: pl.BlockSpec((block_rows, width), lambda b, h, c, off=off: (b * nc + c, off + h))
    par = lambda rows: pl.BlockSpec((rows, width), lambda b, h, c: (0, h))
    full = lambda arr: pl.BlockSpec(arr.shape, lambda b, h, c: (0, 0))
    row = lambda p: p.reshape(1, db)
    return pl.pallas_call(
        _rwkv_kernel,
        grid=(batch, ng, nc),
        in_specs=[blk(0), blk(ng), blk(2 * ng), blk(0), blk(0), blk(0),
                  par(3), par(1), par(1), par(1), par(1), par(1),
                  full(cst["bd"]), full(cst["ones"]), full(cst["ltri"]), full(cst["ts"]),
                  full(cst["ti"]), full(cst["ic"]), full(cst["eye"])],
        out_specs=blk(0),
        out_shape=jax.ShapeDtypeStruct((m, db), BF16),
        scratch_shapes=[pltpu.VMEM((RWKV_TILES_PER_STEP, TILE_B, TILE_B), F32),
                        pltpu.VMEM((SUBLANES, width), F32)],
        compiler_params=_cparams(("arbitrary", "arbitrary", "arbitrary"), 32),
        name="rwkv7",
    )(rkv, rkv, rkv, lw, a, g, mu_rkv, row(k_k), row(k_a), row(r_k), row(ln_w), row(ln_b),
      cst["bd"], cst["ones"], cst["ltri"], cst["ts"], cst["ti"], cst["ic"], cst["eye"])


def _merge_kernel(xn_ref, oa_ref, ob_ref, wga_ref, wgb_ref, bga_ref, bgb_ref, wpa_ref, wpb_ref, o_ref):
    xn = xn_ref[...]
    ga = _sigmoid(_dot(xn, wga_ref[...]) + bga_ref[...])
    gb = _sigmoid(_dot(xn, wgb_ref[...]) + bgb_ref[...])
    ya = _dot(oa_ref[...], wpa_ref[...])
    yb = _dot(ob_ref[...], wpb_ref[...])
    o_ref[...] = (ga * ya + gb * yb).astype(o_ref.dtype)


def _merge(xn, oa, ob, w_gate, b_gate, w_pa, w_pb, bm=1024, bn=256, side=None):
    m, d = xn.shape
    bm = min(bm, m)
    nj = d // bn
    return _matmul_call(
        _merge_kernel, (m // bm, nj),
        [pl.BlockSpec((bm, d), lambda i, j: (i, 0)),
         pl.BlockSpec((bm, oa.shape[1]), lambda i, j: (i, 0)),
         pl.BlockSpec((bm, ob.shape[1]), lambda i, j: (i, 0)),
         pl.BlockSpec((d, bn), lambda i, j: (0, j)),
         pl.BlockSpec((d, bn), lambda i, j: (0, j + nj)),
         pl.BlockSpec((1, bn), lambda i, j: (0, j)),
         pl.BlockSpec((1, bn), lambda i, j: (0, j + nj)),
         pl.BlockSpec((w_pa.shape[0], bn), lambda i, j: (0, j)),
         pl.BlockSpec((w_pb.shape[0], bn), lambda i, j: (0, j))],
        pl.BlockSpec((bm, bn), lambda i, j: (i, j)),
        jax.ShapeDtypeStruct((m, d), BF16),
        (xn, oa, ob, w_gate, w_gate, b_gate, b_gate, w_pa, w_pb), "gated_merge", vmem_mib=56, side=side)


def _mm_res_kernel(x_ref, w_ref, res_ref, o_ref):
    o_ref[...] = res_ref[...] + _dot(x_ref[...], w_ref[...])


def _matmul_residual(x, w, res, bm=1024, bn=1024, name="matmul_res", side=None):
    m, k = x.shape
    n = w.shape[1]
    bm = min(bm, m)
    return _matmul_call(
        _mm_res_kernel, (m // bm, n // bn),
        [pl.BlockSpec((bm, k), lambda i, j: (i, 0)),
         pl.BlockSpec((k, bn), lambda i, j: (0, j)),
         pl.BlockSpec((bm, bn), lambda i, j: (i, j))],
        pl.BlockSpec((bm, bn), lambda i, j: (i, j)),
        jax.ShapeDtypeStruct((m, n), F32), (x, w, res), name, side=side)


def _ffn_up_kernel(x_ref, w1_ref, w3_ref, o_ref):
    x = x_ref[...]
    h1 = _dot(x, w1_ref[...])
    h3 = _dot(x, w3_ref[...])
    o_ref[...] = (h1 * _sigmoid(h1) * h3).astype(o_ref.dtype)


def _ffn_up(x, w1, w3, bm=1024, bn=512, side=None):
    m, k = x.shape
    n = w1.shape[1]
    bm = min(bm, m)
    return _matmul_call(
        _ffn_up_kernel, (m // bm, pl.cdiv(n, bn)),
        [pl.BlockSpec((bm, k), lambda i, j: (i, 0)),
         pl.BlockSpec((k, bn), lambda i, j: (0, j)),
         pl.BlockSpec((k, bn), lambda i, j: (0, j))],
        pl.BlockSpec((bm, bn), lambda i, j: (i, j)),
        jax.ShapeDtypeStruct((m, n), BF16), (x, w1, w3), "ffn_up", side=side)


def _ple_kernel(h_ref, wg_ref, p_ref, wp_ref, gain_ref, o_ref, hn_ref, *, bn, final_norm):
    j = pl.program_id(1)
    n_row_chunks = h_ref.shape[0] // NORM_ROW_CHUNK

    def row_chunk(r):
        return pl.ds(pl.multiple_of(r * NORM_ROW_CHUNK, NORM_ROW_CHUNK), NORM_ROW_CHUNK)

    @pl.when(j == 0)
    def _():
        def body(r, carry):
            hn_ref[row_chunk(r), :] = _rms_rows(h_ref[row_chunk(r), :]).astype(BF16)
            return carry
        lax.fori_loop(0, n_row_chunks, body, 0)

    col = pl.multiple_of(j * bn, bn)
    gate = _sigmoid(_dot(hn_ref[...], wg_ref[...]))
    o_ref[:, pl.ds(col, bn)] = h_ref[:, pl.ds(col, bn)] + gate * _dot(p_ref[...], wp_ref[...])

    if final_norm:
        @pl.when(j == pl.num_programs(1) - 1)
        def _():
            def body(r, carry):
                o_ref[row_chunk(r), :] = _rms_rows(o_ref[row_chunk(r), :]) * gain_ref[...]
                return carry
            lax.fori_loop(0, n_row_chunks, body, 0)


def _ple(h, wg, p, wp, final_gain, bm=512, bn=512):
    m, d = h.shape
    bm = min(bm, m)
    final_norm = final_gain is not None
    gain = (final_gain if final_norm else jnp.ones((d,), F32)).reshape(1, d)
    return pl.pallas_call(
        functools.partial(_ple_kernel, bn=bn, final_norm=final_norm),
        grid=(m // bm, d // bn),
        in_specs=[pl.BlockSpec((bm, d), lambda i, j: (i, 0)),
                  pl.BlockSpec((d, bn), lambda i, j: (0, j)),
                  pl.BlockSpec((bm, p.shape[1]), lambda i, j: (i, 0)),
                  pl.BlockSpec((p.shape[1], bn), lambda i, j: (0, j)),
                  pl.BlockSpec((1, d), lambda i, j: (0, 0))],
        out_specs=pl.BlockSpec((bm, d), lambda i, j: (i, 0)),
        out_shape=jax.ShapeDtypeStruct((m, d), F32),
        scratch_shapes=[pltpu.VMEM((bm, d), BF16)],
        compiler_params=_cparams(("arbitrary", "arbitrary"), 52),
        name="ple",
    )(h, wg, p, wp, gain)


def _rmsnorm_kernel(x_ref, g_ref, o_ref):
    o_ref[...] = (_rms_rows(x_ref[...]) * g_ref[...]).astype(o_ref.dtype)


def _rmsnorm(x, gain, out_dtype, bm=512):
    m, d = x.shape
    bm = min(bm, m)
    row_spec = pl.BlockSpec((bm, d), lambda i: (i, 0))
    return pl.pallas_call(
        _rmsnorm_kernel,
        grid=(m // bm,),
        in_specs=[row_spec, pl.BlockSpec((1, d), lambda i: (0, 0))],
        out_specs=row_spec,
        out_shape=jax.ShapeDtypeStruct((m, d), out_dtype),
        compiler_params=_cparams(("arbitrary",), 48),
        name="rmsnorm",
    )(x, gain.reshape(1, d))


def _pad_to(a, axis, size):
    pad = [(0, 0)] * a.ndim
    pad[axis] = (0, size - a.shape[axis])
    return jnp.pad(a, pad)


def _layer(h2, p2, cos, sin, batch, seq_len, norm_mix, w_in, mu_rkv, mu_wag, w0, w1, w2, a0, a1, a2,
           g1, g2, k_k, k_a, r_k, ln_w, ln_b, w_pa, w_pb, w_gate, b_gate, w_o, norm_ffn,
           w_ffn1, w_ffn3, w_ffn2, w_ple_gate, w_ple, final_gain):
    d = h2.shape[1]
    d_a = N_HEADS_A * HEAD_DIM
    kvd = N_KV_A * HEAD_DIM
    d_iq = N_HEADS_IDX * HEAD_DIM
    d_b = w_pb.shape[0]
    o_q, o_k, o_v = 0, d_a, d_a + kvd
    o_qi = o_v + kvd
    o_ki = o_qi + d_iq
    o_wi = o_ki + HEAD_DIM
    o_r = o_wi + N_HEADS_IDX

    bf = lambda a: a.astype(BF16)
    w_in_bf = bf(w_in)
    assert d_a == d_iq and o_wi % LANES == 0
    w_rkv = w_in_bf[:, o_r:o_r + 3 * d_b]
    lora = LANES
    w1p, a1p = bf(_pad_to(w1, 1, lora)), bf(_pad_to(a1, 1, lora))
    w2p, a2p = bf(_pad_to(w2, 0, lora)), bf(_pad_to(a2, 0, lora))

    xn, lw, a, g, wi = _prep(h2, norm_mix.reshape(1, d), mu_wag, w1p, w2p, w0.reshape(1, d_b),
                             a1p, a2p, a0.reshape(1, d_b), bf(g1), bf(g2), w_in_bf, o_wi // LANES,
                             seq_len)

    qh, w_o_bf = _proj_rope_headmajor(xn, w_in_bf, (o_q, o_qi), d_a, cos, sin, side=w_o)
    kkiv = _proj_kv(xn, w_in_bf, o_k, o_ki, o_v, cos, sin)
    rkv, w_gate_bf = _matmul(xn, w_rkv, F32, bn=512, name="proj_rkv", side=w_gate)

    kkiv = kkiv.reshape(batch, seq_len, -1)
    vt = jnp.swapaxes(kkiv[:, :, kvd + HEAD_DIM:], 1, 2)
    o_att = _attention(qh, wi, kkiv, vt, batch, seq_len)

    o_rwkv = _rwkv(rkv, lw, a, g, mu_rkv, k_k, k_a, r_k.reshape(-1), ln_w, ln_b, batch, seq_len)

    mixed, w_ffn1_bf = _merge(xn, o_att, o_rwkv, w_gate_bf, b_gate.reshape(1, -1), bf(w_pa), bf(w_pb),
                              side=w_ffn1)
    h2, w_ffn3_bf = _matmul_residual(mixed, w_o_bf, h2, bn=512, name="out_proj", side=w_ffn3)

    xf = _rmsnorm(h2, norm_ffn, BF16)
    u, w_ffn2_bf = _ffn_up(xf, w_ffn1_bf, w_ffn3_bf, side=w_ffn2)
    h2, w_ple_gate_bf = _matmul_residual(u, w_ffn2_bf, h2, bm=512, bn=512, name="ffn_down",
                                         side=w_ple_gate)

    return _ple(h2, w_ple_gate_bf, bf(p2), bf(w_ple), final_gain)


def kernel(x, p, positions, norm_mix, w_in, mu_rkv, mu_wag, w0, w1, w2, a0, a1, a2, g1, g2, k_k, k_a,
           r_k, ln_w, ln_b, w_pa, w_pb, w_gate, b_gate, w_o, norm_ffn, w_ffn1, w_ffn3, w_ffn2,
           w_ple_gate, w_ple, norm_final):
    batch, seq_len, d = x.shape
    depth = p.shape[0]
    h2 = x.reshape(batch * seq_len, d)
    cos, sin = _rope_tables(positions)
    for i in range(depth):
        h2 = _layer(h2, p[i].reshape(batch * seq_len, -1), cos, sin, batch, seq_len,
                    norm_mix[i], w_in[i], mu_rkv[i], mu_wag[i], w0[i], w1[i], w2[i], a0[i], a1[i], a2[i],
                    g1[i], g2[i], k_k[i], k_a[i], r_k[i], ln_w[i], ln_b[i], w_pa[i], w_pb[i], w_gate[i],
                    b_gate[i], w_o[i], norm_ffn[i], w_ffn1[i], w_ffn3[i], w_ffn2[i], w_ple_gate[i],
                    w_ple[i], norm_final if i == depth - 1 else None)
    return h2.reshape(batch, seq_len, d)
```

```python
import functools

import jax
import jax.numpy as jnp
from jax import lax
from jax.experimental import pallas as pl
from jax.experimental.pallas import tpu as pltpu

F32 = jnp.float32
BF16 = jnp.bfloat16
I32 = jnp.int32

N_HEADS_A = 16
HEAD_DIM = 128
N_KV_A = 4
N_HEADS_IDX = 16
TOPK_MAX = 256
Q_BLOCK = 128
ROPE_THETA = 10000.0
HEAD_DIM_B = 64
GN_EPS = 64e-5
RMS_EPS = 1e-6

LANES = 128
SUBLANES = 8
BF16_SUBLANES = 16
MXU_DIM = 256

INT_MIN = -2 ** 31
LOG2_E = 1.4426950408889634
NORM_ROW_CHUNK = 64
DECAY_SCALE = 0.6065306597126334

CHUNK = 64
HEADS_PER_TILE = MXU_DIM // HEAD_DIM_B
TILE_B = HEADS_PER_TILE * HEAD_DIM_B
RWKV_CHUNKS_PER_STEP = 4
RWKV_TILES_PER_STEP = 4


def _cparams(sem, vmem_mib):
    return pltpu.CompilerParams(dimension_semantics=sem, vmem_limit_bytes=vmem_mib << 20)


def _sidecar_cast(body, n_in, every):
    def kern(*refs):
        step = pl.program_id(0) * pl.num_programs(1) + pl.program_id(1)

        @pl.when(step % every == 0)
        def _():
            refs[n_in + 2][...] = refs[n_in][...].astype(BF16)

        body(*refs[:n_in], refs[n_in + 1])
    return kern


def _matmul_call(body, grid, in_specs, out_spec, out_shape, args, name, vmem_mib=52, side=None):
    params = _cparams(("arbitrary", "arbitrary"), vmem_mib)
    if side is None:
        out = pl.pallas_call(body, grid=grid, in_specs=in_specs, out_specs=out_spec,
                             out_shape=out_shape, compiler_params=params, name=name)(*args)
        return out, None
    side, every = side if isinstance(side, tuple) else (side, 1)
    n_inner = grid[1]
    n_blocks = -(-(grid[0] * grid[1]) // every)
    rows = side.shape[0]
    rb = -(-(-(-rows // n_blocks)) // BF16_SUBLANES) * BF16_SUBLANES
    last = -(-rows // rb) - 1
    side_spec = pl.BlockSpec(
        (rb, side.shape[1]), lambda i, j: (jnp.minimum((i * n_inner + j) // every, last), 0))
    return pl.pallas_call(
        _sidecar_cast(body, len(in_specs), every), grid=grid, in_specs=[*in_specs, side_spec],
        out_specs=[out_spec, side_spec],
        out_shape=[out_shape, jax.ShapeDtypeStruct(side.shape, BF16)],
        compiler_params=params, name=name)(*args, side)


def _dot(a, b):
    return jnp.dot(a, b, preferred_element_type=F32)


def _dot_nt(a, b):
    return lax.dot_general(a, b, (((1,), (1,)), ((), ())), preferred_element_type=F32)


def _dot_tn(a, b):
    return lax.dot_general(a, b, (((0,), (0,)), ((), ())), preferred_element_type=F32)


def _sigmoid(x):
    return 1.0 / (1.0 + jnp.exp(-x))


def _rms_rows(x):
    return x * lax.rsqrt(jnp.mean(x * x, axis=-1, keepdims=True) + RMS_EPS)


def _rope_tab_kernel(pos_ref, freq_ref, sign_ref, cos_ref, sin_ref):
    ang = pos_ref[...].astype(F32) * freq_ref[...]
    cos_ref[...] = jnp.cos(ang)
    sin_ref[...] = jnp.sin(ang) * sign_ref[...]


def _rope_tables(positions):
    n = positions.size
    half = HEAD_DIM // 2
    inv_freq = ROPE_THETA ** (-jnp.arange(0, HEAD_DIM, 2, dtype=F32) / HEAD_DIM)
    freq2 = jnp.concatenate([inv_freq, inv_freq]).reshape(1, HEAD_DIM)
    sign = jnp.concatenate([-jnp.ones((half,), F32), jnp.ones((half,), F32)]).reshape(1, HEAD_DIM)
    bm = min(2048, n)
    return pl.pallas_call(
        _rope_tab_kernel,
        grid=(n // bm,),
        in_specs=[pl.BlockSpec((bm, 1), lambda i: (i, 0)),
                  pl.BlockSpec((1, HEAD_DIM), lambda i: (0, 0)),
                  pl.BlockSpec((1, HEAD_DIM), lambda i: (0, 0))],
        out_specs=[pl.BlockSpec((bm, HEAD_DIM), lambda i: (i, 0)),
                   pl.BlockSpec((bm, HEAD_DIM), lambda i: (i, 0))],
        out_shape=[jax.ShapeDtypeStruct((n, HEAD_DIM), F32)] * 2,
        compiler_params=_cparams(("arbitrary",), 32),
        name="rope_tables",
    )(positions.reshape(n, 1), freq2, sign)


def _rope(t, cos, sin):
    return t * cos + pltpu.roll(t, HEAD_DIM // 2, axis=1) * sin


def _prep_kernel(x_ref, xp_ref, gain_ref, mu_ref, w1_ref, w2_ref, w0_ref, a1_ref, a2_ref, a0_ref,
                 g1_ref, g2_ref, wwi_ref,
                 xn_ref, lw_ref, a_ref, g_ref, wi_ref, *, seq_len, bm):
    i = pl.program_id(0)
    gain = gain_ref[...]
    xn = _rms_rows(x_ref[...]) * gain
    prev = (_rms_rows(xp_ref[...]) * gain)[SUBLANES - 1:SUBLANES, :]
    prev = jnp.where((i * bm) % seq_len == 0, jnp.zeros_like(prev), prev)
    row = lax.broadcasted_iota(I32, xn.shape, 0)
    sh = jnp.where(row == 0, prev, pltpu.roll(xn, 1, axis=0))
    xx = sh - xn
    xn_bf = xn.astype(BF16)
    xn_ref[...] = xn_bf
    wi_ref[...] = _dot(xn_bf, wwi_ref[...]) * (N_HEADS_IDX ** -0.5)

    xw = (xn + xx * mu_ref[0:1, :]).astype(BF16)
    hw = jnp.tanh(_dot(xw, w1_ref[...])).astype(BF16)
    wl = w0_ref[...] + _dot(hw, w2_ref[...])
    lw_ref[...] = -DECAY_SCALE * _sigmoid(wl)

    xa = (xn + xx * mu_ref[1:2, :]).astype(BF16)
    ha = _dot(xa, a1_ref[...]).astype(BF16)
    a_ref[...] = _sigmoid(a0_ref[...] + _dot(ha, a2_ref[...]))

    xg = (xn + xx * mu_ref[2:3, :]).astype(BF16)
    hg = _sigmoid(_dot(xg, g1_ref[...])).astype(BF16)
    g_ref[...] = _dot(hg, g2_ref[...])


def _prep(x2, gain, mu_wag, w1, w2, w0, a1, a2, a0, g1, g2, w_in, wi_block, seq_len):
    m, d = x2.shape
    db = w2.shape[1]
    bm = min(128, m)
    full = lambda a: pl.BlockSpec(a.shape, lambda i: (0,) * a.ndim)
    nsub = bm // SUBLANES
    return pl.pallas_call(
        functools.partial(_prep_kernel, seq_len=seq_len, bm=bm),
        grid=(m // bm,),
        in_specs=[pl.BlockSpec((bm, d), lambda i: (i, 0)),
                  pl.BlockSpec((SUBLANES, d), lambda i: (jnp.maximum(i * nsub - 1, 0), 0)),
                  full(gain), full(mu_wag), full(w1), full(w2), full(w0), full(a1), full(a2), full(a0),
                  full(g1), full(g2), pl.BlockSpec((d, LANES), lambda i: (0, wi_block))],
        out_specs=[pl.BlockSpec((bm, d), lambda i: (i, 0)),
                   pl.BlockSpec((bm, db), lambda i: (i, 0)),
                   pl.BlockSpec((bm, db), lambda i: (i, 0)),
                   pl.BlockSpec((bm, db), lambda i: (i, 0)),
                   pl.BlockSpec((bm, LANES), lambda i: (i, 0))],
        out_shape=[jax.ShapeDtypeStruct((m, d), BF16),
                   jax.ShapeDtypeStruct((m, db), F32),
                   jax.ShapeDtypeStruct((m, db), F32),
                   jax.ShapeDtypeStruct((m, db), F32),
                   jax.ShapeDtypeStruct((m, LANES), F32)],
        compiler_params=_cparams(("arbitrary",), 48),
        name="prep",
    )(x2, x2, gain, mu_wag, w1, w2, w0, a1, a2, a0, g1, g2, w_in)


def _proj_rope_hm_kernel(x_ref, w_ref, cos_ref, sin_ref, o_ref):
    acc = _dot(x_ref[...], w_ref[...])
    cos = cos_ref[...]
    sin = sin_ref[...]
    bm, bn = acc.shape
    for j in range(bn // HEAD_DIM):
        t = _rope(acc[:, j * HEAD_DIM:(j + 1) * HEAD_DIM], cos, sin).astype(o_ref.dtype)
        for r in range(bm // Q_BLOCK):
            o_ref[r, j] = t[r * Q_BLOCK:(r + 1) * Q_BLOCK, :]


def _proj_rope_headmajor(xn, w, col_starts, width, cos, sin, bm=1024, bn=1024, side=None):
    m, k = xn.shape
    n = width * len(col_starts)
    bm = min(bm, m)
    per = width // bn
    first, second = (c // bn for c in col_starts)

    def w_block(i, j):
        return 0, jnp.where(j < per, first + j, second + j - per)

    return _matmul_call(
        _proj_rope_hm_kernel, (m // bm, n // bn),
        [pl.BlockSpec((bm, k), lambda i, j: (i, 0)),
         pl.BlockSpec((k, bn), w_block),
         pl.BlockSpec((bm, HEAD_DIM), lambda i, j: (i, 0)),
         pl.BlockSpec((bm, HEAD_DIM), lambda i, j: (i, 0))],
        pl.BlockSpec((bm // Q_BLOCK, bn // HEAD_DIM, Q_BLOCK, HEAD_DIM), lambda i, j: (i, j, 0, 0)),
        jax.ShapeDtypeStruct((m // Q_BLOCK, n // HEAD_DIM, Q_BLOCK, HEAD_DIM), BF16),
        (xn, w, cos, sin), "proj_q", side=side)


def _proj_kv_kernel(x_ref, wk_ref, wki_ref, wv_ref, cos_ref, sin_ref, o_ref):
    x = x_ref[...]
    cos = cos_ref[...]
    sin = sin_ref[...]
    col = 0
    for w_ref, rotary in ((wk_ref, True), (wki_ref, True), (wv_ref, False)):
        acc = _dot(x, w_ref[...])
        for j in range(acc.shape[1] // HEAD_DIM):
            t = acc[:, j * HEAD_DIM:(j + 1) * HEAD_DIM]
            if rotary:
                t = _rope(t, cos, sin)
            o_ref[:, col:col + HEAD_DIM] = t.astype(o_ref.dtype)
            col += HEAD_DIM


def _proj_kv(xn, w, col_k, col_ki, col_v, cos, sin, bm=1024):
    m, k = xn.shape
    kvd = N_KV_A * HEAD_DIM
    n = 2 * kvd + HEAD_DIM
    bm = min(bm, m)
    return pl.pallas_call(
        _proj_kv_kernel,
        grid=(m // bm,),
        in_specs=[pl.BlockSpec((bm, k), lambda i: (i, 0)),
                  pl.BlockSpec((k, kvd), lambda i: (0, col_k // kvd)),
                  pl.BlockSpec((k, HEAD_DIM), lambda i: (0, col_ki // HEAD_DIM)),
                  pl.BlockSpec((k, kvd), lambda i: (0, col_v // kvd)),
                  pl.BlockSpec((bm, HEAD_DIM), lambda i: (i, 0)),
                  pl.BlockSpec((bm, HEAD_DIM), lambda i: (i, 0))],
        out_specs=pl.BlockSpec((bm, n), lambda i: (i, 0)),
        out_shape=jax.ShapeDtypeStruct((m, n), BF16),
        compiler_params=_cparams(("arbitrary",), 52),
        name="proj_kv",
    )(xn, w, w, w, cos, sin)


def _mm_kernel(x_ref, w_ref, o_ref):
    o_ref[...] = _dot(x_ref[...], w_ref[...]).astype(o_ref.dtype)


def _matmul(x, w, out_dtype, bm=1024, bn=1024, name="matmul", side=None):
    m, k = x.shape
    n = w.shape[1]
    bm = min(bm, m)
    return _matmul_call(
        _mm_kernel, (m // bm, n // bn),
        [pl.BlockSpec((bm, k), lambda i, j: (i, 0)),
         pl.BlockSpec((k, bn), lambda i, j: (0, j))],
        pl.BlockSpec((bm, bn), lambda i, j: (i, j)),
        jax.ShapeDtypeStruct((m, n), out_dtype), (x, w), name, side=side)


WORD_BITS = 32
SUM_ROWS = 16
KEYS_PER_WORD_GROUP = WORD_BITS * SUBLANES


def _bit_transpose32(words):
    a = list(words)
    j, m = 16, 0x0000FFFF
    while j:
        mask = jnp.int32(m - (1 << 32) if m >= (1 << 31) else m)
        k = 0
        while k < WORD_BITS:
            t = (a[k] ^ lax.shift_right_logical(a[k + j], jnp.full_like(a[k], j))) & mask
            a[k] = a[k] ^ t
            a[k + j] = a[k + j] ^ (t << j)
            k = (k + j + 1) & ~j
        j >>= 1
        m = (m ^ (m << j)) & 0xFFFFFFFF
    return a


def _popcount_rows(words):
    per_sublane = jnp.sum(lax.population_count(words).reshape(-1, SUBLANES, words.shape[1]), axis=0)
    return jnp.sum(per_sublane.astype(F32), axis=0, keepdims=True)


def _attn_kernel(q_ref, qi_ref, wi_ref, k_ref, ki_ref, vt_ref, o_ref,
                 plane_ref, sel_ref, bias_ref, m_ref, acc_ref, alpha_ref, s_ref, p_ref,
                 *, topk, tk, idx_bits):
    i = pl.program_id(1)
    nk = (i + 1) * Q_BLOCK
    nch = (nk + tk - 1) // tk
    n_words = plane_ref.shape[1]
    groups_per_chunk = tk // KEYS_PER_WORD_GROUP

    @pl.when(i == 0)
    def _():
        plane_ref[...] = jnp.zeros_like(plane_ref)

    qi = qi_ref[0].reshape(N_HEADS_IDX * Q_BLOCK, HEAD_DIM)
    wi_t = jnp.transpose(wi_ref[...]) * (HEAD_DIM ** -0.5)
    qpos = i * Q_BLOCK + lax.broadcasted_iota(I32, (tk, Q_BLOCK), 1)
    krow = lax.broadcasted_iota(I32, (tk, Q_BLOCK), 0)

    def score_chunk(c):
        off = pl.multiple_of(c * tk, tk)
        lg = _dot_nt(ki_ref[0, pl.ds(off, tk), :], qi)
        yield
        sc = jnp.zeros((tk, Q_BLOCK), F32)
        for h in range(N_HEADS_IDX):
            sc = sc + wi_t[h:h + 1, :] * jnp.maximum(lg[:, h * Q_BLOCK:(h + 1) * Q_BLOCK], 0.0)
        yield
        bits = pltpu.bitcast(sc, I32)
        key = bits ^ ((bits >> 31) & 0x7FFFFFFF)
        key = jnp.where(key == -1, 0, key)
        ukey = jnp.where(off + krow <= qpos, key ^ INT_MIN, 0)
        for gi in range(groups_per_chunk):
            base = gi * KEYS_PER_WORD_GROUP
            planes = _bit_transpose32(
                [ukey[base + t * SUBLANES:base + (t + 1) * SUBLANES, :] for t in range(WORD_BITS)])
            row0 = pl.multiple_of((c * groups_per_chunk + gi) * SUBLANES, SUBLANES)
            for b in range(WORD_BITS):
                plane_ref[b, pl.ds(row0, SUBLANES), :] = planes[WORD_BITS - 1 - b]
            yield

    def score_chunks(chunks):
        pieces = [score_chunk(c) for c in chunks]
        while pieces:
            pieces = [t for t in pieces if next(t, "done") != "done"]

    def score_quad(j, carry):
        score_chunks((4 * j, 4 * j + 1, 4 * j + 2, 4 * j + 3))
        return carry

    lax.fori_loop(0, nch // 4, score_quad, 0)

    @pl.when(nch % 4 >= 2)
    def _():
        base = (nch // 4) * 4
        score_chunks((base, base + 1))

    @pl.when(nch % 2 == 1)
    def _():
        score_chunks((nch - 1,))

    cand = jnp.full((n_words, Q_BLOCK), -1, I32)
    greater = jnp.zeros((n_words, Q_BLOCK), I32)
    cnt_gt = jnp.zeros((1, Q_BLOCK), F32)
    for b in range(WORD_BITS - 1, -1, -1):
        ones = cand & plane_ref[b]
        cnt = _popcount_rows(ones)
        take = cnt_gt + cnt >= topk
        greater = jnp.where(take, greater, greater | ones)
        cnt_gt = jnp.where(take, cnt_gt, cnt_gt + cnt)
        cand = jnp.where(take, ones, cand ^ ones)

    word_row = lax.broadcasted_iota(I32, (n_words, Q_BLOCK), 0)
    word_pos = (word_row >> 3) * KEYS_PER_WORD_GROUP + (word_row & (SUBLANES - 1))
    qcol = i * Q_BLOCK + lax.broadcasted_iota(I32, (n_words, Q_BLOCK), 1)

    def prefix(limit):
        nt = jnp.clip((limit - word_pos + (SUBLANES - 1)) >> 3, 0, WORD_BITS)
        top = lax.shift_right_arithmetic(jnp.full_like(nt, INT_MIN), jnp.maximum(nt, 1) - 1)
        return jnp.where(nt <= 0, 0, top)

    cand = cand & prefix(qcol + 1)
    need = topk - cnt_gt
    sel_ref[...] = greater | cand

    @pl.when(jnp.max(_popcount_rows(cand) - need) > 0)
    def _():
        def jbody(bi, jt):
            cj = jt | jnp.left_shift(jnp.int32(1), idx_bits - 1 - bi)
            return jnp.where(_popcount_rows(cand & prefix(cj)) <= need, cj, jt)
        jt = lax.fori_loop(0, idx_bits, jbody, jnp.zeros((1, Q_BLOCK), I32))
        sel_ref[...] = greater | (cand & prefix(jt))

    def bias_body(c, carry):
        off = pl.multiple_of(c * tk, tk)
        for gi in range(groups_per_chunk):
            row0 = pl.multiple_of((c * groups_per_chunk + gi) * SUBLANES, SUBLANES)
            w = sel_ref[pl.ds(row0, SUBLANES), :]
            for t in range(WORD_BITS):
                dst = pl.multiple_of(off + gi * KEYS_PER_WORD_GROUP + t * SUBLANES, SUBLANES)
                bias_ref[pl.ds(dst, SUBLANES), :] = jnp.where((w << t) < 0, 0.0, -jnp.inf)
        return carry

    lax.fori_loop(0, nch, bias_body, 0)

    n_rep = N_HEADS_A // N_KV_A
    cols = n_rep * Q_BLOCK
    q_all = q_ref[0].reshape(N_HEADS_A * Q_BLOCK, HEAD_DIM)
    scale2 = (HEAD_DIM ** -0.5) * LOG2_E
    m_ref[...] = jnp.full(m_ref.shape, -1e30, F32)
    acc_ref[...] = jnp.zeros(acc_ref.shape, F32)
    ones_rows = jnp.ones((SUM_ROWS, tk), BF16)

    def stage_qk(g, off):
        qg = q_all[g * cols:(g + 1) * cols, :]
        s_ref[g] = _dot_nt(k_ref[0, pl.ds(off, tk), g * HEAD_DIM:(g + 1) * HEAD_DIM], qg)

    def stage_max(g, off):
        for n in range(n_rep):
            csl = slice(n * Q_BLOCK, (n + 1) * Q_BLOCK)
            t = s_ref[g, :, csl] + bias_ref[pl.ds(off, tk), :]
            s_ref[g, :, csl] = t
            m_old = m_ref[g, :, csl]
            m_new = jnp.maximum(m_old, jnp.max(t, axis=0, keepdims=True))
            alpha_ref[g, :, csl] = jnp.exp2((m_old - m_new) * scale2)
            m_ref[g, :, csl] = m_new

    def stage_exp(g, off):
        for n in range(n_rep):
            csl = slice(n * Q_BLOCK, (n + 1) * Q_BLOCK)
            p_ref[g, :, csl] = jnp.exp2((s_ref[g, :, csl] - m_ref[g, 0:1, csl]) * scale2).astype(BF16)

    def stage_pv(g, off):
        v_ext = jnp.concatenate(
            [vt_ref[0, g * HEAD_DIM:(g + 1) * HEAD_DIM, pl.ds(off, tk)], ones_rows], axis=0)
        acc_ref[g] = alpha_ref[g, 0:1, :] * acc_ref[g] + _dot(v_ext, p_ref[g])

    stages = (stage_qk, stage_max, stage_exp, stage_pv)

    def attend_chunks(chunks):
        pieces = [(g, pl.multiple_of(c * tk, tk)) for c in chunks for g in range(N_KV_A)]
        for step in range(len(pieces) + len(stages) - 1):
            for k, (g, off) in enumerate(pieces):
                if 0 <= step - k < len(stages):
                    stages[step - k](g, off)

    def att_quad(j, carry):
        attend_chunks((4 * j, 4 * j + 1, 4 * j + 2, 4 * j + 3))
        return carry

    lax.fori_loop(0, nch // 4, att_quad, 0)

    @pl.when(nch % 4 >= 2)
    def _():
        base = (nch // 4) * 4
        attend_chunks((base, base + 1))

    @pl.when(nch % 2 == 1)
    def _():
        attend_chunks((nch - 1,))
    for g in range(N_KV_A):
        acc = acc_ref[g]
        o_t = acc[0:HEAD_DIM, :] / acc[HEAD_DIM:HEAD_DIM + 1, :]
        for n in range(n_rep):
            h = g * n_rep + n
            o_ref[:, h * HEAD_DIM:(h + 1) * HEAD_DIM] = jnp.transpose(
                o_t[:, n * Q_BLOCK:(n + 1) * Q_BLOCK]).astype(o_ref.dtype)


def _attention(qh, wi, kkiv, vt, batch, seq_len):
    nb = seq_len // Q_BLOCK
    topk = min(TOPK_MAX, seq_len // 4)
    tk = min(512, seq_len)
    kvd = N_KV_A * HEAD_DIM
    cols = N_HEADS_A // N_KV_A * Q_BLOCK
    single = pl.Buffered(1)
    return pl.pallas_call(
        functools.partial(_attn_kernel, topk=topk, tk=tk, idx_bits=seq_len.bit_length()),
        grid=(batch, nb),
        in_specs=[pl.BlockSpec((1, N_HEADS_A, Q_BLOCK, HEAD_DIM), lambda b, i: (b * nb + i, 0, 0, 0)),
                  pl.BlockSpec((1, N_HEADS_IDX, Q_BLOCK, HEAD_DIM), lambda b, i: (b * nb + i, 1, 0, 0)),
                  pl.BlockSpec((Q_BLOCK, LANES), lambda b, i: (b * nb + i, 0)),
                  pl.BlockSpec((1, seq_len, kvd), lambda b, i: (b, 0, 0), pipeline_mode=single),
                  pl.BlockSpec((1, seq_len, HEAD_DIM), lambda b, i: (b, 0, kvd // HEAD_DIM),
                               pipeline_mode=single),
                  pl.BlockSpec((1, kvd, seq_len), lambda b, i: (b, 0, 0), pipeline_mode=single)],
        out_specs=pl.BlockSpec((Q_BLOCK, N_HEADS_A * HEAD_DIM), lambda b, i: (b * nb + i, 0)),
        out_shape=jax.ShapeDtypeStruct((batch * seq_len, N_HEADS_A * HEAD_DIM), BF16),
        scratch_shapes=[pltpu.VMEM((WORD_BITS, seq_len // WORD_BITS, Q_BLOCK), I32),
                        pltpu.VMEM((seq_len // WORD_BITS, Q_BLOCK), I32),
                        pltpu.VMEM((seq_len, Q_BLOCK), F32),
                        pltpu.VMEM((N_KV_A, SUBLANES, cols), F32),
                        pltpu.VMEM((N_KV_A, HEAD_DIM + SUM_ROWS, cols), F32),
                        pltpu.VMEM((N_KV_A, SUBLANES, cols), F32),
                        pltpu.VMEM((N_KV_A, tk, cols), F32),
                        pltpu.VMEM((N_KV_A, tk, cols), BF16)],
        compiler_params=_cparams(("arbitrary", "arbitrary"), 52),
        name="dsa_attention",
    )(qh, qh, wi, kkiv, kkiv, vt)


def _dot_hp(x, w_bf):
    hi = x.astype(BF16)
    r1 = x - hi.astype(F32)
    mid = r1.astype(BF16)
    lo = (r1 - mid.astype(F32)).astype(BF16)
    n = x.shape[0]
    parts = _dot(jnp.concatenate([hi, mid, lo], axis=0), w_bf)
    return parts[0:n] + parts[n:2 * n] + parts[2 * n:3 * n]


def _rwkv_kernel(r_ref, k_ref, v_ref, lw_ref, a_ref, g_ref, mu_ref, kk_ref, ka_ref, rk_ref,
                 lnw_ref, lnb_ref, bd_ref, ones_ref, ltri_ref, ts_ref, ti_ref, ic_ref, eye_ref,
                 o_ref, st_ref, prev_ref):
    @pl.when(pl.program_id(2) == 0)
    def _():
        st_ref[...] = jnp.zeros_like(st_ref)
        prev_ref[...] = jnp.zeros_like(prev_ref)

    pieces = [_rwkv_tile(tile, row0, r_ref, k_ref, v_ref, lw_ref, a_ref, g_ref, mu_ref, kk_ref, ka_ref,
                         rk_ref, lnw_ref, lnb_ref, bd_ref, ones_ref, ltri_ref, ts_ref, ti_ref, ic_ref,
                         eye_ref, o_ref, st_ref, prev_ref)
              for row0 in range(0, r_ref.shape[0], CHUNK)
              for tile in range(r_ref.shape[1] // TILE_B)]
    while pieces:
        pieces = [t for t in pieces if next(t, "done") != "done"]


def _rwkv_tile(tile, row0, r_ref, k_ref, v_ref, lw_ref, a_ref, g_ref, mu_ref, kk_ref, ka_ref, rk_ref,
               lnw_ref, lnb_ref, bd_ref, ones_ref, ltri_ref, ts_ref, ti_ref, ic_ref, eye_ref,
               o_ref, st_ref, prev_ref):
    cols = slice(tile * TILE_B, (tile + 1) * TILE_B)
    rows = slice(row0, row0 + CHUNK)
    bdm = bd_ref[...]
    row = lax.broadcasted_iota(I32, (CHUNK, TILE_B), 0)

    def shifted(x, slot):
        prev = prev_ref[slot:slot + 1, cols]
        prev_ref[slot:slot + 1, cols] = x[CHUNK - 1:CHUNK, :]
        return jnp.where(row == 0, prev, pltpu.roll(x, 1, axis=0))

    r0 = r_ref[rows, cols]
    k0 = k_ref[rows, cols]
    v0 = v_ref[rows, cols]
    r = r0 + (shifted(r0, 0) - r0) * mu_ref[0:1, cols]
    k = k0 + (shifted(k0, 1) - k0) * mu_ref[1:2, cols]
    v = v0 + (shifted(v0, 2) - v0) * mu_ref[2:3, cols]
    a = a_ref[rows, cols]
    ones_bd = ones_ref[...]

    kk = k * kk_ref[:, cols]
    k2 = k * (1.0 + (a - 1.0) * ka_ref[:, cols])
    head_sums = _dot_hp(jnp.concatenate([kk * kk, r * k2 * rk_ref[:, cols]], axis=0), ones_bd)
    ss = head_sums[0:CHUNK]
    bonus = head_sums[CHUNK:2 * CHUNK] * v
    yield
    kk = kk / jnp.maximum(jnp.sqrt(ss), 1e-12)
    aa = -kk
    bb = kk * a

    lw = lw_ref[rows, cols]
    cs = _cumsum_rows(lw, ltri_ref[...])
    yield
    tot = cs[CHUNK - 1:CHUNK, :]
    e_in = jnp.exp(cs)
    e_out = jnp.exp(-cs)
    e_tail = jnp.exp(tot - cs)
    at = aa * jnp.exp(cs - lw)
    rt = r * e_in
    bt = (bb * e_out)
    kt = (k2 * e_out)
    bh = (bb * e_tail).astype(BF16)
    kh = (k2 * e_tail).astype(BF16)
    wc = jnp.exp(tot)

    def bd(x):
        return jnp.concatenate([x.astype(BF16)] * HEADS_PER_TILE, axis=0) * ones_bd

    lhs = jnp.concatenate([at, rt], axis=0).astype(BF16)
    mb = _dot_nt(lhs, bd(bt))
    mk = _dot_nt(lhs, bd(kt))
    yield
    ts = ts_ref[...]
    ti = ti_ref[...]
    m_ab = mb[0:CHUNK] * ts
    m_rb = (mb[CHUNK:2 * CHUNK] * ti).astype(BF16)
    m_ak = (mk[0:CHUNK] * ts).astype(BF16)
    m_rk = (mk[CHUNK:2 * CHUNK] * ti).astype(BF16)

    n_round = CHUNK.bit_length() - 1
    t_inv = ic_ref[...] + m_ab
    m_pow = _dot(m_ab.astype(BF16), bd(m_ab))
    yield
    for rnd in range(1, n_round):
        last = rnd == n_round - 1
        lhs_rows = [t_inv] if last else [t_inv, m_pow]
        prod = _dot(jnp.concatenate(lhs_rows, axis=0).astype(BF16), bd(m_pow))
        yield
        t_inv = t_inv + prod[0:CHUNK]
        if not last:
            m_pow = prod[CHUNK:2 * CHUNK]
    t_bf = t_inv.astype(BF16)

    p = _dot(t_bf, bd(at))
    uv = _dot(jnp.concatenate([m_ak, m_rk], axis=0), bd(v))
    u = uv[0:CHUNK]
    yield
    q = _dot(t_bf, bd(u))
    rp = rt + _dot(m_rb, bd(p))
    yield
    y0 = _dot(m_rb, bd(q)) + uv[CHUNK:2 * CHUNK]

    yield
    a_t = _dot_tn(bh, p.astype(BF16)) * bdm + eye_ref[...] * wc
    d_t = _dot_tn(jnp.concatenate([bh, kh], axis=0),
                  jnp.concatenate([q.astype(BF16), v.astype(BF16)], axis=0)) * bdm
    st = st_ref[tile]
    from_state = _dot(jnp.concatenate([rp.astype(BF16), a_t.astype(BF16)], axis=0), st.astype(BF16))
    y = from_state[0:CHUNK] + y0
    st_ref[tile] = from_state[CHUNK:CHUNK + TILE_B] + d_t

    yield
    inv_n = 1.0 / HEAD_DIM_B
    mu = _dot_hp(y, ones_bd) * inv_n
    yield
    yc = y - mu
    var = _dot_hp(yc * yc, ones_bd) * inv_n
    yield
    yn =(yc * lax.rsqrt(var + GN_EPS)) * lnw_ref[:, cols] + lnb_ref[:, cols]
    o_ref[rows, cols] = ((yn + bonus) * g_ref[rows, cols]).astype(o_ref.dtype)


def _cumsum_rows(x, ltri_bf):
    hi = x.astype(BF16)
    r1 = x - hi.astype(F32)
    mid = r1.astype(BF16)
    lo = (r1 - mid.astype(F32)).astype(BF16)
    return _dot(ltri_bf, hi) + _dot(ltri_bf, mid) + _dot(ltri_bf, lo)


def _rwkv_consts():
    idx = jnp.arange(TILE_B)
    head = idx // HEAD_DIM_B
    bdm = (head[:, None] == head[None, :])
    t = jnp.arange(CHUNK)[:, None]
    s = (idx % CHUNK)[None, :]
    return dict(
        bd=bdm.astype(F32),
        ones=bdm.astype(BF16),
        ltri=(jnp.arange(CHUNK)[None, :] <= t).astype(BF16),
        ts=(s < t).astype(F32),
        ti=(s <= t).astype(F32),
        ic=(s == t).astype(F32),
        eye=jnp.eye(TILE_B, dtype=F32),
    )


def _rwkv(rkv, lw, a, g, mu_rkv, k_k, k_a, r_k, ln_w, ln_b, batch, seq_len):
    m, db = lw.shape
    block_rows = RWKV_CHUNKS_PER_STEP * CHUNK
    nc = seq_len // block_rows
    width = RWKV_TILES_PER_STEP * TILE_B
    ng = db // width
    cst = _rwkv_consts()
    blk = lambda off: pl.BlockSpec((block_rows, width), lambda b, h, c, off=off: (b * nc + c, off + h))
    par = lambda rows: pl.BlockSpec((rows, width), lambda b, h, c: (0, h))
    full = lambda arr: pl.BlockSpec(arr.shape, lambda b, h, c: (0, 0))
    row = lambda p: p.reshape(1, db)
    return pl.pallas_call(
        _rwkv_kernel,
        grid=(batch, ng, nc),
        in_specs=[blk(0), blk(ng), blk(2 * ng), blk(0), blk(0), blk(0),
                  par(3), par(1), par(1), par(1), par(1), par(1),
                  full(cst["bd"]), full(cst["ones"]), full(cst["ltri"]), full(cst["ts"]),
                  full(cst["ti"]), full(cst["ic"]), full(cst["eye"])],
        out_specs=blk(0),
        out_shape=jax.ShapeDtypeStruct((m, db), BF16),
        scratch_shapes=[pltpu.VMEM((RWKV_TILES_PER_STEP, TILE_B, TILE_B), F32),
                        pltpu.VMEM((SUBLANES, width), F32)],
        compiler_params=_cparams(("arbitrary", "arbitrary", "arbitrary"), 32),
        name="rwkv7",
    )(rkv, rkv, rkv, lw, a, g, mu_rkv, row(k_k), row(k_a), row(r_k), row(ln_w), row(ln_b),
      cst["bd"], cst["ones"], cst["ltri"], cst["ts"], cst["ti"], cst["ic"], cst["eye"])


def _merge_kernel(xn_ref, oa_ref, ob_ref, wga_ref, wgb_ref, bga_ref, bgb_ref, wpa_ref, wpb_ref, o_ref):
    xn = xn_ref[...]
    ga = _sigmoid(_dot(xn, wga_ref[...]) + bga_ref[...])
    gb = _sigmoid(_dot(xn, wgb_ref[...]) + bgb_ref[...])
    ya = _dot(oa_ref[...], wpa_ref[...])
    yb = _dot(ob_ref[...], wpb_ref[...])
    o_ref[...] = (ga * ya + gb * yb).astype(o_ref.dtype)


def _merge(xn, oa, ob, w_gate, b_gate, w_pa, w_pb, bm=1024, bn=256, side=None):
    m, d = xn.shape
    bm = min(bm, m)
    nj = d // bn
    return _matmul_call(
        _merge_kernel, (m // bm, nj),
        [pl.BlockSpec((bm, d), lambda i, j: (i, 0)),
         pl.BlockSpec((bm, oa.shape[1]), lambda i, j: (i, 0)),
         pl.BlockSpec((bm, ob.shape[1]), lambda i, j: (i, 0)),
         pl.BlockSpec((d, bn), lambda i, j: (0, j)),
         pl.BlockSpec((d, bn), lambda i, j: (0, j + nj)),
         pl.BlockSpec((1, bn), lambda i, j: (0, j)),
         pl.BlockSpec((1, bn), lambda i, j: (0, j + nj)),
         pl.BlockSpec((w_pa.shape[0], bn), lambda i, j: (0, j)),
         pl.BlockSpec((w_pb.shape[0], bn), lambda i, j: (0, j))],
        pl.BlockSpec((bm, bn), lambda i, j: (i, j)),
        jax.ShapeDtypeStruct((m, d), BF16),
        (xn, oa, ob, w_gate, w_gate, b_gate, b_gate, w_pa, w_pb), "gated_merge", vmem_mib=56, side=side)


def _mm_res_kernel(x_ref, w_ref, res_ref, o_ref):
    o_ref[...] = res_ref[...] + _dot(x_ref[...], w_ref[...])


def _matmul_residual(x, w, res, bm=1024, bn=1024, name="matmul_res", side=None):
    m, k = x.shape
    n = w.shape[1]
    bm = min(bm, m)
    return _matmul_call(
        _mm_res_kernel, (m // bm, n // bn),
        [pl.BlockSpec((bm, k), lambda i, j: (i, 0)),
         pl.BlockSpec((k, bn), lambda i, j: (0, j)),
         pl.BlockSpec((bm, bn), lambda i, j: (i, j))],
        pl.BlockSpec((bm, bn), lambda i, j: (i, j)),
        jax.ShapeDtypeStruct((m, n), F32), (x, w, res), name, side=side)


def _ffn_up_kernel(x_ref, w1_ref, w3_ref, o_ref):
    x = x_ref[...]
    h1 = _dot(x, w1_ref[...])
    h3 = _dot(x, w3_ref[...])
    o_ref[...] = (h1 * _sigmoid(h1) * h3).astype(o_ref.dtype)


def _ffn_up(x, w1, w3, bm=1024, bn=512, side=None):
    m, k = x.shape
    n = w1.shape[1]
    bm = min(bm, m)
    return _matmul_call(
        _ffn_up_kernel, (m // bm, pl.cdiv(n, bn)),
        [pl.BlockSpec((bm, k), lambda i, j: (i, 0)),
         pl.BlockSpec((k, bn), lambda i, j: (0, j)),
         pl.BlockSpec((k, bn), lambda i, j: (0, j))],
        pl.BlockSpec((bm, bn), lambda i, j: (i, j)),
        jax.ShapeDtypeStruct((m, n), BF16), (x, w1, w3), "ffn_up", side=side)


def _ple_kernel(h_ref, wg_ref, p_ref, wp_ref, gain_ref, o_ref, hn_ref, *, bn, final_norm):
    j = pl.program_id(1)
    n_row_chunks = h_ref.shape[0] // NORM_ROW_CHUNK

    def row_chunk(r):
        return pl.ds(pl.multiple_of(r * NORM_ROW_CHUNK, NORM_ROW_CHUNK), NORM_ROW_CHUNK)

    @pl.when(j == 0)
    def _():
        def body(r, carry):
            hn_ref[row_chunk(r), :] = _rms_rows(h_ref[row_chunk(r), :]).astype(BF16)
            return carry
        lax.fori_loop(0, n_row_chunks, body, 0)

    col = pl.multiple_of(j * bn, bn)
    gate = _sigmoid(_dot(hn_ref[...], wg_ref[...]))
    o_ref[:, pl.ds(col, bn)] = h_ref[:, pl.ds(col, bn)] + gate * _dot(p_ref[...], wp_ref[...])

    if final_norm:
        @pl.when(j == pl.num_programs(1) - 1)
        def _():
            def body(r, carry):
                o_ref[row_chunk(r), :] = _rms_rows(o_ref[row_chunk(r), :]) * gain_ref[...]
                return carry
            lax.fori_loop(0, n_row_chunks, body, 0)


def _ple(h, wg, p, wp, final_gain, bm=512, bn=512):
    m, d = h.shape
    bm = min(bm, m)
    final_norm = final_gain is not None
    gain = (final_gain if final_norm else jnp.ones((d,), F32)).reshape(1, d)
    return pl.pallas_call(
        functools.partial(_ple_kernel, bn=bn, final_norm=final_norm),
        grid=(m // bm, d // bn),
        in_specs=[pl.BlockSpec((bm, d), lambda i, j: (i, 0)),
                  pl.BlockSpec((d, bn), lambda i, j: (0, j)),
                  pl.BlockSpec((bm, p.shape[1]), lambda i, j: (i, 0)),
                  pl.BlockSpec((p.shape[1], bn), lambda i, j: (0, j)),
                  pl.BlockSpec((1, d), lambda i, j: (0, 0))],
        out_specs=pl.BlockSpec((bm, d), lambda i, j: (i, 0)),
        out_shape=jax.ShapeDtypeStruct((m, d), F32),
        scratch_shapes=[pltpu.VMEM((bm, d), BF16)],
        compiler_params=_cparams(("arbitrary", "arbitrary"), 52),
        name="ple",
    )(h, wg, p, wp, gain)


def _rmsnorm_kernel(x_ref, g_ref, o_ref):
    o_ref[...] = (_rms_rows(x_ref[...]) * g_ref[...]).astype(o_ref.dtype)


def _rmsnorm(x, gain, out_dtype, bm=512):
    m, d = x.shape
    bm = min(bm, m)
    row_spec = pl.BlockSpec((bm, d), lambda i: (i, 0))
    return pl.pallas_call(
        _rmsnorm_kernel,
        grid=(m // bm,),
        in_specs=[row_spec, pl.BlockSpec((1, d), lambda i: (0, 0))],
        out_specs=row_spec,
        out_shape=jax.ShapeDtypeStruct((m, d), out_dtype),
        compiler_params=_cparams(("arbitrary",), 48),
        name="rmsnorm",
    )(x, gain.reshape(1, d))


def _pad_to(a, axis, size):
    pad = [(0, 0)] * a.ndim
    pad[axis] = (0, size - a.shape[axis])
    return jnp.pad(a, pad)


def _layer(h2, p2, cos, sin, batch, seq_len, norm_mix, w_in, mu_rkv, mu_wag, w0, w1, w2, a0, a1, a2,
           g1, g2, k_k, k_a, r_k, ln_w, ln_b, w_pa, w_pb, w_gate, b_gate, w_o, norm_ffn,
           w_ffn1, w_ffn3, w_ffn2, w_ple_gate, w_ple, final_gain):
    d = h2.shape[1]
    d_a = N_HEADS_A * HEAD_DIM
    kvd = N_KV_A * HEAD_DIM
    d_iq = N_HEADS_IDX * HEAD_DIM
    d_b = w_pb.shape[0]
    o_q, o_k, o_v = 0, d_a, d_a + kvd
    o_qi = o_v + kvd
    o_ki = o_qi + d_iq
    o_wi = o_ki + HEAD_DIM
    o_r = o_wi + N_HEADS_IDX

    bf = lambda a: a.astype(BF16)
    w_in_bf = bf(w_in)
    assert d_a == d_iq and o_wi % LANES == 0
    w_rkv = w_in_bf[:, o_r:o_r + 3 * d_b]
    lora = LANES
    w1p, a1p = bf(_pad_to(w1, 1, lora)), bf(_pad_to(a1, 1, lora))
    w2p, a2p = bf(_pad_to(w2, 0, lora)), bf(_pad_to(a2, 0, lora))

    xn, lw, a, g, wi = _prep(h2, norm_mix.reshape(1, d), mu_wag, w1p, w2p, w0.reshape(1, d_b),
                             a1p, a2p, a0.reshape(1, d_b), bf(g1), bf(g2), w_in_bf, o_wi // LANES,
                             seq_len)

    qh, w_o_bf = _proj_rope_headmajor(xn, w_in_bf, (o_q, o_qi), d_a, cos, sin, side=w_o)
    kkiv = _proj_kv(xn, w_in_bf, o_k, o_ki, o_v, cos, sin)
    rkv, w_gate_bf = _matmul(xn, w_rkv, F32, bn=512, name="proj_rkv", side=(w_gate, 4))

    kkiv = kkiv.reshape(batch, seq_len, -1)
    vt = jnp.swapaxes(kkiv[:, :, kvd + HEAD_DIM:], 1, 2)
    o_att = _attention(qh, wi, kkiv, vt, batch, seq_len)

    o_rwkv = _rwkv(rkv, lw, a, g, mu_rkv, k_k, k_a, r_k.reshape(-1), ln_w, ln_b, batch, seq_len)

    mixed, w_ffn1_bf = _merge(xn, o_att, o_rwkv, w_gate_bf, b_gate.reshape(1, -1), bf(w_pa), bf(w_pb),
                              side=w_ffn1)
    h2, w_ffn3_bf = _matmul_residual(mixed, w_o_bf, h2, bn=512, name="out_proj", side=(w_ffn3, 2))

    xf = _rmsnorm(h2, norm_ffn, BF16)
    u, w_ffn2_bf = _ffn_up(xf, w_ffn1_bf, w_ffn3_bf, side=(w_ffn2, 4))
    h2, w_ple_gate_bf = _matmul_residual(u, w_ffn2_bf, h2, bm=512, bn=512, name="ffn_down",
                                         side=w_ple_gate)

    return _ple(h2, w_ple_gate_bf, bf(p2), bf(w_ple), final_gain)


def kernel(x, p, positions, norm_mix, w_in, mu_rkv, mu_wag, w0, w1, w2, a0, a1, a2, g1, g2, k_k, k_a,
           r_k, ln_w, ln_b, w_pa, w_pb, w_gate, b_gate, w_o, norm_ffn, w_ffn1, w_ffn3, w_ffn2,
           w_ple_gate, w_ple, norm_final):
    batch, seq_len, d = x.shape
    depth = p.shape[0]
    h2 = x.reshape(batch * seq_len, d)
    cos, sin = _rope_tables(positions)
    for i in range(depth):
        h2 = _layer(h2, p[i].reshape(batch * seq_len, -1), cos, sin, batch, seq_len,
                    norm_mix[i], w_in[i], mu_rkv[i], mu_wag[i], w0[i], w1[i], w2[i], a0[i], a1[i], a2[i],
                    g1[i], g2[i], k_k[i], k_a[i], r_k[i], ln_w[i], ln_b[i], w_pa[i], w_pb[i], w_gate[i],
                    b_gate[i], w_o[i], norm_ffn[i], w_ffn1[i], w_ffn3[i], w_ffn2[i], w_ple_gate[i],
                    w_ple[i], norm_final if i == depth - 1 else None)
    return h2.reshape(batch, seq_len, d)
```

```python
import functools

import jax
import jax.numpy as jnp
from jax import lax
from jax.experimental import pallas as pl
from jax.experimental.pallas import tpu as pltpu

F32 = jnp.float32
BF16 = jnp.bfloat16
I32 = jnp.int32

N_HEADS_A = 16
HEAD_DIM = 128
N_KV_A = 4
N_HEADS_IDX = 16
TOPK_MAX = 256
Q_BLOCK = 128
ROPE_THETA = 10000.0
HEAD_DIM_B = 64
GN_EPS = 64e-5
RMS_EPS = 1e-6

LANES = 128
SUBLANES = 8
BF16_SUBLANES = 16
MXU_DIM = 256

INT_MIN = -2 ** 31
LOG2_E = 1.4426950408889634
NORM_ROW_CHUNK = 64
DECAY_SCALE = 0.6065306597126334

CHUNK = 64
HEADS_PER_TILE = MXU_DIM // HEAD_DIM_B
TILE_B = HEADS_PER_TILE * HEAD_DIM_B
RWKV_CHUNKS_PER_STEP = 4
RWKV_TILES_PER_STEP = 4


def _cparams(sem, vmem_mib):
    return pltpu.CompilerParams(dimension_semantics=sem, vmem_limit_bytes=vmem_mib << 20)


def _sidecar_cast(body, n_in):
    def kern(*refs):
        refs[n_in + 2][...] = refs[n_in][...].astype(BF16)
        body(*refs[:n_in], refs[n_in + 1])
    return kern


def _matmul_call(body, grid, in_specs, out_spec, out_shape, args, name, vmem_mib=52, side=None):
    params = _cparams(("arbitrary", "arbitrary"), vmem_mib)
    if side is None:
        out = pl.pallas_call(body, grid=grid, in_specs=in_specs, out_specs=out_spec,
                             out_shape=out_shape, compiler_params=params, name=name)(*args)
        return out, None
    steps, n_inner = grid[0] * grid[1], grid[1]
    rows = side.shape[0]
    rb = -(-(-(-rows // steps)) // BF16_SUBLANES) * BF16_SUBLANES
    last = -(-rows // rb) - 1
    side_spec = pl.BlockSpec((rb, side.shape[1]), lambda i, j: (jnp.minimum(i * n_inner + j, last), 0))
    return pl.pallas_call(
        _sidecar_cast(body, len(in_specs)), grid=grid, in_specs=[*in_specs, side_spec],
        out_specs=[out_spec, side_spec],
        out_shape=[out_shape, jax.ShapeDtypeStruct(side.shape, BF16)],
        compiler_params=params, name=name)(*args, side)


def _dot(a, b):
    return jnp.dot(a, b, preferred_element_type=F32)


def _dot_nt(a, b):
    return lax.dot_general(a, b, (((1,), (1,)), ((), ())), preferred_element_type=F32)


def _dot_tn(a, b):
    return lax.dot_general(a, b, (((0,), (0,)), ((), ())), preferred_element_type=F32)


def _sigmoid(x):
    return 1.0 / (1.0 + jnp.exp(-x))


def _rms_rows(x):
    return x * lax.rsqrt(jnp.mean(x * x, axis=-1, keepdims=True) + RMS_EPS)


def _rope_tab_kernel(pos_ref, freq_ref, sign_ref, cos_ref, sin_ref):
    ang = pos_ref[...].astype(F32) * freq_ref[...]
    cos_ref[...] = jnp.cos(ang)
    sin_ref[...] = jnp.sin(ang) * sign_ref[...]


def _rope_tables(positions):
    n = positions.size
    half = HEAD_DIM // 2
    inv_freq = ROPE_THETA ** (-jnp.arange(0, HEAD_DIM, 2, dtype=F32) / HEAD_DIM)
    freq2 = jnp.concatenate([inv_freq, inv_freq]).reshape(1, HEAD_DIM)
    sign = jnp.concatenate([-jnp.ones((half,), F32), jnp.ones((half,), F32)]).reshape(1, HEAD_DIM)
    bm = min(2048, n)
    return pl.pallas_call(
        _rope_tab_kernel,
        grid=(n // bm,),
        in_specs=[pl.BlockSpec((bm, 1), lambda i: (i, 0)),
                  pl.BlockSpec((1, HEAD_DIM), lambda i: (0, 0)),
                  pl.BlockSpec((1, HEAD_DIM), lambda i: (0, 0))],
        out_specs=[pl.BlockSpec((bm, HEAD_DIM), lambda i: (i, 0)),
                   pl.BlockSpec((bm, HEAD_DIM), lambda i: (i, 0))],
        out_shape=[jax.ShapeDtypeStruct((n, HEAD_DIM), F32)] * 2,
        compiler_params=_cparams(("arbitrary",), 32),
        name="rope_tables",
    )(positions.reshape(n, 1), freq2, sign)


def _rope(t, cos, sin):
    return t * cos + pltpu.roll(t, HEAD_DIM // 2, axis=1) * sin


def _prep_kernel(x_ref, xp_ref, gain_ref, mu_ref, w1_ref, w2_ref, w0_ref, a1_ref, a2_ref, a0_ref,
                 g1_ref, g2_ref, wwi_ref,
                 xn_ref, lw_ref, a_ref, g_ref, wi_ref, *, seq_len, bm):
    i = pl.program_id(0)
    gain = gain_ref[...]
    xn = _rms_rows(x_ref[...]) * gain
    prev = (_rms_rows(xp_ref[...]) * gain)[SUBLANES - 1:SUBLANES, :]
    prev = jnp.where((i * bm) % seq_len == 0, jnp.zeros_like(prev), prev)
    row = lax.broadcasted_iota(I32, xn.shape, 0)
    sh = jnp.where(row == 0, prev, pltpu.roll(xn, 1, axis=0))
    xx = sh - xn
    xn_bf = xn.astype(BF16)
    xn_ref[...] = xn_bf
    wi_ref[...] = _dot(xn_bf, wwi_ref[...]) * (N_HEADS_IDX ** -0.5)

    xw = (xn + xx * mu_ref[0:1, :]).astype(BF16)
    hw = jnp.tanh(_dot(xw, w1_ref[...])).astype(BF16)
    wl = w0_ref[...] + _dot(hw, w2_ref[...])
    lw_ref[...] = -DECAY_SCALE * _sigmoid(wl)

    xa = (xn + xx * mu_ref[1:2, :]).astype(BF16)
    ha = _dot(xa, a1_ref[...]).astype(BF16)
    a_ref[...] = _sigmoid(a0_ref[...] + _dot(ha, a2_ref[...]))

    xg = (xn + xx * mu_ref[2:3, :]).astype(BF16)
    hg = _sigmoid(_dot(xg, g1_ref[...])).astype(BF16)
    g_ref[...] = _dot(hg, g2_ref[...])


def _prep(x2, gain, mu_wag, w1, w2, w0, a1, a2, a0, g1, g2, w_in, wi_block, seq_len):
    m, d = x2.shape
    db = w2.shape[1]
    bm = min(256, m)
    full = lambda a: pl.BlockSpec(a.shape, lambda i: (0,) * a.ndim, pipeline_mode=pl.Buffered(1))
    nsub = bm // SUBLANES
    return pl.pallas_call(
        functools.partial(_prep_kernel, seq_len=seq_len, bm=bm),
        grid=(m // bm,),
        in_specs=[pl.BlockSpec((bm, d), lambda i: (i, 0)),
                  pl.BlockSpec((SUBLANES, d), lambda i: (jnp.maximum(i * nsub - 1, 0), 0)),
                  full(gain), full(mu_wag), full(w1), full(w2), full(w0), full(a1), full(a2), full(a0),
                  full(g1), full(g2), pl.BlockSpec((d, LANES), lambda i: (0, wi_block))],
        out_specs=[pl.BlockSpec((bm, d), lambda i: (i, 0)),
                   pl.BlockSpec((bm, db), lambda i: (i, 0)),
                   pl.BlockSpec((bm, db), lambda i: (i, 0)),
                   pl.BlockSpec((bm, db), lambda i: (i, 0)),
                   pl.BlockSpec((bm, LANES), lambda i: (i, 0))],
        out_shape=[jax.ShapeDtypeStruct((m, d), BF16),
                   jax.ShapeDtypeStruct((m, db), F32),
                   jax.ShapeDtypeStruct((m, db), F32),
                   jax.ShapeDtypeStruct((m, db), F32),
                   jax.ShapeDtypeStruct((m, LANES), F32)],
        compiler_params=_cparams(("arbitrary",), 48),
        name="prep",
    )(x2, x2, gain, mu_wag, w1, w2, w0, a1, a2, a0, g1, g2, w_in)


def _proj_rope_hm_kernel(x_ref, w_ref, cos_ref, sin_ref, o_ref):
    acc = _dot(x_ref[...], w_ref[...])
    cos = cos_ref[...]
    sin = sin_ref[...]
    bm, bn = acc.shape
    for j in range(bn // HEAD_DIM):
        t = _rope(acc[:, j * HEAD_DIM:(j + 1) * HEAD_DIM], cos, sin).astype(o_ref.dtype)
        for r in range(bm // Q_BLOCK):
            o_ref[r, j] = t[r * Q_BLOCK:(r + 1) * Q_BLOCK, :]


def _proj_rope_headmajor(xn, w, col_starts, width, cos, sin, bm=1024, bn=1024, side=None):
    m, k = xn.shape
    n = width * len(col_starts)
    bm = min(bm, m)
    per = width // bn
    first, second = (c // bn for c in col_starts)

    def w_block(i, j):
        return 0, jnp.where(j < per, first + j, second + j - per)

    return _matmul_call(
        _proj_rope_hm_kernel, (m // bm, n // bn),
        [pl.BlockSpec((bm, k), lambda i, j: (i, 0)),
         pl.BlockSpec((k, bn), w_block),
         pl.BlockSpec((bm, HEAD_DIM), lambda i, j: (i, 0)),
         pl.BlockSpec((bm, HEAD_DIM), lambda i, j: (i, 0))],
        pl.BlockSpec((bm // Q_BLOCK, bn // HEAD_DIM, Q_BLOCK, HEAD_DIM), lambda i, j: (i, j, 0, 0)),
        jax.ShapeDtypeStruct((m // Q_BLOCK, n // HEAD_DIM, Q_BLOCK, HEAD_DIM), BF16),
        (xn, w, cos, sin), "proj_q", side=side)


def _proj_kv_kernel(x_ref, wk_ref, wki_ref, wv_ref, cos_ref, sin_ref, o_ref):
    x = x_ref[...]
    cos = cos_ref[...]
    sin = sin_ref[...]
    col = 0
    for w_ref, rotary in ((wk_ref, True), (wki_ref, True), (wv_ref, False)):
        acc = _dot(x, w_ref[...])
        for j in range(acc.shape[1] // HEAD_DIM):
            t = acc[:, j * HEAD_DIM:(j + 1) * HEAD_DIM]
            if rotary:
                t = _rope(t, cos, sin)
            o_ref[:, col:col + HEAD_DIM] = t.astype(o_ref.dtype)
            col += HEAD_DIM


def _proj_kv(xn, w, col_k, col_ki, col_v, cos, sin, bm=1024):
    m, k = xn.shape
    kvd = N_KV_A * HEAD_DIM
    n = 2 * kvd + HEAD_DIM
    bm = min(bm, m)
    return pl.pallas_call(
        _proj_kv_kernel,
        grid=(m // bm,),
        in_specs=[pl.BlockSpec((bm, k), lambda i: (i, 0)),
                  pl.BlockSpec((k, kvd), lambda i: (0, col_k // kvd)),
                  pl.BlockSpec((k, HEAD_DIM), lambda i: (0, col_ki // HEAD_DIM)),
                  pl.BlockSpec((k, kvd), lambda i: (0, col_v // kvd)),
                  pl.BlockSpec((bm, HEAD_DIM), lambda i: (i, 0)),
                  pl.BlockSpec((bm, HEAD_DIM), lambda i: (i, 0))],
        out_specs=pl.BlockSpec((bm, n), lambda i: (i, 0)),
        out_shape=jax.ShapeDtypeStruct((m, n), BF16),
        compiler_params=_cparams(("arbitrary",), 52),
        name="proj_kv",
    )(xn, w, w, w, cos, sin)


def _mm_kernel(x_ref, w_ref, o_ref):
    o_ref[...] = _dot(x_ref[...], w_ref[...]).astype(o_ref.dtype)


def _matmul(x, w, out_dtype, bm=1024, bn=1024, name="matmul", side=None):
    m, k = x.shape
    n = w.shape[1]
    bm = min(bm, m)
    return _matmul_call(
        _mm_kernel, (m // bm, n // bn),
        [pl.BlockSpec((bm, k), lambda i, j: (i, 0)),
         pl.BlockSpec((k, bn), lambda i, j: (0, j))],
        pl.BlockSpec((bm, bn), lambda i, j: (i, j)),
        jax.ShapeDtypeStruct((m, n), out_dtype), (x, w), name, side=side)


WORD_BITS = 32
SUM_ROWS = 16
KEYS_PER_WORD_GROUP = WORD_BITS * SUBLANES


def _bit_transpose32(words):
    a = list(words)
    j, m = 16, 0x0000FFFF
    while j:
        mask = jnp.int32(m - (1 << 32) if m >= (1 << 31) else m)
        k = 0
        while k < WORD_BITS:
            t = (a[k] ^ lax.shift_right_logical(a[k + j], jnp.full_like(a[k], j))) & mask
            a[k] = a[k] ^ t
            a[k + j] = a[k + j] ^ (t << j)
            k = (k + j + 1) & ~j
        j >>= 1
        m = (m ^ (m << j)) & 0xFFFFFFFF
    return a


def _popcount_rows(words):
    per_sublane = jnp.sum(lax.population_count(words).reshape(-1, SUBLANES, words.shape[1]), axis=0)
    return jnp.sum(per_sublane.astype(F32), axis=0, keepdims=True)


def _attn_kernel(q_ref, qi_ref, wi_ref, k_ref, ki_ref, vt_ref, o_ref,
                 plane_ref, sel_ref, bias_ref, m_ref, acc_ref, alpha_ref, s_ref, p_ref,
                 *, topk, tk, idx_bits):
    i = pl.program_id(1)
    nk = (i + 1) * Q_BLOCK
    nch = (nk + tk - 1) // tk
    n_words = plane_ref.shape[1]
    groups_per_chunk = tk // KEYS_PER_WORD_GROUP

    @pl.when(i == 0)
    def _():
        plane_ref[...] = jnp.zeros_like(plane_ref)

    qi = qi_ref[0].reshape(N_HEADS_IDX * Q_BLOCK, HEAD_DIM)
    wi_t = jnp.transpose(wi_ref[...]) * (HEAD_DIM ** -0.5)
    qpos = i * Q_BLOCK + lax.broadcasted_iota(I32, (tk, Q_BLOCK), 1)
    krow = lax.broadcasted_iota(I32, (tk, Q_BLOCK), 0)

    def score_chunk(c):
        off = pl.multiple_of(c * tk, tk)
        lg = _dot_nt(ki_ref[0, pl.ds(off, tk), :], qi)
        yield
        sc = jnp.zeros((tk, Q_BLOCK), F32)
        for h in range(N_HEADS_IDX):
            sc = sc + wi_t[h:h + 1, :] * jnp.maximum(lg[:, h * Q_BLOCK:(h + 1) * Q_BLOCK], 0.0)
        yield
        bits = pltpu.bitcast(sc, I32)
        key = bits ^ ((bits >> 31) & 0x7FFFFFFF)
        key = jnp.where(key == -1, 0, key)
        ukey = jnp.where(off + krow <= qpos, key ^ INT_MIN, 0)
        for gi in range(groups_per_chunk):
            base = gi * KEYS_PER_WORD_GROUP
            planes = _bit_transpose32(
                [ukey[base + t * SUBLANES:base + (t + 1) * SUBLANES, :] for t in range(WORD_BITS)])
            row0 = pl.multiple_of((c * groups_per_chunk + gi) * SUBLANES, SUBLANES)
            for b in range(WORD_BITS):
                plane_ref[b, pl.ds(row0, SUBLANES), :] = planes[WORD_BITS - 1 - b]
            yield

    def score_chunks(chunks):
        pieces = [score_chunk(c) for c in chunks]
        while pieces:
            pieces = [t for t in pieces if next(t, "done") != "done"]

    def score_quad(j, carry):
        score_chunks((4 * j, 4 * j + 1, 4 * j + 2, 4 * j + 3))
        return carry

    lax.fori_loop(0, nch // 4, score_quad, 0)

    @pl.when(nch % 4 >= 2)
    def _():
        base = (nch // 4) * 4
        score_chunks((base, base + 1))

    @pl.when(nch % 2 == 1)
    def _():
        score_chunks((nch - 1,))

    cand = jnp.full((n_words, Q_BLOCK), -1, I32)
    greater = jnp.zeros((n_words, Q_BLOCK), I32)
    cnt_gt = jnp.zeros((1, Q_BLOCK), F32)
    for b in range(WORD_BITS - 1, -1, -1):
        ones = cand & plane_ref[b]
        cnt = _popcount_rows(ones)
        take = cnt_gt + cnt >= topk
        greater = jnp.where(take, greater, greater | ones)
        cnt_gt = jnp.where(take, cnt_gt, cnt_gt + cnt)
        cand = jnp.where(take, ones, cand ^ ones)

    word_row = lax.broadcasted_iota(I32, (n_words, Q_BLOCK), 0)
    word_pos = (word_row >> 3) * KEYS_PER_WORD_GROUP + (word_row & (SUBLANES - 1))
    qcol = i * Q_BLOCK + lax.broadcasted_iota(I32, (n_words, Q_BLOCK), 1)

    def prefix(limit):
        nt = jnp.clip((limit - word_pos + (SUBLANES - 1)) >> 3, 0, WORD_BITS)
        top = lax.shift_right_arithmetic(jnp.full_like(nt, INT_MIN), jnp.maximum(nt, 1) - 1)
        return jnp.where(nt <= 0, 0, top)

    cand = cand & prefix(qcol + 1)
    need = topk - cnt_gt
    sel_ref[...] = greater | cand

    @pl.when(jnp.max(_popcount_rows(cand) - need) > 0)
    def _():
        def jbody(bi, jt):
            cj = jt | jnp.left_shift(jnp.int32(1), idx_bits - 1 - bi)
            return jnp.where(_popcount_rows(cand & prefix(cj)) <= need, cj, jt)
        jt = lax.fori_loop(0, idx_bits, jbody, jnp.zeros((1, Q_BLOCK), I32))
        sel_ref[...] = greater | (cand & prefix(jt))

    def bias_body(c, carry):
        off = pl.multiple_of(c * tk, tk)
        for gi in range(groups_per_chunk):
            row0 = pl.multiple_of((c * groups_per_chunk + gi) * SUBLANES, SUBLANES)
            w = sel_ref[pl.ds(row0, SUBLANES), :]
            for t in range(WORD_BITS):
                dst = pl.multiple_of(off + gi * KEYS_PER_WORD_GROUP + t * SUBLANES, SUBLANES)
                bias_ref[pl.ds(dst, SUBLANES), :] = jnp.where((w << t) < 0, 0.0, -jnp.inf)
        return carry

    lax.fori_loop(0, nch, bias_body, 0)

    n_rep = N_HEADS_A // N_KV_A
    cols = n_rep * Q_BLOCK
    q_all = q_ref[0].reshape(N_HEADS_A * Q_BLOCK, HEAD_DIM)
    scale2 = (HEAD_DIM ** -0.5) * LOG2_E
    m_ref[...] = jnp.full(m_ref.shape, -1e30, F32)
    acc_ref[...] = jnp.zeros(acc_ref.shape, F32)
    ones_rows = jnp.ones((SUM_ROWS, tk), BF16)

    def stage_qk(g, off):
        qg = q_all[g * cols:(g + 1) * cols, :]
        s_ref[g] = _dot_nt(k_ref[0, pl.ds(off, tk), g * HEAD_DIM:(g + 1) * HEAD_DIM], qg)

    def stage_max(g, off):
        for n in range(n_rep):
            csl = slice(n * Q_BLOCK, (n + 1) * Q_BLOCK)
            t = s_ref[g, :, csl] + bias_ref[pl.ds(off, tk), :]
            s_ref[g, :, csl] = t
            m_old = m_ref[g, :, csl]
            m_new = jnp.maximum(m_old, jnp.max(t, axis=0, keepdims=True))
            alpha_ref[g, :, csl] = jnp.exp2((m_old - m_new) * scale2)
            m_ref[g, :, csl] = m_new

    def stage_exp(g, off):
        for n in range(n_rep):
            csl = slice(n * Q_BLOCK, (n + 1) * Q_BLOCK)
            p_ref[g, :, csl] = jnp.exp2((s_ref[g, :, csl] - m_ref[g, 0:1, csl]) * scale2).astype(BF16)

    def stage_pv(g, off):
        v_ext = jnp.concatenate(
            [vt_ref[0, g * HEAD_DIM:(g + 1) * HEAD_DIM, pl.ds(off, tk)], ones_rows], axis=0)
        acc_ref[g] = alpha_ref[g, 0:1, :] * acc_ref[g] + _dot(v_ext, p_ref[g])

    stages = (stage_qk, stage_max, stage_exp, stage_pv)

    def attend_chunks(chunks):
        pieces = [(g, pl.multiple_of(c * tk, tk)) for c in chunks for g in range(N_KV_A)]
        for step in range(len(pieces) + len(stages) - 1):
            for k, (g, off) in enumerate(pieces):
                if 0 <= step - k < len(stages):
                    stages[step - k](g, off)

    def att_quad(j, carry):
        attend_chunks((4 * j, 4 * j + 1, 4 * j + 2, 4 * j + 3))
        return carry

    lax.fori_loop(0, nch // 4, att_quad, 0)

    @pl.when(nch % 4 >= 2)
    def _():
        base = (nch // 4) * 4
        attend_chunks((base, base + 1))

    @pl.when(nch % 2 == 1)
    def _():
        attend_chunks((nch - 1,))
    for g in range(N_KV_A):
        acc = acc_ref[g]
        o_t = acc[0:HEAD_DIM, :] / acc[HEAD_DIM:HEAD_DIM + 1, :]
        for n in range(n_rep):
            h = g * n_rep + n
            o_ref[:, h * HEAD_DIM:(h + 1) * HEAD_DIM] = jnp.transpose(
                o_t[:, n * Q_BLOCK:(n + 1) * Q_BLOCK]).astype(o_ref.dtype)


def _attention(qh, wi, kkiv, vt, batch, seq_len):
    nb = seq_len // Q_BLOCK
    topk = min(TOPK_MAX, seq_len // 4)
    tk = min(512, seq_len)
    kvd = N_KV_A * HEAD_DIM
    cols = N_HEADS_A // N_KV_A * Q_BLOCK
    single = pl.Buffered(1)
    return pl.pallas_call(
        functools.partial(_attn_kernel, topk=topk, tk=tk, idx_bits=seq_len.bit_length()),
        grid=(batch, nb),
        in_specs=[pl.BlockSpec((1, N_HEADS_A, Q_BLOCK, HEAD_DIM), lambda b, i: (b * nb + i, 0, 0, 0)),
                  pl.BlockSpec((1, N_HEADS_IDX, Q_BLOCK, HEAD_DIM), lambda b, i: (b * nb + i, 1, 0, 0)),
                  pl.BlockSpec((Q_BLOCK, LANES), lambda b, i: (b * nb + i, 0)),
                  pl.BlockSpec((1, seq_len, kvd), lambda b, i: (b, 0, 0), pipeline_mode=single),
                  pl.BlockSpec((1, seq_len, HEAD_DIM), lambda b, i: (b, 0, kvd // HEAD_DIM),
                               pipeline_mode=single),
                  pl.BlockSpec((1, kvd, seq_len), lambda b, i: (b, 0, 0), pipeline_mode=single)],
        out_specs=pl.BlockSpec((Q_BLOCK, N_HEADS_A * HEAD_DIM), lambda b, i: (b * nb + i, 0)),
        out_shape=jax.ShapeDtypeStruct((batch * seq_len, N_HEADS_A * HEAD_DIM), BF16),
        scratch_shapes=[pltpu.VMEM((WORD_BITS, seq_len // WORD_BITS, Q_BLOCK), I32),
                        pltpu.VMEM((seq_len // WORD_BITS, Q_BLOCK), I32),
                        pltpu.VMEM((seq_len, Q_BLOCK), F32),
                        pltpu.VMEM((N_KV_A, SUBLANES, cols), F32),
                        pltpu.VMEM((N_KV_A, HEAD_DIM + SUM_ROWS, cols), F32),
                        pltpu.VMEM((N_KV_A, SUBLANES, cols), F32),
                        pltpu.VMEM((N_KV_A, tk, cols), F32),
                        pltpu.VMEM((N_KV_A, tk, cols), BF16)],
        compiler_params=_cparams(("arbitrary", "arbitrary"), 52),
        name="dsa_attention",
    )(qh, qh, wi, kkiv, kkiv, vt)


def _dot_hp(x, w_bf):
    hi = x.astype(BF16)
    r1 = x - hi.astype(F32)
    mid = r1.astype(BF16)
    lo = (r1 - mid.astype(F32)).astype(BF16)
    n = x.shape[0]
    parts = _dot(jnp.concatenate([hi, mid, lo], axis=0), w_bf)
    return parts[0:n] + parts[n:2 * n] + parts[2 * n:3 * n]


def _rwkv_kernel(r_ref, k_ref, v_ref, lw_ref, a_ref, g_ref, mu_ref, kk_ref, ka_ref, rk_ref,
                 lnw_ref, lnb_ref, bd_ref, ones_ref, ltri_ref, ts_ref, ti_ref, ic_ref, eye_ref,
                 o_ref, st_ref, prev_ref):
    @pl.when(pl.program_id(2) == 0)
    def _():
        st_ref[...] = jnp.zeros_like(st_ref)
        prev_ref[...] = jnp.zeros_like(prev_ref)

    pieces = [_rwkv_tile(tile, row0, r_ref, k_ref, v_ref, lw_ref, a_ref, g_ref, mu_ref, kk_ref, ka_ref,
                         rk_ref, lnw_ref, lnb_ref, bd_ref, ones_ref, ltri_ref, ts_ref, ti_ref, ic_ref,
                         eye_ref, o_ref, st_ref, prev_ref)
              for row0 in range(0, r_ref.shape[0], CHUNK)
              for tile in range(r_ref.shape[1] // TILE_B)]
    while pieces:
        pieces = [t for t in pieces if next(t, "done") != "done"]


def _rwkv_tile(tile, row0, r_ref, k_ref, v_ref, lw_ref, a_ref, g_ref, mu_ref, kk_ref, ka_ref, rk_ref,
               lnw_ref, lnb_ref, bd_ref, ones_ref, ltri_ref, ts_ref, ti_ref, ic_ref, eye_ref,
               o_ref, st_ref, prev_ref):
    cols = slice(tile * TILE_B, (tile + 1) * TILE_B)
    rows = slice(row0, row0 + CHUNK)
    bdm = bd_ref[...]
    row = lax.broadcasted_iota(I32, (CHUNK, TILE_B), 0)

    def shifted(x, slot):
        prev = prev_ref[slot:slot + 1, cols]
        prev_ref[slot:slot + 1, cols] = x[CHUNK - 1:CHUNK, :]
        return jnp.where(row == 0, prev, pltpu.roll(x, 1, axis=0))

    r0 = r_ref[rows, cols]
    k0 = k_ref[rows, cols]
    v0 = v_ref[rows, cols]
    r = r0 + (shifted(r0, 0) - r0) * mu_ref[0:1, cols]
    k = k0 + (shifted(k0, 1) - k0) * mu_ref[1:2, cols]
    v = v0 + (shifted(v0, 2) - v0) * mu_ref[2:3, cols]
    a = a_ref[rows, cols]
    ones_bd = ones_ref[...]

    kk = k * kk_ref[:, cols]
    k2 = k * (1.0 + (a - 1.0) * ka_ref[:, cols])
    head_sums = _dot_hp(jnp.concatenate([kk * kk, r * k2 * rk_ref[:, cols]], axis=0), ones_bd)
    ss = head_sums[0:CHUNK]
    bonus = head_sums[CHUNK:2 * CHUNK] * v
    yield
    kk = kk / jnp.maximum(jnp.sqrt(ss), 1e-12)
    aa = -kk
    bb = kk * a

    lw = lw_ref[rows, cols]
    cs = _cumsum_rows(lw, ltri_ref[...])
    yield
    tot = cs[CHUNK - 1:CHUNK, :]
    e_in = jnp.exp(cs)
    e_out = jnp.exp(-cs)
    e_tail = jnp.exp(tot - cs)
    at = aa * jnp.exp(cs - lw)
    rt = r * e_in
    bt = (bb * e_out)
    kt = (k2 * e_out)
    bh = (bb * e_tail).astype(BF16)
    kh = (k2 * e_tail).astype(BF16)
    wc = jnp.exp(tot)

    def bd(x):
        return jnp.concatenate([x.astype(BF16)] * HEADS_PER_TILE, axis=0) * ones_bd

    lhs = jnp.concatenate([at, rt], axis=0).astype(BF16)
    mb = _dot_nt(lhs, bd(bt))
    mk = _dot_nt(lhs, bd(kt))
    yield
    ts = ts_ref[...]
    ti = ti_ref[...]
    m_ab = mb[0:CHUNK] * ts
    m_rb = (mb[CHUNK:2 * CHUNK] * ti).astype(BF16)
    m_ak = (mk[0:CHUNK] * ts).astype(BF16)
    m_rk = (mk[CHUNK:2 * CHUNK] * ti).astype(BF16)

    n_round = CHUNK.bit_length() - 1
    t_inv = ic_ref[...] + m_ab
    m_pow = _dot(m_ab.astype(BF16), bd(m_ab))
    yield
    for rnd in range(1, n_round):
        last = rnd == n_round - 1
        lhs_rows = [t_inv] if last else [t_inv, m_pow]
        prod = _dot(jnp.concatenate(lhs_rows, axis=0).astype(BF16), bd(m_pow))
        yield
        t_inv = t_inv + prod[0:CHUNK]
        if not last:
            m_pow = prod[CHUNK:2 * CHUNK]
    t_bf = t_inv.astype(BF16)

    p = _dot(t_bf, bd(at))
    uv = _dot(jnp.concatenate([m_ak, m_rk], axis=0), bd(v))
    u = uv[0:CHUNK]
    yield
    q = _dot(t_bf, bd(u))
    rp = rt + _dot(m_rb, bd(p))
    yield
    y0 = _dot(m_rb, bd(q)) + uv[CHUNK:2 * CHUNK]

    yield
    a_t = _dot_tn(bh, p.astype(BF16)) * bdm + eye_ref[...] * wc
    d_t = _dot_tn(jnp.concatenate([bh, kh], axis=0),
                  jnp.concatenate([q.astype(BF16), v.astype(BF16)], axis=0)) * bdm
    st = st_ref[tile]
    from_state = _dot(jnp.concatenate([rp.astype(BF16), a_t.astype(BF16)], axis=0), st.astype(BF16))
    y = from_state[0:CHUNK] + y0
    st_ref[tile] = from_state[CHUNK:CHUNK + TILE_B] + d_t

    yield
    inv_n = 1.0 / HEAD_DIM_B
    mu = _dot_hp(y, ones_bd) * inv_n
    yield
    yc = y - mu
    var = _dot_hp(yc * yc, ones_bd) * inv_n
    yield
    yn =(yc * lax.rsqrt(var + GN_EPS)) * lnw_ref[:, cols] + lnb_ref[:, cols]
    o_ref[rows, cols] = ((yn + bonus) * g_ref[rows, cols]).astype(o_ref.dtype)


def _cumsum_rows(x, ltri_bf):
    hi = x.astype(BF16)
    r1 = x - hi.astype(F32)
    mid = r1.astype(BF16)
    lo = (r1 - mid.astype(F32)).astype(BF16)
    return _dot(ltri_bf, hi) + _dot(ltri_bf, mid) + _dot(ltri_bf, lo)


def _rwkv_consts():
    idx = jnp.arange(TILE_B)
    head = idx // HEAD_DIM_B
    bdm = (head[:, None] == head[None, :])
    t = jnp.arange(CHUNK)[:, None]
    s = (idx % CHUNK)[None, :]
    return dict(
        bd=bdm.astype(F32),
        ones=bdm.astype(BF16),
        ltri=(jnp.arange(CHUNK)[None, :] <= t).astype(BF16),
        ts=(s < t).astype(F32),
        ti=(s <= t).astype(F32),
        ic=(s == t).astype(F32),
        eye=jnp.eye(TILE_B, dtype=F32),
    )


def _rwkv(rkv, lw, a, g, mu_rkv, k_k, k_a, r_k, ln_w, ln_b, batch, seq_len):
    m, db = lw.shape
    block_rows = RWKV_CHUNKS_PER_STEP * CHUNK
    nc = seq_len // block_rows
    width = RWKV_TILES_PER_STEP * TILE_B
    ng = db // width
    cst = _rwkv_consts()
    blk = lambda off: pl.BlockSpec((block_rows, width), lambda b, h, c, off=off: (b * nc + c, off + h))
    par = lambda rows: pl.BlockSpec((rows, width), lambda b, h, c: (0, h))
    full = lambda arr: pl.BlockSpec(arr.shape, lambda b, h, c: (0, 0))
    row = lambda p: p.reshape(1, db)
    return pl.pallas_call(
        _rwkv_kernel,
        grid=(batch, ng, nc),
        in_specs=[blk(0), blk(ng), blk(2 * ng), blk(0), blk(0), blk(0),
                  par(3), par(1), par(1), par(1), par(1), par(1),
                  full(cst["bd"]), full(cst["ones"]), full(cst["ltri"]), full(cst["ts"]),
                  full(cst["ti"]), full(cst["ic"]), full(cst["eye"])],
        out_specs=blk(0),
        out_shape=jax.ShapeDtypeStruct((m, db), BF16),
        scratch_shapes=[pltpu.VMEM((RWKV_TILES_PER_STEP, TILE_B, TILE_B), F32),
                        pltpu.VMEM((SUBLANES, width), F32)],
        compiler_params=_cparams(("arbitrary", "arbitrary", "arbitrary"), 32),
        name="rwkv7",
    )(rkv, rkv, rkv, lw, a, g, mu_rkv, row(k_k), row(k_a), row(r_k), row(ln_w), row(ln_b),
      cst["bd"], cst["ones"], cst["ltri"], cst["ts"], cst["ti"], cst["ic"], cst["eye"])


def _merge_kernel(xn_ref, oa_ref, ob_ref, wga_ref, wgb_ref, bga_ref, bgb_ref, wpa_ref, wpb_ref, o_ref):
    xn = xn_ref[...]
    ga = _sigmoid(_dot(xn, wga_ref[...]) + bga_ref[...])
    gb = _sigmoid(_dot(xn, wgb_ref[...]) + bgb_ref[...])
    ya = _dot(oa_ref[...], wpa_ref[...])
    yb = _dot(ob_ref[...], wpb_ref[...])
    o_ref[...] = (ga * ya + gb * yb).astype(o_ref.dtype)


def _merge(xn, oa, ob, w_gate, b_gate, w_pa, w_pb, bm=1024, bn=256, side=None):
    m, d = xn.shape
    bm = min(bm, m)
    nj = d // bn
    return _matmul_call(
        _merge_kernel, (m // bm, nj),
        [pl.BlockSpec((bm, d), lambda i, j: (i, 0)),
         pl.BlockSpec((bm, oa.shape[1]), lambda i, j: (i, 0)),
         pl.BlockSpec((bm, ob.shape[1]), lambda i, j: (i, 0)),
         pl.BlockSpec((d, bn), lambda i, j: (0, j)),
         pl.BlockSpec((d, bn), lambda i, j: (0, j + nj)),
         pl.BlockSpec((1, bn), lambda i, j: (0, j)),
         pl.BlockSpec((1, bn), lambda i, j: (0, j + nj)),
         pl.BlockSpec((w_pa.shape[0], bn), lambda i, j: (0, j)),
         pl.BlockSpec((w_pb.shape[0], bn), lambda i, j: (0, j))],
        pl.BlockSpec((bm, bn), lambda i, j: (i, j)),
        jax.ShapeDtypeStruct((m, d), BF16),
        (xn, oa, ob, w_gate, w_gate, b_gate, b_gate, w_pa, w_pb), "gated_merge", vmem_mib=56, side=side)


def _mm_res_kernel(x_ref, w_ref, res_ref, o_ref):
    o_ref[...] = res_ref[...] + _dot(x_ref[...], w_ref[...])


def _matmul_residual(x, w, res, bm=1024, bn=1024, name="matmul_res", side=None):
    m, k = x.shape
    n = w.shape[1]
    bm = min(bm, m)
    return _matmul_call(
        _mm_res_kernel, (m // bm, n // bn),
        [pl.BlockSpec((bm, k), lambda i, j: (i, 0)),
         pl.BlockSpec((k, bn), lambda i, j: (0, j)),
         pl.BlockSpec((bm, bn), lambda i, j: (i, j))],
        pl.BlockSpec((bm, bn), lambda i, j: (i, j)),
        jax.ShapeDtypeStruct((m, n), F32), (x, w, res), name, side=side)


def _ffn_up_kernel(x_ref, w1_ref, w3_ref, o_ref):
    x = x_ref[...]
    h1 = _dot(x, w1_ref[...])
    h3 = _dot(x, w3_ref[...])
    o_ref[...] = (h1 * _sigmoid(h1) * h3).astype(o_ref.dtype)


def _ffn_up(x, w1, w3, bm=1024, bn=512, side=None):
    m, k = x.shape
    n = w1.shape[1]
    bm = min(bm, m)
    return _matmul_call(
        _ffn_up_kernel, (m // bm, pl.cdiv(n, bn)),
        [pl.BlockSpec((bm, k), lambda i, j: (i, 0)),
         pl.BlockSpec((k, bn), lambda i, j: (0, j)),
         pl.BlockSpec((k, bn), lambda i, j: (0, j))],
        pl.BlockSpec((bm, bn), lambda i, j: (i, j)),
        jax.ShapeDtypeStruct((m, n), BF16), (x, w1, w3), "ffn_up", side=side)


def _ple_kernel(h_ref, wg_ref, p_ref, wp_ref, gain_ref, o_ref, hn_ref, *, bn, final_norm):
    j = pl.program_id(1)
    n_row_chunks = h_ref.shape[0] // NORM_ROW_CHUNK

    def row_chunk(r):
        return pl.ds(pl.multiple_of(r * NORM_ROW_CHUNK, NORM_ROW_CHUNK), NORM_ROW_CHUNK)

    @pl.when(j == 0)
    def _():
        def body(r, carry):
            hn_ref[row_chunk(r), :] = _rms_rows(h_ref[row_chunk(r), :]).astype(BF16)
            return carry
        lax.fori_loop(0, n_row_chunks, body, 0)

    col = pl.multiple_of(j * bn, bn)
    gate = _sigmoid(_dot(hn_ref[...], wg_ref[...]))
    o_ref[:, pl.ds(col, bn)] = h_ref[:, pl.ds(col, bn)] + gate * _dot(p_ref[...], wp_ref[...])

    if final_norm:
        @pl.when(j == pl.num_programs(1) - 1)
        def _():
            def body(r, carry):
                o_ref[row_chunk(r), :] = _rms_rows(o_ref[row_chunk(r), :]) * gain_ref[...]
                return carry
            lax.fori_loop(0, n_row_chunks, body, 0)


def _ple(h, wg, p, wp, final_gain, bm=512, bn=512):
    m, d = h.shape
    bm = min(bm, m)
    final_norm = final_gain is not None
    gain = (final_gain if final_norm else jnp.ones((d,), F32)).reshape(1, d)
    return pl.pallas_call(
        functools.partial(_ple_kernel, bn=bn, final_norm=final_norm),
        grid=(m // bm, d // bn),
        in_specs=[pl.BlockSpec((bm, d), lambda i, j: (i, 0)),
                  pl.BlockSpec((d, bn), lambda i, j: (0, j)),
                  pl.BlockSpec((bm, p.shape[1]), lambda i, j: (i, 0)),
                  pl.BlockSpec((p.shape[1], bn), lambda i, j: (0, j)),
                  pl.BlockSpec((1, d), lambda i, j: (0, 0))],
        out_specs=pl.BlockSpec((bm, d), lambda i, j: (i, 0)),
        out_shape=jax.ShapeDtypeStruct((m, d), F32),
        scratch_shapes=[pltpu.VMEM((bm, d), BF16)],
        compiler_params=_cparams(("arbitrary", "arbitrary"), 52),
        name="ple",
    )(h, wg, p, wp, gain)


def _rmsnorm_kernel(x_ref, g_ref, o_ref):
    o_ref[...] = (_rms_rows(x_ref[...]) * g_ref[...]).astype(o_ref.dtype)


def _rmsnorm(x, gain, out_dtype, bm=512):
    m, d = x.shape
    bm = min(bm, m)
    row_spec = pl.BlockSpec((bm, d), lambda i: (i, 0))
    return pl.pallas_call(
        _rmsnorm_kernel,
        grid=(m // bm,),
        in_specs=[row_spec, pl.BlockSpec((1, d), lambda i: (0, 0))],
        out_specs=row_spec,
        out_shape=jax.ShapeDtypeStruct((m, d), out_dtype),
        compiler_params=_cparams(("arbitrary",), 48),
        name="rmsnorm",
    )(x, gain.reshape(1, d))


def _pad_to(a, axis, size):
    pad = [(0, 0)] * a.ndim
    pad[axis] = (0, size - a.shape[axis])
    return jnp.pad(a, pad)


def _layer(h2, p2, cos, sin, batch, seq_len, norm_mix, w_in, mu_rkv, mu_wag, w0, w1, w2, a0, a1, a2,
           g1, g2, k_k, k_a, r_k, ln_w, ln_b, w_pa, w_pb, w_gate, b_gate, w_o, norm_ffn,
           w_ffn1, w_ffn3, w_ffn2, w_ple_gate, w_ple, final_gain):
    d = h2.shape[1]
    d_a = N_HEADS_A * HEAD_DIM
    kvd = N_KV_A * HEAD_DIM
    d_iq = N_HEADS_IDX * HEAD_DIM
    d_b = w_pb.shape[0]
    o_q, o_k, o_v = 0, d_a, d_a + kvd
    o_qi = o_v + kvd
    o_ki = o_qi + d_iq
    o_wi = o_ki + HEAD_DIM
    o_r = o_wi + N_HEADS_IDX

    bf = lambda a: a.astype(BF16)
    w_in_bf = bf(w_in)
    assert d_a == d_iq and o_wi % LANES == 0
    w_rkv = w_in_bf[:, o_r:o_r + 3 * d_b]
    lora = LANES
    w1p, a1p = bf(_pad_to(w1, 1, lora)), bf(_pad_to(a1, 1, lora))
    w2p, a2p = bf(_pad_to(w2, 0, lora)), bf(_pad_to(a2, 0, lora))

    xn, lw, a, g, wi = _prep(h2, norm_mix.reshape(1, d), mu_wag, w1p, w2p, w0.reshape(1, d_b),
                             a1p, a2p, a0.reshape(1, d_b), bf(g1), bf(g2), w_in_bf, o_wi // LANES,
                             seq_len)

    qh, w_o_bf = _proj_rope_headmajor(xn, w_in_bf, (o_q, o_qi), d_a, cos, sin, side=w_o)
    kkiv = _proj_kv(xn, w_in_bf, o_k, o_ki, o_v, cos, sin)
    rkv, w_gate_bf = _matmul(xn, w_rkv, F32, bn=512, name="proj_rkv", side=w_gate)

    kkiv = kkiv.reshape(batch, seq_len, -1)
    vt = jnp.swapaxes(kkiv[:, :, kvd + HEAD_DIM:], 1, 2)
    o_att = _attention(qh, wi, kkiv, vt, batch, seq_len)

    o_rwkv = _rwkv(rkv, lw, a, g, mu_rkv, k_k, k_a, r_k.reshape(-1), ln_w, ln_b, batch, seq_len)

    mixed, w_ffn1_bf = _merge(xn, o_att, o_rwkv, w_gate_bf, b_gate.reshape(1, -1), bf(w_pa), bf(w_pb),
                              side=w_ffn1)
    h2, w_ffn3_bf = _matmul_residual(mixed, w_o_bf, h2, bn=512, name="out_proj", side=w_ffn3)

    xf = _rmsnorm(h2, norm_ffn, BF16)
    u, w_ffn2_bf = _ffn_up(xf, w_ffn1_bf, w_ffn3_bf, side=w_ffn2)
    h2, w_ple_gate_bf = _matmul_residual(u, w_ffn2_bf, h2, bm=512, bn=512, name="ffn_down",
                                         side=w_ple_gate)

    return _ple(h2, w_ple_gate_bf, bf(p2), bf(w_ple), final_gain)


def kernel(x, p, positions, norm_mix, w_in, mu_rkv, mu_wag, w0, w1, w2, a0, a1, a2, g1, g2, k_k, k_a,
           r_k, ln_w, ln_b, w_pa, w_pb, w_gate, b_gate, w_o, norm_ffn, w_ffn1, w_ffn3, w_ffn2,
           w_ple_gate, w_ple, norm_final):
    batch, seq_len, d = x.shape
    depth = p.shape[0]
    h2 = x.reshape(batch * seq_len, d)
    cos, sin = _rope_tables(positions)
    for i in range(depth):
        h2 = _layer(h2, p[i].reshape(batch * seq_len, -1), cos, sin, batch, seq_len,
                    norm_mix[i], w_in[i], mu_rkv[i], mu_wag[i], w0[i], w1[i], w2[i], a0[i], a1[i], a2[i],
                    g1[i], g2[i], k_k[i], k_a[i], r_k[i], ln_w[i], ln_b[i], w_pa[i], w_pb[i], w_gate[i],
                    b_gate[i], w_o[i], norm_ffn[i], w_ffn1[i], w_ffn3[i], w_ffn2[i], w_ple_gate[i],
                    w_ple[i], norm_final if i == depth - 1 else None)
    return h2.reshape(batch, seq_len, d)
```
